```python
import math
import jax, jax.numpy as jnp
from jax import lax
import numpy as np

D_MODEL = 1024
BATCH = 2
SEQ = 8192
DEPTH = 1
DEC_BATCH = 32
DEC_SEQ = 64
PAST_LEN = 4096

CHUNK = 64
D_POOL = 512
POOL_WINDOWS = (2, 4, 8, 16)
N_POOL_GROUPS = 4
POOL_GROUP = D_POOL // N_POOL_GROUPS
POOL_STATE = max(POOL_WINDOWS) - 1
DN_HEADS = 4
DN_DK = 128
DN_DV = 128
DN_QK = DN_HEADS * DN_DK
DN_VW = DN_HEADS * DN_DV
CONV_W = 4
CONV_CH = 2 * DN_QK + DN_VW
N_BRANCH = 2
OFF_U = 0
OFF_QKV = OFF_U + D_POOL
OFF_Z = OFF_QKV + CONV_CH
OFF_B = OFF_Z + DN_VW
OFF_A = OFF_B + DN_HEADS
OFF_G = OFF_A + DN_HEADS
D_IN = OFF_G + N_BRANCH * D_MODEL
PEER_HEADS = 8
PEER_KEYS = 128
PEER_N = PEER_KEYS * PEER_KEYS
PEER_DQ = 256
PEER_TOPK = 16
PEER_BLOCK = 128
EPS = 1e-6

kernel_name = "hybrid_pool_gdn_peer_stream_step"


def _rmsnorm(x, g):
    xf = x.astype(jnp.float32)
    y = xf * lax.rsqrt(jnp.mean(xf * xf, axis=-1, keepdims=True) + EPS)
    return (y * g.astype(jnp.float32)).astype(x.dtype)


def _l2norm(t):
    return t * lax.rsqrt(jnp.sum(t * t, axis=-1, keepdims=True) + EPS)


def _pool_mixer(u, prefix, pos0, w_grp, scale):
    B, L, _ = u.shape
    P = POOL_STATE
    ext = jnp.concatenate([prefix.astype(u.dtype), u], axis=1)
    cs = jnp.cumsum(ext.astype(jnp.float32), axis=1)
    cs = jnp.pad(cs, ((0, 0), (1, 0), (0, 0)))
    pos = pos0 + jnp.arange(L)
    uf = u.astype(jnp.float32)
    outs = []
    for gi, w in enumerate(POOL_WINDOWS):
        sl = slice(gi * POOL_GROUP, (gi + 1) * POOL_GROUP)
        s = cs[:, P + 1:P + 1 + L, sl] - cs[:, P + 1 - w:P + 1 - w + L, sl]
        cnt = jnp.minimum(pos + 1, w).astype(jnp.float32)[None, :, None]
        outs.append(s / cnt - uf[..., sl])
    pooled = jnp.stack(outs, axis=2)
    mixed = jnp.einsum('blgc,gcd->blgd', pooled, w_grp.astype(jnp.float32)).reshape(B, L, D_POOL)
    y = (mixed * scale.astype(jnp.float32)).astype(u.dtype)
    return y, ext[:, -P:]


def _causal_conv(xc, prefix, w):
    L = xc.shape[1]
    ext = jnp.concatenate([prefix.astype(xc.dtype), xc], axis=1)
    y = ext[:, 0:L] * w[0]
    for j in range(1, CONV_W):
        y = y + ext[:, j:j + L] * w[j]
    return jax.nn.silu(y), ext[:, -(CONV_W - 1):]


def _gated_delta(q, k, v, g, beta, S0):
    B, L, H, DK = q.shape
    DV = v.shape[-1]
    NC = -(-L // CHUNK)
    pad = NC * CHUNK - L

    def prep(t):
        t = jnp.pad(t, [(0, 0), (0, pad)] + [(0, 0)] * (t.ndim - 2))
        t = t.reshape((B, NC, CHUNK) + t.shape[2:])
        return jnp.moveaxis(t, 3, 1)

    q, k, v, g, beta = prep(q), prep(k), prep(v), prep(g), prep(beta)
    q = q * (DK ** -0.5)
    gc = jnp.cumsum(g, axis=-1)
    idx = jnp.arange(CHUNK)
    incl = idx[:, None] >= idx[None, :]
    strict = idx[:, None] > idx[None, :]
    decay = jnp.exp(jnp.where(incl, gc[..., :, None] - gc[..., None, :], -jnp.inf))
    kb = k * beta[..., None]
    A = jnp.where(strict, jnp.einsum('bhnid,bhnjd->bhnij', kb, k) * decay, 0.0)
    T = A + jnp.eye(CHUNK, dtype=jnp.float32)
    u_v = lax.linalg.triangular_solve(T, v * beta[..., None], left_side=True, lower=True, unit_diagonal=True)
    w_k = lax.linalg.triangular_solve(T, kb * jnp.exp(gc)[..., None], left_side=True, lower=True, unit_diagonal=True)
    qk = jnp.einsum('bhnid,bhnjd->bhnij', q, k) * decay
    gl = gc[..., -1]
    q_dec = q * jnp.exp(gc)[..., None]
    k_dec = k * jnp.exp(gl[..., None] - gc)[..., None]

    def step(S, xs):
        u_c, w_c, q_c, qk_c, k_c, gl_c = xs
        v_new = u_c - jnp.einsum('bhck,bhkv->bhcv', w_c, S)
        o = jnp.einsum('bhck,bhkv->bhcv', q_c, S) + jnp.einsum('bhij,bhjv->bhiv', qk_c, v_new)
        S = S * jnp.exp(gl_c)[..., None, None] + jnp.einsum('bhck,bhcv->bhkv', k_c, v_new)
        return S, o

    xs = tuple(jnp.moveaxis(t, 2, 0) for t in (u_v, w_k, q_dec, qk, k_dec, gl))
    S, o = lax.scan(step, S0, xs)
    o = jnp.transpose(o, (1, 0, 3, 2, 4)).reshape(B, NC * CHUNK, H, DV)[:, :L]
    return o, S


def _peer(xn, w_q, sub_keys, u_tab, v_tab):
    B, L, D = xn.shape
    T = B * L
    x2 = xn.reshape(T, D)
    q = jnp.einsum('td,dhq->thq', x2, w_q)
    half = PEER_DQ // 2
    s1 = jnp.einsum('thq,hkq->thk', q[..., :half], sub_keys[0]).astype(jnp.float32)
    s2 = jnp.einsum('thq,hkq->thk', q[..., half:], sub_keys[1]).astype(jnp.float32)
    v1, i1 = lax.top_k(s1, PEER_TOPK)
    v2, i2 = lax.top_k(s2, PEER_TOPK)
    cand = (v1[..., :, None] + v2[..., None, :]).reshape(T, PEER_HEADS, PEER_TOPK * PEER_TOPK)
    cv, ci = lax.top_k(cand, PEER_TOPK)
    e1 = jnp.take_along_axis(i1, ci // PEER_TOPK, axis=-1)
    e2 = jnp.take_along_axis(i2, ci % PEER_TOPK, axis=-1)
    experts = e1 * PEER_KEYS + e2
    gates = jax.nn.softmax(cv, axis=-1)
    nb = -(-T // PEER_BLOCK)
    pad = nb * PEER_BLOCK - T
    xp = jnp.pad(x2, ((0, pad), (0, 0))).reshape(nb, PEER_BLOCK, D)
    ep = jnp.pad(experts, ((0, pad), (0, 0), (0, 0))).reshape(nb, PEER_BLOCK, PEER_HEADS, PEER_TOPK)
    gp = jnp.pad(gates, ((0, pad), (0, 0), (0, 0))).reshape(nb, PEER_BLOCK, PEER_HEADS, PEER_TOPK)

    def blk(args):
        xb, eb, gb = args
        ub = u_tab[eb]
        vb = v_tab[eb]
        act = jax.nn.gelu(jnp.einsum('thkd,td->thk', ub, xb).astype(jnp.float32), approximate=False)
        return jnp.einsum('thk,thkd->td', (gb * act).astype(xb.dtype), vb)

    y = lax.map(blk, (xp, ep, gp))
    return y.reshape(nb * PEER_BLOCK, D)[:T].reshape(B, L, D)


def _layer(x, pool_prev, conv_prev, S_prev, pos0, g_mix, w_in, w_pool_grp, pool_scale, w_conv,
           a_log, dt_bias, g_dn_out, w_up_pool, w_up_dn, w_out, g_ffn, w_peer_q,
           peer_sub_keys, peer_u, peer_v):
    B, L, _ = x.shape
    h = _rmsnorm(x, g_mix)
    zc = jnp.einsum('bld,de->ble', h, w_in)
    u = zc[..., OFF_U:OFF_QKV]
    qkv = zc[..., OFF_QKV:OFF_Z]
    zg = zc[..., OFF_Z:OFF_B].astype(jnp.float32).reshape(B, L, DN_HEADS, DN_DV)
    b_raw = zc[..., OFF_B:OFF_A].astype(jnp.float32)
    a_raw = zc[..., OFF_A:OFF_G].astype(jnp.float32)
    gate_raw = zc[..., OFF_G:]
    ya, pool_new = _pool_mixer(u, pool_prev, pos0, w_pool_grp, pool_scale)
    qkv_c, conv_new = _causal_conv(qkv, conv_prev, w_conv)
    qkv_c = qkv_c.astype(jnp.float32)
    q = _l2norm(qkv_c[..., :DN_QK].reshape(B, L, DN_HEADS, DN_DK))
    k = _l2norm(qkv_c[..., DN_QK:2 * DN_QK].reshape(B, L, DN_HEADS, DN_DK))
    v = qkv_c[..., 2 * DN_QK:].reshape(B, L, DN_HEADS, DN_DV)
    beta = jax.nn.sigmoid(b_raw)
    g = -jnp.exp(a_log.astype(jnp.float32)) * jax.nn.softplus(a_raw + dt_bias.astype(jnp.float32))
    o, S_new = _gated_delta(q, k, v, g, beta, S_prev.astype(jnp.float32))
    o = _rmsnorm(o, g_dn_out) * jax.nn.silu(zg)
    yb = o.reshape(B, L, DN_VW).astype(x.dtype)
    ga = jax.nn.sigmoid(gate_raw[..., :D_MODEL])
    gb = jax.nn.sigmoid(gate_raw[..., D_MODEL:])
    merged = ga * jnp.einsum('blc,cd->bld', ya, w_up_pool) + gb * jnp.einsum('blc,cd->bld', yb, w_up_dn)
    x = x + jnp.einsum('bld,de->ble', merged, w_out)
    x = x + _peer(_rmsnorm(x, g_ffn), w_peer_q, peer_sub_keys, peer_u, peer_v)
    return x, pool_new, conv_new, S_new.astype(S_prev.dtype)


def setup_inputs(seed: int = 0) -> dict:
    key = jax.random.key(seed)
    ks = jax.random.split(key, 24)
    f32 = jnp.float32

    def nrm(k, shape, s):
        return jax.random.normal(k, shape, f32) * s

    return {
        "x_prompt": nrm(ks[0], (BATCH, SEQ, D_MODEL), 1.0),
        "x_sample": nrm(ks[1], (DEC_BATCH, DEC_SEQ, D_MODEL), 1.0),
        "cache_pool": nrm(ks[2], (DEPTH, DEC_BATCH, POOL_STATE, D_POOL), 1.0),
        "state_dn_conv": nrm(ks[3], (DEPTH, DEC_BATCH, CONV_W - 1, CONV_CH), 1.0),
        "state_dn": nrm(ks[4], (DEPTH, DEC_BATCH, DN_HEADS, DN_DK, DN_DV), 0.1),
        "g_mix": 1.0 + nrm(ks[5], (DEPTH, D_MODEL), 0.02),
        "w_in": nrm(ks[6], (DEPTH, D_MODEL, D_IN), D_MODEL ** -0.5),
        "w_pool_grp": nrm(ks[7], (DEPTH, N_POOL_GROUPS, POOL_GROUP, POOL_GROUP), POOL_GROUP ** -0.5),
        "pool_scale": 1.0 + nrm(ks[8], (DEPTH, D_POOL), 0.02),
        "w_conv": nrm(ks[9], (DEPTH, CONV_W, CONV_CH), CONV_W ** -0.5),
        "a_log": jnp.log(jax.random.uniform(ks[10], (DEPTH, DN_HEADS), f32, 1.0, 16.0)),
        "dt_bias": jax.random.uniform(ks[11], (DEPTH, DN_HEADS), f32, -5.0, -2.0),
        "g_dn_out": 1.0 + nrm(ks[12], (DEPTH, DN_DV), 0.02),
        "w_up_pool": nrm(ks[13], (DEPTH, D_POOL, D_MODEL), D_POOL ** -0.5),
        "w_up_dn": nrm(ks[14], (DEPTH, DN_VW, D_MODEL), DN_VW ** -0.5),
        "w_out": nrm(ks[15], (DEPTH, D_MODEL, D_MODEL), D_MODEL ** -0.5),
        "g_ffn": 1.0 + nrm(ks[16], (DEPTH, D_MODEL), 0.02),
        "w_peer_q": nrm(ks[17], (DEPTH, D_MODEL, PEER_HEADS, PEER_DQ), D_MODEL ** -0.5),
        "peer_sub_keys": nrm(ks[18], (DEPTH, 2, PEER_HEADS, PEER_KEYS, PEER_DQ // 2), (PEER_DQ // 2) ** -0.5),
        "peer_u": nrm(ks[19], (DEPTH, PEER_N, D_MODEL), D_MODEL ** -0.5),
        "peer_v": nrm(ks[20], (DEPTH, PEER_N, D_MODEL), (PEER_HEADS * PEER_TOPK) ** -0.5),
        "g_final": 1.0 + nrm(ks[21], (D_MODEL,), 0.02),
    }


def reference(x_prompt, x_sample, cache_pool, state_dn_conv, state_dn, g_mix, w_in, w_pool_grp,
              pool_scale, w_conv, a_log, dt_bias, g_dn_out, w_up_pool, w_up_dn, w_out, g_ffn,
              w_peer_q, peer_sub_keys, peer_u, peer_v, g_final):
    xp, xs = x_prompt, x_sample
    Bp = xp.shape[0]
    pool_p, conv_p, dn_p, pool_s, conv_s, dn_s = [], [], [], [], [], []
    for l in range(DEPTH):
        params = (g_mix[l], w_in[l], w_pool_grp[l], pool_scale[l], w_conv[l], a_log[l], dt_bias[l],
                  g_dn_out[l], w_up_pool[l], w_up_dn[l], w_out[l], g_ffn[l], w_peer_q[l],
                  peer_sub_keys[l], peer_u[l], peer_v[l])
        xp, pp, cp, sp = _layer(xp,
                                jnp.zeros((Bp, POOL_STATE, D_POOL), xp.dtype),
                                jnp.zeros((Bp, CONV_W - 1, CONV_CH), xp.dtype),
                                jnp.zeros((Bp, DN_HEADS, DN_DK, DN_DV), state_dn.dtype),
                                0, *params)
        xs, ps, cs_, ss = _layer(xs, cache_pool[l], state_dn_conv[l], state_dn[l], PAST_LEN, *params)
        pool_p.append(pp); conv_p.append(cp); dn_p.append(sp)
        pool_s.append(ps); conv_s.append(cs_); dn_s.append(ss)
    y_prompt = _rmsnorm(xp, g_final)
    y_sample = _rmsnorm(xs, g_final)
    return (y_prompt, y_sample, jnp.stack(pool_p), jnp.stack(conv_p), jnp.stack(dn_p),
            jnp.stack(pool_s), jnp.stack(conv_s), jnp.stack(dn_s))
```

```python
import functools

import jax
import jax.numpy as jnp
from jax import lax
from jax.experimental import pallas as pl
from jax.experimental.pallas import tpu as pltpu

D_MODEL = 1024
CHUNK = 64
D_POOL = 512
POOL_WINDOWS = (2, 4, 8, 16)
POOL_GROUP = 128
POOL_STATE = 15
DN_HEADS = 4
DN_DK = 128
DN_DV = 128
DN_QK = DN_HEADS * DN_DK
DN_VW = DN_HEADS * DN_DV
CONV_W = 4
CONV_CH = 2 * DN_QK + DN_VW
OFF_U = 0
OFF_QKV = OFF_U + D_POOL
OFF_Z = OFF_QKV + CONV_CH
OFF_B = OFF_Z + DN_VW
OFF_A = OFF_B + DN_HEADS
OFF_G = OFF_A + DN_HEADS
PEER_HEADS = 8
PEER_KEYS = 128
PEER_DQ = 256
PEER_TOPK = 16
PEER_PICKS = PEER_HEADS * PEER_TOPK
EPS = 1e-6

LANES = 128
SUBLANES = 8
POOL_PREFIX_ROWS = 16
CONV_PREFIX_ROWS = 8
VMEM_LIMIT = 56 * 1024 * 1024

F32 = jnp.float32
BF16 = jnp.bfloat16
HIGHEST = lax.Precision.HIGHEST


def _mm(a, b):
    return jnp.dot(a.astype(BF16), b.astype(BF16), preferred_element_type=F32)


def _mm_f32(a, b):
    return jnp.dot(a, b, precision=HIGHEST, preferred_element_type=F32)


def _mm_nt_f32(a, b):
    return lax.dot_general(a, b, (((1,), (1,)), ((), ())), precision=HIGHEST, preferred_element_type=F32)


def _mm_tn_f32(a, b):
    return lax.dot_general(a, b, (((0,), (0,)), ((), ())), precision=HIGHEST, preferred_element_type=F32)


def _sigmoid(x):
    return 1.0 / (1.0 + jnp.exp(-x))


def _silu(x):
    return x * _sigmoid(x)


def _softplus(x):
    return jnp.maximum(x, 0.0) + jnp.log1p(jnp.exp(-jnp.abs(x)))


def _rms(x, g):
    return x * lax.rsqrt(jnp.mean(x * x, axis=-1, keepdims=True) + EPS) * g


def _front_kernel(pos0, x_ref, pool_pre_ref, conv_pre_ref, g_mix_ref, w_u_ref, w_qkv_ref, w_z_ref, w_ba_ref, w_g_ref,
                  w_grp_ref, pool_scale_ref, w_conv_ref, a_log_ref, dt_ref,
                  ya_ref, q_ref, k_ref, v_ref, z_ref, graw_ref, bg_ref, pool_tail_ref, conv_tail_ref,
                  carry_u, carry_c):
    sb, tl, _ = x_ref.shape
    li = pl.program_id(1)

    @pl.when(li == 0)
    def _():
        carry_u[...] = pool_pre_ref[...]
        carry_c[...] = conv_pre_ref[...]

    x = x_ref[...].reshape(sb * tl, D_MODEL)
    h = _rms(x, g_mix_ref[...]).astype(BF16)
    u = jnp.dot(h, w_u_ref[...], preferred_element_type=F32)
    qkv = jnp.dot(h, w_qkv_ref[...], preferred_element_type=F32)
    z_ref[...] = jnp.dot(h, w_z_ref[...], preferred_element_type=F32).reshape(sb, tl, DN_VW)
    graw_ref[...] = jnp.dot(h, w_g_ref[...], preferred_element_type=F32).reshape(sb, tl, 2 * D_MODEL)
    ba = jnp.dot(h, w_ba_ref[...], preferred_element_type=F32)
    lane = lax.broadcasted_iota(jnp.int32, ba.shape, 1)
    beta = _sigmoid(ba)
    g = -jnp.exp(a_log_ref[...]) * _softplus(ba + dt_ref[...])
    bg_ref[...] = jnp.where(lane < DN_HEADS, beta, g).reshape(sb, tl, LANES)

    row = lax.broadcasted_iota(jnp.int32, (tl, POOL_GROUP), 0)
    pos1 = pos0 + li * tl + row + 1
    w_conv = w_conv_ref[...]

    for s in range(sb):
        u_s = u[s * tl:(s + 1) * tl]
        ext = jnp.concatenate([carry_u[s], u_s], axis=0)
        mixed = []
        for gi, w in enumerate(POOL_WINDOWS):
            acc = ext[:, gi * POOL_GROUP:(gi + 1) * POOL_GROUP]
            span = 1
            while span < w:
                acc = acc + pltpu.roll(acc, span, axis=0)
                span *= 2
            win = acc[POOL_PREFIX_ROWS:]
            cnt = jnp.minimum(pos1, w).astype(F32)
            pooled = win / cnt - u_s[:, gi * POOL_GROUP:(gi + 1) * POOL_GROUP]
            mixed.append(_mm(pooled, w_grp_ref[gi]))
        ya_ref[s] = jnp.concatenate(mixed, axis=1) * pool_scale_ref[...]
        pool_tail_ref[s] = ext[tl:]
        carry_u[s] = ext[tl:]

        c_s = qkv[s * tl:(s + 1) * tl]
        cext = jnp.concatenate([carry_c[s], c_s], axis=0)
        y = c_s * w_conv[CONV_W - 1:CONV_W]
        for j in range(1, CONV_W):
            y = y + pltpu.roll(cext, j, axis=0)[CONV_PREFIX_ROWS:] * w_conv[CONV_W - 1 - j:CONV_W - j]
        y = _silu(y)
        conv_tail_ref[s] = cext[tl:]
        carry_c[s] = cext[tl:]
        for hh in range(DN_HEADS):
            sl = slice(hh * DN_DK, (hh + 1) * DN_DK)
            qh = y[:, sl]
            q_ref[s, :, sl] = qh * lax.rsqrt(jnp.sum(qh * qh, axis=-1, keepdims=True) + EPS)
            kh = y[:, DN_QK + hh * DN_DK:DN_QK + (hh + 1) * DN_DK]
            k_ref[s, :, sl] = kh * lax.rsqrt(jnp.sum(kh * kh, axis=-1, keepdims=True) + EPS)
        v_ref[s] = y[:, 2 * DN_QK:]


def _front(x, pool_pre, conv_pre, pos0, sb, tl, g_mix, w_u, w_qkv, w_z, w_ba, w_g, w_grp, pool_scale, w_conv8,
           a_log_pad, dt_pad):
    b, l, _ = x.shape
    grid = (b // sb, l // tl)
    tok = lambda width: pl.BlockSpec((sb, tl, width), lambda i, j: (i, j, 0))
    seq = lambda rows, width: pl.BlockSpec((sb, rows, width), lambda i, j: (i, 0, 0))
    full = lambda a: pl.BlockSpec(a.shape, lambda i, j: (0,) * a.ndim)
    weights = (g_mix, w_u, w_qkv, w_z, w_ba, w_g, w_grp, pool_scale, w_conv8, a_log_pad, dt_pad)
    out_shape = (
        jax.ShapeDtypeStruct((b, l, D_POOL), F32),
        jax.ShapeDtypeStruct((b, l, DN_QK), F32),
        jax.ShapeDtypeStruct((b, l, DN_QK), F32),
        jax.ShapeDtypeStruct((b, l, DN_VW), F32),
        jax.ShapeDtypeStruct((b, l, DN_VW), F32),
        jax.ShapeDtypeStruct((b, l, 2 * D_MODEL), F32),
        jax.ShapeDtypeStruct((b, l, LANES), F32),
        jax.ShapeDtypeStruct((b, POOL_PREFIX_ROWS, D_POOL), F32),
        jax.ShapeDtypeStruct((b, CONV_PREFIX_ROWS, CONV_CH), F32),
    )
    return pl.pallas_call(
        functools.partial(_front_kernel, pos0),
        out_shape=out_shape,
        grid=grid,
        in_specs=[tok(D_MODEL), seq(POOL_PREFIX_ROWS, D_POOL), seq(CONV_PREFIX_ROWS, CONV_CH)] + [full(a) for a in weights],
        out_specs=(tok(D_POOL), tok(DN_QK), tok(DN_QK), tok(DN_VW), tok(DN_VW), tok(2 * D_MODEL), tok(LANES),
                   seq(POOL_PREFIX_ROWS, D_POOL), seq(CONV_PREFIX_ROWS, CONV_CH)),
        scratch_shapes=[pltpu.VMEM((sb, POOL_PREFIX_ROWS, D_POOL), F32), pltpu.VMEM((sb, CONV_PREFIX_ROWS, CONV_CH), F32)],
        compiler_params=pltpu.CompilerParams(dimension_semantics=("arbitrary", "arbitrary"), vmem_limit_bytes=VMEM_LIMIT),
        name="mixer_front",
    )(x, pool_pre, conv_pre, *weights)


def _delta_kernel(q_ref, k_ref, v_ref, z_ref, bg_ref, s0_ref, g_out_ref, yb_ref, s_out_ref, s_scr):
    ci = pl.program_id(1)

    @pl.when(ci == 0)
    def _():
        s_scr[...] = s0_ref[0]

    row = lax.broadcasted_iota(jnp.int32, (CHUNK, CHUNK), 0)
    col = lax.broadcasted_iota(jnp.int32, (CHUNK, CHUNK), 1)
    incl = row >= col
    strict = row > col
    bg = bg_ref[0]
    gc_all = _mm_f32(incl.astype(F32), bg)
    gc_rows = gc_all.T
    g_out = g_out_ref[...]

    for hh in range(DN_HEADS):
        sl = slice(hh * DN_DK, (hh + 1) * DN_DK)
        q = q_ref[0, :, sl] * (DN_DK ** -0.5)
        k = k_ref[0, :, sl]
        v = v_ref[0, :, sl]
        beta = bg[:, hh:hh + 1]
        gcol = gc_all[:, DN_HEADS + hh:DN_HEADS + hh + 1]
        grow = gc_rows[DN_HEADS + hh:DN_HEADS + hh + 1, :]
        decay = jnp.exp(jnp.where(incl, gcol - grow, -jnp.inf))
        egc = jnp.exp(gcol)
        kb = k * beta
        a = jnp.where(strict, _mm_nt_f32(kb, k) * decay, 0.0)
        xs = jnp.concatenate([v * beta, kb * egc], axis=1)
        pw = -a
        span = 1
        while True:
            xs = xs + _mm_f32(pw, xs)
            span *= 2
            if span >= CHUNK:
                break
            pw = _mm_f32(pw, pw)
        u_v = xs[:, :DN_DV]
        w_k = xs[:, DN_DV:]
        qk = _mm_nt_f32(q, k) * decay
        gl = gcol[CHUNK - 1:CHUNK, :]
        q_dec = q * egc
        k_dec = k * jnp.exp(gl - gcol)
        s = s_scr[hh]
        v_new = u_v - _mm_f32(w_k, s)
        o = _mm_f32(q_dec, s) + _mm_f32(qk, v_new)
        s_new = s * jnp.exp(gl) + _mm_tn_f32(k_dec, v_new)
        s_scr[hh] = s_new
        s_out_ref[0, hh] = s_new
        yb_ref[0, :, sl] = _rms(o, g_out) * _silu(z_ref[0, :, sl])


def _delta(q, k, v, z, bg, s0, g_out):
    b, l, _ = q.shape
    nc = l // CHUNK
    tok = lambda width: pl.BlockSpec((1, CHUNK, width), lambda i, j: (i, j, 0))
    st = pl.BlockSpec((1, DN_HEADS, DN_DK, DN_DV), lambda i, j: (i, 0, 0, 0))
    return pl.pallas_call(
        _delta_kernel,
        out_shape=(jax.ShapeDtypeStruct((b, l, DN_VW), F32), jax.ShapeDtypeStruct((b, DN_HEADS, DN_DK, DN_DV), F32)),
        grid=(b, nc),
        in_specs=[tok(DN_QK), tok(DN_QK), tok(DN_VW), tok(DN_VW), tok(LANES), st,
                  pl.BlockSpec((1, DN_DV), lambda i, j: (0, 0))],
        out_specs=(tok(DN_VW), st),
        scratch_shapes=[pltpu.VMEM((DN_HEADS, DN_DK, DN_DV), F32)],
        compiler_params=pltpu.CompilerParams(dimension_semantics=("arbitrary", "arbitrary"), vmem_limit_bytes=VMEM_LIMIT),
        name="delta_rule",
    )(q, k, v, z, bg, s0, g_out)


def _merge_kernel(x_ref, ya_ref, yb_ref, graw_ref, w_up_pool_ref, w_up_dn_ref, w_out_ref, g_ffn_ref, x2_ref, xn_ref):
    graw = graw_ref[...]
    ga = _sigmoid(graw[:, :D_MODEL])
    gb = _sigmoid(graw[:, D_MODEL:])
    merged = ga * _mm(ya_ref[...], w_up_pool_ref[...]) + gb * _mm(yb_ref[...], w_up_dn_ref[...])
    x2 = x_ref[...] + _mm(merged, w_out_ref[...])
    x2_ref[...] = x2
    xn_ref[...] = _rms(x2, g_ffn_ref[...])


def _merge(x, ya, yb, graw, w_up_pool, w_up_dn, w_out, g_ffn, tm):
    t = x.shape[0]
    tok = lambda width: pl.BlockSpec((tm, width), lambda i: (i, 0))
    full = lambda a: pl.BlockSpec(a.shape, lambda i: (0,) * a.ndim)
    weights = (w_up_pool, w_up_dn, w_out, g_ffn)
    return pl.pallas_call(
        _merge_kernel,
        out_shape=(jax.ShapeDtypeStruct((t, D_MODEL), F32), jax.ShapeDtypeStruct((t, D_MODEL), F32)),
        grid=(t // tm,),
        in_specs=[tok(D_MODEL), tok(D_POOL), tok(DN_VW), tok(2 * D_MODEL)] + [full(a) for a in weights],
        out_specs=(tok(D_MODEL), tok(D_MODEL)),
        compiler_params=pltpu.CompilerParams(dimension_semantics=("arbitrary",), vmem_limit_bytes=VMEM_LIMIT),
        name="branch_merge",
    )(x, ya, yb, graw, *weights)


def _top16_rows(s):
    n = s.shape[0]
    iota = lax.broadcasted_iota(jnp.int32, s.shape, 0)
    vals, idxs = [], []
    for _ in range(PEER_TOPK):
        m = jnp.max(s, axis=0, keepdims=True)
        idx = jnp.min(jnp.where(s == m, iota, n), axis=0, keepdims=True)
        vals.append(m)
        idxs.append(idx)
        s = jnp.where(iota == idx, -jnp.inf, s)
    return jnp.concatenate(vals, axis=0), jnp.concatenate(idxs, axis=0)


def _take_rows(table, idx):
    out = jnp.zeros_like(table)
    for a in range(PEER_TOPK):
        out = jnp.where(idx == a, table[a:a + 1, :], out)
    return out


def _route_kernel(xn_ref, wq_ref, keys_ref, experts_ref, gates_ref):
    half = PEER_DQ // 2
    q = jnp.dot(xn_ref[...].astype(BF16), wq_ref[...], preferred_element_type=F32).astype(BF16)
    experts, gates = [], []
    for hh in range(PEER_HEADS):
        q1 = q[:, hh * PEER_DQ:hh * PEER_DQ + half]
        q2 = q[:, hh * PEER_DQ + half:(hh + 1) * PEER_DQ]
        nt = (((1,), (1,)), ((), ()))
        s1 = lax.dot_general(keys_ref[0, hh], q1, nt, preferred_element_type=F32)
        s2 = lax.dot_general(keys_ref[1, hh], q2, nt, preferred_element_type=F32)
        v1, i1 = _top16_rows(s1)
        v2, i2 = _top16_rows(s2)
        cand = jnp.concatenate([v1[a:a + 1, :] + v2 for a in range(PEER_TOPK)], axis=0)
        cv, ci = _top16_rows(cand)
        e1 = _take_rows(i1, ci // PEER_TOPK)
        e2 = _take_rows(i2, ci % PEER_TOPK)
        experts.append(e1 * PEER_KEYS + e2)
        ex = jnp.exp(cv - cv[0:1, :])
        gates.append(ex / jnp.sum(ex, axis=0, keepdims=True))
    experts_ref[...] = jnp.concatenate(experts, axis=0).T
    gates_ref[...] = jnp.concatenate(gates, axis=0).T


def _route(xn, wq, keys, tm):
    t = xn.shape[0]
    return pl.pallas_call(
        _route_kernel,
        out_shape=(jax.ShapeDtypeStruct((t, PEER_PICKS), jnp.int32), jax.ShapeDtypeStruct((t, PEER_PICKS), F32)),
        grid=(t // tm,),
        in_specs=[pl.BlockSpec((tm, D_MODEL), lambda i: (i, 0)),
                  pl.BlockSpec(wq.shape, lambda i: (0, 0)),
                  pl.BlockSpec(keys.shape, lambda i: (0, 0, 0, 0))],
        out_specs=(pl.BlockSpec((tm, PEER_PICKS), lambda i: (i, 0)), pl.BlockSpec((tm, PEER_PICKS), lambda i: (i, 0))),
        compiler_params=pltpu.CompilerParams(dimension_semantics=("arbitrary",), vmem_limit_bytes=VMEM_LIMIT),
        name="peer_route",
    )(xn, wq, keys)


EXPERT_TOKENS = 16


def _experts_kernel(idx_ref, idx_next_ref, gates_ref, xn_ref, x2_ref, g_final_ref, tab_ref, out_ref, buf, sem):
    i = pl.program_id(0)
    n = pl.num_programs(0)
    tb = EXPERT_TOKENS
    slot = i % 2

    def row_copy(e, t, p, sl):
        return pltpu.make_async_copy(tab_ref.at[pl.ds(e, 1), :], buf.at[sl, p, pl.ds(t, 1), :], sem.at[sl])

    def issue(ref, sl):
        def body(t, carry):
            for p in range(PEER_PICKS):
                row_copy(ref[0, t, p], t, p, sl).start()
            return carry
        lax.fori_loop(0, tb, body, 0)

    @pl.when(i == 0)
    def _():
        issue(idx_ref, 0)

    @pl.when(i + 1 < n)
    def _():
        issue(idx_next_ref, 1 - slot)

    for p in range(PEER_PICKS):
        pltpu.make_async_copy(tab_ref.at[pl.ds(0, tb), :], buf.at[slot, p], sem.at[slot]).wait()

    xn = xn_ref[...]
    lane = lax.broadcasted_iota(jnp.int32, (tb, PEER_PICKS), 1)
    act = jnp.zeros((tb, PEER_PICKS), F32)
    for p in range(PEER_PICKS):
        s = jnp.sum(buf[slot, p, :, :D_MODEL] * xn, axis=-1, keepdims=True)
        act = jnp.where(lane == p, s, act)
    coef = gates_ref[...] * (0.5 * act * (1.0 + lax.erf(act * (0.5 ** 0.5))))
    y = jnp.zeros((tb, D_MODEL), F32)
    for p in range(PEER_PICKS):
        y = y + coef[:, p:p + 1] * buf[slot, p, :, D_MODEL:]
    out_ref[...] = _rms(x2_ref[...] + y, g_final_ref[...])


def _experts(experts, gates, xn, x2, g_final, table):
    t = xn.shape[0]
    tb = EXPERT_TOKENS
    nb = t // tb
    idx = experts.reshape(nb, tb, PEER_PICKS)
    tok = lambda width: pl.BlockSpec((tb, width), lambda i: (i, 0))
    return pl.pallas_call(
        _experts_kernel,
        out_shape=jax.ShapeDtypeStruct((t, D_MODEL), F32),
        grid=(nb,),
        in_specs=[pl.BlockSpec((1, tb, PEER_PICKS), lambda i: (i, 0, 0), memory_space=pltpu.SMEM),
                  pl.BlockSpec((1, tb, PEER_PICKS), lambda i: (jnp.minimum(i + 1, nb - 1), 0, 0), memory_space=pltpu.SMEM),
                  tok(PEER_PICKS), tok(D_MODEL), tok(D_MODEL),
                  pl.BlockSpec((1, D_MODEL), lambda i: (0, 0)),
                  pl.BlockSpec(memory_space=pl.ANY)],
        out_specs=tok(D_MODEL),
        scratch_shapes=[pltpu.VMEM((2, PEER_PICKS, tb, 2 * D_MODEL), F32), pltpu.SemaphoreType.DMA((2,))],
        compiler_params=pltpu.CompilerParams(dimension_semantics=("arbitrary",), vmem_limit_bytes=VMEM_LIMIT),
        name="peer_experts",
    )(idx, idx, gates, xn, x2, g_final, table)


def _pad_rows_front(a, rows):
    return jnp.pad(a, ((0, 0), (rows - a.shape[1], 0), (0, 0)))


def _group(x, pool_prev, conv_prev, s_prev, pos0, sb, tl, tm_merge, tm_route, p):
    b, l, _ = x.shape
    ya, q, k, v, z, graw, bg, pool_tail, conv_tail = _front(
        x, _pad_rows_front(pool_prev, POOL_PREFIX_ROWS), _pad_rows_front(conv_prev, CONV_PREFIX_ROWS), pos0, sb, tl,
        p["g_mix"], p["w_u"], p["w_qkv"], p["w_z"], p["w_ba"], p["w_g"], p["w_grp"], p["pool_scale"], p["w_conv8"],
        p["a_log_pad"], p["dt_pad"])
    yb, s_new = _delta(q, k, v, z, bg, s_prev, p["g_dn_out"])
    t = b * l
    flat = lambda a: a.reshape(t, a.shape[-1])
    x2, xn = _merge(flat(x), flat(ya), flat(yb), flat(graw), p["w_up_pool"], p["w_up_dn"], p["w_out"], p["g_ffn"], tm_merge)
    experts, gates = _route(xn, p["w_peer_q"], p["peer_keys"], tm_route)
    y = _experts(experts, gates, xn, x2, p["g_final"], p["peer_table"])
    return (y.reshape(b, l, D_MODEL), pool_tail[:, POOL_PREFIX_ROWS - POOL_STATE:],
            conv_tail[:, CONV_PREFIX_ROWS - (CONV_W - 1):], s_new)


def kernel(x_prompt, x_sample, cache_pool, state_dn_conv, state_dn, g_mix, w_in, w_pool_grp, pool_scale, w_conv, a_log,
           dt_bias, g_dn_out, w_up_pool, w_up_dn, w_out, g_ffn, w_peer_q, peer_sub_keys, peer_u, peer_v, g_final):
    depth = w_in.shape[0]
    assert depth == 1
    bp = x_prompt.shape[0]
    lane_pad = lambda a, off: jnp.pad(a.astype(F32)[None, :], ((0, 0), (off, LANES - off - a.shape[0])))
    w = w_in[0]
    params = {
        "g_mix": g_mix[0][None, :],
        "w_u": w[:, OFF_U:OFF_QKV].astype(BF16),
        "w_qkv": w[:, OFF_QKV:OFF_Z].astype(BF16),
        "w_z": w[:, OFF_Z:OFF_B].astype(BF16),
        "w_ba": jnp.pad(w[:, OFF_B:OFF_G], ((0, 0), (0, LANES - 2 * DN_HEADS))).astype(BF16),
        "w_g": w[:, OFF_G:].astype(BF16),
        "w_grp": w_pool_grp[0].astype(BF16),
        "pool_scale": pool_scale[0][None, :],
        "w_conv8": jnp.pad(w_conv[0], ((0, SUBLANES - CONV_W), (0, 0))),
        "a_log_pad": lane_pad(a_log[0], DN_HEADS),
        "dt_pad": lane_pad(dt_bias[0], DN_HEADS),
        "g_dn_out": g_dn_out[0][None, :],
        "w_up_pool": w_up_pool[0].astype(BF16),
        "w_up_dn": w_up_dn[0].astype(BF16),
        "w_out": w_out[0].astype(BF16),
        "g_ffn": g_ffn[0][None, :],
        "w_peer_q": w_peer_q[0].reshape(D_MODEL, PEER_HEADS * PEER_DQ).astype(BF16),
        "peer_keys": peer_sub_keys[0].astype(BF16),
        "peer_table": jnp.concatenate([peer_u[0], peer_v[0]], axis=1),
        "g_final": g_final[None, :],
    }
    zeros = lambda *shape: jnp.zeros(shape, F32)
    yp, pool_p, conv_p, dn_p = _group(
        x_prompt, zeros(bp, POOL_STATE, D_POOL), zeros(bp, CONV_W - 1, CONV_CH), zeros(bp, DN_HEADS, DN_DK, DN_DV),
        0, 1, min(512, x_prompt.shape[1]), 512, 256, params)
    ys, pool_s, conv_s, dn_s = _group(
        x_sample, cache_pool[0], state_dn_conv[0], state_dn[0].astype(F32),
        PAST_LEN, 8, x_sample.shape[1], 512, 256, params)
    return (yp, ys, pool_p[None], conv_p[None], dn_p[None].astype(state_dn.dtype),
            pool_s[None], conv_s[None], dn_s[None].astype(state_dn.dtype))


PAST_LEN = 4096
```

```python
import functools

import jax
import jax.numpy as jnp
from jax import lax
from jax.experimental import pallas as pl
from jax.experimental.pallas import tpu as pltpu

D_MODEL = 1024
CHUNK = 64
D_POOL = 512
POOL_WINDOWS = (2, 4, 8, 16)
POOL_GROUP = 128
POOL_STATE = 15
DN_HEADS = 4
DN_DK = 128
DN_DV = 128
DN_QK = DN_HEADS * DN_DK
DN_VW = DN_HEADS * DN_DV
CONV_W = 4
CONV_CH = 2 * DN_QK + DN_VW
OFF_U = 0
OFF_QKV = OFF_U + D_POOL
OFF_Z = OFF_QKV + CONV_CH
OFF_B = OFF_Z + DN_VW
OFF_A = OFF_B + DN_HEADS
OFF_G = OFF_A + DN_HEADS
PEER_HEADS = 8
PEER_KEYS = 128
PEER_DQ = 256
PEER_TOPK = 16
PEER_PICKS = PEER_HEADS * PEER_TOPK
EPS = 1e-6
PAST_LEN = 4096

LANES = 128
SUBLANES = 8
POOL_PREFIX_ROWS = 16
CONV_PREFIX_ROWS = 8
VMEM_LIMIT = 56 * 1024 * 1024

F32 = jnp.float32
BF16 = jnp.bfloat16
HIGHEST = lax.Precision.HIGHEST


def _mm(a, b):
    return jnp.dot(a.astype(BF16), b.astype(BF16), preferred_element_type=F32)


def _mm_f32(a, b):
    return jnp.dot(a, b, precision=HIGHEST, preferred_element_type=F32)


def _mm_nt_f32(a, b):
    return lax.dot_general(a, b, (((1,), (1,)), ((), ())), precision=HIGHEST, preferred_element_type=F32)


def _mm_tn_f32(a, b):
    return lax.dot_general(a, b, (((0,), (0,)), ((), ())), precision=HIGHEST, preferred_element_type=F32)


def _sigmoid(x):
    return 1.0 / (1.0 + jnp.exp(-x))


def _silu(x):
    return x * _sigmoid(x)


def _softplus(x):
    return jnp.maximum(x, 0.0) + jnp.log1p(jnp.exp(-jnp.abs(x)))


def _rms(x, g):
    return x * lax.rsqrt(jnp.mean(x * x, axis=-1, keepdims=True) + EPS) * g


def _front_kernel(pos0, x_ref, pool_pre_ref, conv_pre_ref, g_mix_ref, w_u_ref, w_qkv_ref, w_z_ref, w_ba_ref, w_g_ref,
                  w_grp_ref, pool_scale_ref, w_conv_ref, a_log_ref, dt_ref,
                  ya_ref, q_ref, k_ref, v_ref, z_ref, graw_ref, bg_ref, pool_tail_ref, conv_tail_ref,
                  carry_u, carry_c):
    sb, tl, _ = x_ref.shape
    li = pl.program_id(1)

    @pl.when(li == 0)
    def _():
        carry_u[...] = pool_pre_ref[...]
        carry_c[...] = conv_pre_ref[...]

    x = x_ref[...].reshape(sb * tl, D_MODEL)
    h = _rms(x, g_mix_ref[...]).astype(BF16)
    u = jnp.dot(h, w_u_ref[...], preferred_element_type=F32)
    qkv = jnp.dot(h, w_qkv_ref[...], preferred_element_type=F32)
    z_ref[...] = jnp.dot(h, w_z_ref[...], preferred_element_type=F32).reshape(sb, tl, DN_VW)
    graw_ref[...] = jnp.dot(h, w_g_ref[...], preferred_element_type=F32).reshape(sb, tl, 2 * D_MODEL)
    ba = jnp.dot(h, w_ba_ref[...], preferred_element_type=F32)
    lane = lax.broadcasted_iota(jnp.int32, ba.shape, 1)
    beta = _sigmoid(ba)
    g = -jnp.exp(a_log_ref[...]) * _softplus(ba + dt_ref[...])
    bg_ref[...] = jnp.where(lane < DN_HEADS, beta, g).reshape(sb, tl, LANES)

    row = lax.broadcasted_iota(jnp.int32, (tl, POOL_GROUP), 0)
    pos1 = pos0 + li * tl + row + 1
    w_conv = w_conv_ref[...]

    for s in range(sb):
        u_s = u[s * tl:(s + 1) * tl]
        ext = jnp.concatenate([carry_u[s], u_s], axis=0)
        mixed = []
        for gi, w in enumerate(POOL_WINDOWS):
            acc = ext[:, gi * POOL_GROUP:(gi + 1) * POOL_GROUP]
            span = 1
            while span < w:
                acc = acc + pltpu.roll(acc, span, axis=0)
                span *= 2
            win = acc[POOL_PREFIX_ROWS:]
            cnt = jnp.minimum(pos1, w).astype(F32)
            pooled = win / cnt - u_s[:, gi * POOL_GROUP:(gi + 1) * POOL_GROUP]
            mixed.append(_mm(pooled, w_grp_ref[gi]))
        ya_ref[s] = jnp.concatenate(mixed, axis=1) * pool_scale_ref[...]
        pool_tail_ref[s] = ext[tl:]
        carry_u[s] = ext[tl:]

        c_s = qkv[s * tl:(s + 1) * tl]
        cext = jnp.concatenate([carry_c[s], c_s], axis=0)
        y = c_s * w_conv[CONV_W - 1:CONV_W]
        for j in range(1, CONV_W):
            y = y + pltpu.roll(cext, j, axis=0)[CONV_PREFIX_ROWS:] * w_conv[CONV_W - 1 - j:CONV_W - j]
        y = _silu(y)
        conv_tail_ref[s] = cext[tl:]
        carry_c[s] = cext[tl:]
        for hh in range(DN_HEADS):
            sl = slice(hh * DN_DK, (hh + 1) * DN_DK)
            qh = y[:, sl]
            q_ref[s, :, sl] = qh * lax.rsqrt(jnp.sum(qh * qh, axis=-1, keepdims=True) + EPS)
            kh = y[:, DN_QK + hh * DN_DK:DN_QK + (hh + 1) * DN_DK]
            k_ref[s, :, sl] = kh * lax.rsqrt(jnp.sum(kh * kh, axis=-1, keepdims=True) + EPS)
        v_ref[s] = y[:, 2 * DN_QK:]


def _front(x, pool_pre, conv_pre, pos0, sb, tl, g_mix, w_u, w_qkv, w_z, w_ba, w_g, w_grp, pool_scale, w_conv8,
           a_log_pad, dt_pad):
    b, l, _ = x.shape
    grid = (b // sb, l // tl)
    tok = lambda width: pl.BlockSpec((sb, tl, width), lambda i, j: (i, j, 0))
    seq = lambda rows, width: pl.BlockSpec((sb, rows, width), lambda i, j: (i, 0, 0))
    full = lambda a: pl.BlockSpec(a.shape, lambda i, j: (0,) * a.ndim)
    weights = (g_mix, w_u, w_qkv, w_z, w_ba, w_g, w_grp, pool_scale, w_conv8, a_log_pad, dt_pad)
    out_shape = (
        jax.ShapeDtypeStruct((b, l, D_POOL), F32),
        jax.ShapeDtypeStruct((b, l, DN_QK), F32),
        jax.ShapeDtypeStruct((b, l, DN_QK), F32),
        jax.ShapeDtypeStruct((b, l, DN_VW), F32),
        jax.ShapeDtypeStruct((b, l, DN_VW), F32),
        jax.ShapeDtypeStruct((b, l, 2 * D_MODEL), F32),
        jax.ShapeDtypeStruct((b, l, LANES), F32),
        jax.ShapeDtypeStruct((b, POOL_PREFIX_ROWS, D_POOL), F32),
        jax.ShapeDtypeStruct((b, CONV_PREFIX_ROWS, CONV_CH), F32),
    )
    return pl.pallas_call(
        functools.partial(_front_kernel, pos0),
        out_shape=out_shape,
        grid=grid,
        in_specs=[tok(D_MODEL), seq(POOL_PREFIX_ROWS, D_POOL), seq(CONV_PREFIX_ROWS, CONV_CH)] + [full(a) for a in weights],
        out_specs=(tok(D_POOL), tok(DN_QK), tok(DN_QK), tok(DN_VW), tok(DN_VW), tok(2 * D_MODEL), tok(LANES),
                   seq(POOL_PREFIX_ROWS, D_POOL), seq(CONV_PREFIX_ROWS, CONV_CH)),
        scratch_shapes=[pltpu.VMEM((sb, POOL_PREFIX_ROWS, D_POOL), F32), pltpu.VMEM((sb, CONV_PREFIX_ROWS, CONV_CH), F32)],
        compiler_params=pltpu.CompilerParams(dimension_semantics=("arbitrary", "arbitrary"), vmem_limit_bytes=VMEM_LIMIT),
        name="mixer_front",
    )(x, pool_pre, conv_pre, *weights)


def _delta_kernel(q_ref, k_ref, v_ref, z_ref, bg_ref, s0_ref, g_out_ref, yb_ref, s_out_ref, s_scr):
    ci = pl.program_id(1)

    @pl.when(ci == 0)
    def _():
        s_scr[...] = s0_ref[0]

    row = lax.broadcasted_iota(jnp.int32, (CHUNK, CHUNK), 0)
    col = lax.broadcasted_iota(jnp.int32, (CHUNK, CHUNK), 1)
    incl = row >= col
    strict = row > col
    bg = bg_ref[0]
    gc_all = _mm_f32(incl.astype(F32), bg)
    gc_rows = gc_all.T
    g_out = g_out_ref[...]

    for hh in range(DN_HEADS):
        sl = slice(hh * DN_DK, (hh + 1) * DN_DK)
        q = q_ref[0, :, sl] * (DN_DK ** -0.5)
        k = k_ref[0, :, sl]
        v = v_ref[0, :, sl]
        beta = bg[:, hh:hh + 1]
        gcol = gc_all[:, DN_HEADS + hh:DN_HEADS + hh + 1]
        grow = gc_rows[DN_HEADS + hh:DN_HEADS + hh + 1, :]
        decay = jnp.exp(jnp.where(incl, gcol - grow, -jnp.inf))
        egc = jnp.exp(gcol)
        kb = k * beta
        a = jnp.where(strict, _mm_nt_f32(kb, k) * decay, 0.0)
        xs = jnp.concatenate([v * beta, kb * egc], axis=1)
        pw = -a
        span = 1
        while True:
            xs = xs + _mm_f32(pw, xs)
            span *= 2
            if span >= CHUNK:
                break
            pw = _mm_f32(pw, pw)
        u_v = xs[:, :DN_DV]
        w_k = xs[:, DN_DV:]
        qk = _mm_nt_f32(q, k) * decay
        gl = gcol[CHUNK - 1:CHUNK, :]
        q_dec = q * egc
        k_dec = k * jnp.exp(gl - gcol)
        s = s_scr[hh]
        v_new = u_v - _mm_f32(w_k, s)
        o = _mm_f32(q_dec, s) + _mm_f32(qk, v_new)
        s_new = s * jnp.exp(gl) + _mm_tn_f32(k_dec, v_new)
        s_scr[hh] = s_new
        s_out_ref[0, hh] = s_new
        yb_ref[0, :, sl] = _rms(o, g_out) * _silu(z_ref[0, :, sl])


def _delta(q, k, v, z, bg, s0, g_out):
    b, l, _ = q.shape
    nc = l // CHUNK
    tok = lambda width: pl.BlockSpec((1, CHUNK, width), lambda i, j: (i, j, 0))
    st = pl.BlockSpec((1, DN_HEADS, DN_DK, DN_DV), lambda i, j: (i, 0, 0, 0))
    return pl.pallas_call(
        _delta_kernel,
        out_shape=(jax.ShapeDtypeStruct((b, l, DN_VW), F32), jax.ShapeDtypeStruct((b, DN_HEADS, DN_DK, DN_DV), F32)),
        grid=(b, nc),
        in_specs=[tok(DN_QK), tok(DN_QK), tok(DN_VW), tok(DN_VW), tok(LANES), st,
                  pl.BlockSpec((1, DN_DV), lambda i, j: (0, 0))],
        out_specs=(tok(DN_VW), st),
        scratch_shapes=[pltpu.VMEM((DN_HEADS, DN_DK, DN_DV), F32)],
        compiler_params=pltpu.CompilerParams(dimension_semantics=("arbitrary", "arbitrary"), vmem_limit_bytes=VMEM_LIMIT),
        name="delta_rule",
    )(q, k, v, z, bg, s0, g_out)


def _merge_kernel(x_ref, ya_ref, yb_ref, graw_ref, w_up_pool_ref, w_up_dn_ref, w_out_ref, g_ffn_ref, x2_ref, xn_ref):
    graw = graw_ref[...]
    ga = _sigmoid(graw[:, :D_MODEL])
    gb = _sigmoid(graw[:, D_MODEL:])
    merged = ga * _mm(ya_ref[...], w_up_pool_ref[...]) + gb * _mm(yb_ref[...], w_up_dn_ref[...])
    x2 = x_ref[...] + _mm(merged, w_out_ref[...])
    x2_ref[...] = x2
    xn_ref[...] = _rms(x2, g_ffn_ref[...])


def _merge(x, ya, yb, graw, w_up_pool, w_up_dn, w_out, g_ffn, tm):
    t = x.shape[0]
    tok = lambda width: pl.BlockSpec((tm, width), lambda i: (i, 0))
    full = lambda a: pl.BlockSpec(a.shape, lambda i: (0,) * a.ndim)
    weights = (w_up_pool, w_up_dn, w_out, g_ffn)
    return pl.pallas_call(
        _merge_kernel,
        out_shape=(jax.ShapeDtypeStruct((t, D_MODEL), F32), jax.ShapeDtypeStruct((t, D_MODEL), F32)),
        grid=(t // tm,),
        in_specs=[tok(D_MODEL), tok(D_POOL), tok(DN_VW), tok(2 * D_MODEL)] + [full(a) for a in weights],
        out_specs=(tok(D_MODEL), tok(D_MODEL)),
        compiler_params=pltpu.CompilerParams(dimension_semantics=("arbitrary",), vmem_limit_bytes=VMEM_LIMIT),
        name="branch_merge",
    )(x, ya, yb, graw, *weights)


def _top16_rows(s):
    n = s.shape[0]
    iota = lax.broadcasted_iota(jnp.int32, s.shape, 0)
    vals, idxs = [], []
    for _ in range(PEER_TOPK):
        m = jnp.max(s, axis=0, keepdims=True)
        idx = jnp.min(jnp.where(s == m, iota, n), axis=0, keepdims=True)
        vals.append(m)
        idxs.append(idx)
        s = jnp.where(iota == idx, -jnp.inf, s)
    return jnp.concatenate(vals, axis=0), jnp.concatenate(idxs, axis=0)


def _take_rows(table, idx):
    out = jnp.zeros_like(table)
    for a in range(PEER_TOPK):
        out = jnp.where(idx == a, table[a:a + 1, :], out)
    return out


def _route_kernel(xn_ref, wq_ref, keys_ref, experts_ref, gates_ref):
    half = PEER_DQ // 2
    q = jnp.dot(xn_ref[...].astype(BF16), wq_ref[...], preferred_element_type=F32).astype(BF16)
    experts, gates = [], []
    for hh in range(PEER_HEADS):
        q1 = q[:, hh * PEER_DQ:hh * PEER_DQ + half]
        q2 = q[:, hh * PEER_DQ + half:(hh + 1) * PEER_DQ]
        nt = (((1,), (1,)), ((), ()))
        s1 = lax.dot_general(keys_ref[0, hh], q1, nt, preferred_element_type=F32)
        s2 = lax.dot_general(keys_ref[1, hh], q2, nt, preferred_element_type=F32)
        v1, i1 = _top16_rows(s1)
        v2, i2 = _top16_rows(s2)
        cand = jnp.concatenate([v1[a:a + 1, :] + v2 for a in range(PEER_TOPK)], axis=0)
        cv, ci = _top16_rows(cand)
        e1 = _take_rows(i1, ci // PEER_TOPK)
        e2 = _take_rows(i2, ci % PEER_TOPK)
        experts.append(e1 * PEER_KEYS + e2)
        ex = jnp.exp(cv - cv[0:1, :])
        gates.append(ex / jnp.sum(ex, axis=0, keepdims=True))
    experts_ref[...] = jnp.concatenate(experts, axis=0).T
    gates_ref[...] = jnp.concatenate(gates, axis=0).T


def _route(xn, wq, keys, tm):
    t = xn.shape[0]
    return pl.pallas_call(
        _route_kernel,
        out_shape=(jax.ShapeDtypeStruct((t, PEER_PICKS), jnp.int32), jax.ShapeDtypeStruct((t, PEER_PICKS), F32)),
        grid=(t // tm,),
        in_specs=[pl.BlockSpec((tm, D_MODEL), lambda i: (i, 0)),
                  pl.BlockSpec(wq.shape, lambda i: (0, 0)),
                  pl.BlockSpec(keys.shape, lambda i: (0, 0, 0, 0))],
        out_specs=(pl.BlockSpec((tm, PEER_PICKS), lambda i: (i, 0)), pl.BlockSpec((tm, PEER_PICKS), lambda i: (i, 0))),
        compiler_params=pltpu.CompilerParams(dimension_semantics=("arbitrary",), vmem_limit_bytes=VMEM_LIMIT),
        name="peer_route",
    )(xn, wq, keys)


EXPERT_TOKENS = 16


def _experts_kernel(idx_ref, idx_next_ref, gates_ref, xn_ref, x2_ref, g_final_ref, tab_ref, out_ref, buf, sem):
    i = pl.program_id(0)
    n = pl.num_programs(0)
    tb = EXPERT_TOKENS
    lane = lax.broadcasted_iota(jnp.int32, (PEER_HEADS, PEER_TOPK), 1)
    g_final = g_final_ref[...]

    def issue_token(ref, t, sl):
        for h in range(PEER_HEADS):
            for k in range(PEER_TOPK):
                e = ref[0, t, h * PEER_TOPK + k]
                pltpu.make_async_copy(tab_ref.at[e], buf.at[sl, t, k, pl.ds(h, 1), :], sem.at[sl]).start()

    def wait_slot(sl):
        for t in range(tb):
            pltpu.make_async_copy(buf.at[sl, t], buf.at[sl, t], sem.at[sl]).wait()

    def compute_token(t, sl):
        xb = jnp.broadcast_to(xn_ref[t:t + 1, :], (PEER_HEADS, D_MODEL))
        act = jnp.zeros((PEER_HEADS, PEER_TOPK), F32)
        for k in range(PEER_TOPK):
            s = jnp.sum(buf[sl, t, k, :, :D_MODEL] * xb, axis=-1, keepdims=True)
            act = jnp.where(lane == k, s, act)
        coef = gates_ref[t] * (0.5 * act * (1.0 + lax.erf(act * (0.5 ** 0.5))))
        acc = jnp.zeros((PEER_HEADS, D_MODEL), F32)
        for k in range(PEER_TOPK):
            acc = acc + coef[:, k:k + 1] * buf[sl, t, k, :, D_MODEL:]
        y = jnp.sum(acc, axis=0, keepdims=True)
        out_ref[t:t + 1, :] = _rms(x2_ref[t:t + 1, :] + y, g_final)

    @pl.when(i == 0)
    def _():
        for t in range(tb):
            issue_token(idx_ref, t, 0)

    def step(sl):
        wait_slot(sl)
        for t in range(tb):
            issue_token(idx_next_ref, t, 1 - sl)
            compute_token(t, sl)

        @pl.when(i == n - 1)
        def _():
            wait_slot(1 - sl)

    @pl.when(i % 2 == 0)
    def _():
        step(0)

    @pl.when(i % 2 == 1)
    def _():
        step(1)


def _experts(experts, gates, xn, x2, g_final, table):
    t = xn.shape[0]
    tb = EXPERT_TOKENS
    nb = t // tb
    idx = experts.reshape(nb, tb, PEER_PICKS)
    gates = gates.reshape(t, PEER_HEADS, PEER_TOPK)
    tok = lambda width: pl.BlockSpec((tb, width), lambda i: (i, 0))
    return pl.pallas_call(
        _experts_kernel,
        out_shape=jax.ShapeDtypeStruct((t, D_MODEL), F32),
        grid=(nb,),
        in_specs=[pl.BlockSpec((1, tb, PEER_PICKS), lambda i: (i, 0, 0), memory_space=pltpu.SMEM),
                  pl.BlockSpec((1, tb, PEER_PICKS), lambda i: (jnp.minimum(i + 1, nb - 1), 0, 0), memory_space=pltpu.SMEM),
                  pl.BlockSpec((tb, PEER_HEADS, PEER_TOPK), lambda i: (i, 0, 0)), tok(D_MODEL), tok(D_MODEL),
                  pl.BlockSpec((1, D_MODEL), lambda i: (0, 0)),
                  pl.BlockSpec(memory_space=pl.ANY)],
        out_specs=tok(D_MODEL),
        scratch_shapes=[pltpu.VMEM((2, tb, PEER_TOPK, PEER_HEADS, 2 * D_MODEL), F32), pltpu.SemaphoreType.DMA((2,))],
        compiler_params=pltpu.CompilerParams(dimension_semantics=("arbitrary",), vmem_limit_bytes=VMEM_LIMIT),
        name="peer_experts",
    )(idx, idx, gates, xn, x2, g_final, table)


def _pad_rows_front(a, rows):
    return jnp.pad(a, ((0, 0), (rows - a.shape[1], 0), (0, 0)))


MERGE_ROWS = 512
ROUTE_ROWS = 256
FRONT_ROWS = 512


def _group(x, pool_prev, conv_prev, s_prev, pos0, p):
    b, l, _ = x.shape
    tl = min(FRONT_ROWS, l)
    sb = min(b, FRONT_ROWS // tl)
    tm_merge = min(MERGE_ROWS, b * l)
    tm_route = min(ROUTE_ROWS, b * l)
    ya, q, k, v, z, graw, bg, pool_tail, conv_tail = _front(
        x, _pad_rows_front(pool_prev, POOL_PREFIX_ROWS), _pad_rows_front(conv_prev, CONV_PREFIX_ROWS), pos0, sb, tl,
        p["g_mix"], p["w_u"], p["w_qkv"], p["w_z"], p["w_ba"], p["w_g"], p["w_grp"], p["pool_scale"], p["w_conv8"],
        p["a_log_pad"], p["dt_pad"])
    yb, s_new = _delta(q, k, v, z, bg, s_prev, p["g_dn_out"])
    t = b * l
    flat = lambda a: a.reshape(t, a.shape[-1])
    x2, xn = _merge(flat(x), flat(ya), flat(yb), flat(graw), p["w_up_pool"], p["w_up_dn"], p["w_out"], p["g_ffn"], tm_merge)
    experts, gates = _route(xn, p["w_peer_q"], p["peer_keys"], tm_route)
    y = _experts(experts, gates, xn, x2, p["g_final"], p["peer_table"])
    return (y.reshape(b, l, D_MODEL), pool_tail[:, POOL_PREFIX_ROWS - POOL_STATE:],
            conv_tail[:, CONV_PREFIX_ROWS - (CONV_W - 1):], s_new)


def kernel(x_prompt, x_sample, cache_pool, state_dn_conv, state_dn, g_mix, w_in, w_pool_grp, pool_scale, w_conv, a_log,
           dt_bias, g_dn_out, w_up_pool, w_up_dn, w_out, g_ffn, w_peer_q, peer_sub_keys, peer_u, peer_v, g_final):
    depth = w_in.shape[0]
    assert depth == 1
    bp = x_prompt.shape[0]
    lane_pad = lambda a, off: jnp.pad(a.astype(F32)[None, :], ((0, 0), (off, LANES - off - a.shape[0])))
    w = w_in[0]
    params = {
        "g_mix": g_mix[0][None, :],
        "w_u": w[:, OFF_U:OFF_QKV].astype(BF16),
        "w_qkv": w[:, OFF_QKV:OFF_Z].astype(BF16),
        "w_z": w[:, OFF_Z:OFF_B].astype(BF16),
        "w_ba": jnp.pad(w[:, OFF_B:OFF_G], ((0, 0), (0, LANES - 2 * DN_HEADS))).astype(BF16),
        "w_g": w[:, OFF_G:].astype(BF16),
        "w_grp": w_pool_grp[0].astype(BF16),
        "pool_scale": pool_scale[0][None, :],
        "w_conv8": jnp.pad(w_conv[0], ((0, SUBLANES - CONV_W), (0, 0))),
        "a_log_pad": lane_pad(a_log[0], DN_HEADS),
        "dt_pad": lane_pad(dt_bias[0], DN_HEADS),
        "g_dn_out": g_dn_out[0][None, :],
        "w_up_pool": w_up_pool[0].astype(BF16),
        "w_up_dn": w_up_dn[0].astype(BF16),
        "w_out": w_out[0].astype(BF16),
        "g_ffn": g_ffn[0][None, :],
        "w_peer_q": w_peer_q[0].reshape(D_MODEL, PEER_HEADS * PEER_DQ).astype(BF16),
        "peer_keys": peer_sub_keys[0].astype(BF16),
        "peer_table": jnp.concatenate([peer_u[0][:, None, :], peer_v[0][:, None, :]], axis=2),
        "g_final": g_final[None, :],
    }
    zeros = lambda *shape: jnp.zeros(shape, F32)
    yp, pool_p, conv_p, dn_p = _group(
        x_prompt, zeros(bp, POOL_STATE, D_POOL), zeros(bp, CONV_W - 1, CONV_CH), zeros(bp, DN_HEADS, DN_DK, DN_DV),
        0, params)
    ys, pool_s, conv_s, dn_s = _group(
        x_sample, cache_pool[0], state_dn_conv[0], state_dn[0].astype(F32),
        PAST_LEN, params)
    return (yp, ys, pool_p[None], conv_p[None], dn_p[None].astype(state_dn.dtype),
            pool_s[None], conv_s[None], dn_s[None].astype(state_dn.dtype))
```

```python
import functools

import jax
import jax.numpy as jnp
from jax import lax
from jax.experimental import pallas as pl
from jax.experimental.pallas import tpu as pltpu

D_MODEL = 1024
CHUNK = 64
D_POOL = 512
POOL_WINDOWS = (2, 4, 8, 16)
POOL_GROUP = 128
POOL_STATE = 15
DN_HEADS = 4
DN_DK = 128
DN_DV = 128
DN_QK = DN_HEADS * DN_DK
DN_VW = DN_HEADS * DN_DV
CONV_W = 4
CONV_CH = 2 * DN_QK + DN_VW
OFF_U = 0
OFF_QKV = OFF_U + D_POOL
OFF_Z = OFF_QKV + CONV_CH
OFF_B = OFF_Z + DN_VW
OFF_A = OFF_B + DN_HEADS
OFF_G = OFF_A + DN_HEADS
PEER_HEADS = 8
PEER_KEYS = 128
PEER_DQ = 256
PEER_TOPK = 16
PEER_PICKS = PEER_HEADS * PEER_TOPK
EPS = 1e-6
PAST_LEN = 4096

LANES = 128
SUBLANES = 8
POOL_PREFIX_ROWS = 16
CONV_PREFIX_ROWS = 8
VMEM_LIMIT = 56 * 1024 * 1024

F32 = jnp.float32
BF16 = jnp.bfloat16
HIGHEST = lax.Precision.HIGHEST


def _mm(a, b):
    return jnp.dot(a.astype(BF16), b.astype(BF16), preferred_element_type=F32)


def _mm_f32(a, b):
    return jnp.dot(a, b, precision=HIGHEST, preferred_element_type=F32)


def _mm_nt_f32(a, b):
    return lax.dot_general(a, b, (((1,), (1,)), ((), ())), precision=HIGHEST, preferred_element_type=F32)


def _mm_tn_f32(a, b):
    return lax.dot_general(a, b, (((0,), (0,)), ((), ())), precision=HIGHEST, preferred_element_type=F32)


def _sigmoid(x):
    return 1.0 / (1.0 + jnp.exp(-x))


def _silu(x):
    return x * _sigmoid(x)


def _softplus(x):
    return jnp.maximum(x, 0.0) + jnp.log1p(jnp.exp(-jnp.abs(x)))


def _rms(x, g):
    return x * lax.rsqrt(jnp.mean(x * x, axis=-1, keepdims=True) + EPS) * g


def _front_kernel(pos0, x_ref, pool_pre_ref, conv_pre_ref, g_mix_ref, w_u_ref, w_qkv_ref, w_z_ref, w_ba_ref, w_g_ref,
                  w_grp_ref, pool_scale_ref, w_conv_ref, a_log_ref, dt_ref,
                  ya_ref, q_ref, k_ref, v_ref, z_ref, graw_ref, bg_ref, pool_tail_ref, conv_tail_ref,
                  carry_u, carry_c):
    sb, tl, _ = x_ref.shape
    li = pl.program_id(1)

    @pl.when(li == 0)
    def _():
        carry_u[...] = pool_pre_ref[...]
        carry_c[...] = conv_pre_ref[...]

    x = x_ref[...].reshape(sb * tl, D_MODEL)
    h = _rms(x, g_mix_ref[...]).astype(BF16)
    u = jnp.dot(h, w_u_ref[...], preferred_element_type=F32)
    qkv = jnp.dot(h, w_qkv_ref[...], preferred_element_type=F32)
    z_ref[...] = jnp.dot(h, w_z_ref[...], preferred_element_type=F32).reshape(sb, tl, DN_VW)
    graw_ref[...] = jnp.dot(h, w_g_ref[...], preferred_element_type=F32).reshape(sb, tl, 2 * D_MODEL)
    ba = jnp.dot(h, w_ba_ref[...], preferred_element_type=F32)
    lane = lax.broadcasted_iota(jnp.int32, ba.shape, 1)
    beta = _sigmoid(ba)
    g = -jnp.exp(a_log_ref[...]) * _softplus(ba + dt_ref[...])
    bg_ref[...] = jnp.where(lane < DN_HEADS, beta, g).reshape(sb, tl, LANES)

    row = lax.broadcasted_iota(jnp.int32, (tl, POOL_GROUP), 0)
    pos1 = pos0 + li * tl + row + 1
    w_conv = w_conv_ref[...]

    for s in range(sb):
        u_s = u[s * tl:(s + 1) * tl]
        ext = jnp.concatenate([carry_u[s], u_s], axis=0)
        mixed = []
        for gi, w in enumerate(POOL_WINDOWS):
            acc = ext[:, gi * POOL_GROUP:(gi + 1) * POOL_GROUP]
            span = 1
            while span < w:
                acc = acc + pltpu.roll(acc, span, axis=0)
                span *= 2
            win = acc[POOL_PREFIX_ROWS:]
            cnt = jnp.minimum(pos1, w).astype(F32)
            pooled = win / cnt - u_s[:, gi * POOL_GROUP:(gi + 1) * POOL_GROUP]
            mixed.append(_mm(pooled, w_grp_ref[gi]))
        ya_ref[s] = jnp.concatenate(mixed, axis=1) * pool_scale_ref[...]
        pool_tail_ref[s] = ext[tl:]
        carry_u[s] = ext[tl:]

        c_s = qkv[s * tl:(s + 1) * tl]
        cext = jnp.concatenate([carry_c[s], c_s], axis=0)
        y = c_s * w_conv[CONV_W - 1:CONV_W]
        for j in range(1, CONV_W):
            y = y + pltpu.roll(cext, j, axis=0)[CONV_PREFIX_ROWS:] * w_conv[CONV_W - 1 - j:CONV_W - j]
        y = _silu(y)
        conv_tail_ref[s] = cext[tl:]
        carry_c[s] = cext[tl:]
        for hh in range(DN_HEADS):
            sl = slice(hh * DN_DK, (hh + 1) * DN_DK)
            qh = y[:, sl]
            q_ref[s, :, sl] = qh * lax.rsqrt(jnp.sum(qh * qh, axis=-1, keepdims=True) + EPS)
            kh = y[:, DN_QK + hh * DN_DK:DN_QK + (hh + 1) * DN_DK]
            k_ref[s, :, sl] = kh * lax.rsqrt(jnp.sum(kh * kh, axis=-1, keepdims=True) + EPS)
        v_ref[s] = y[:, 2 * DN_QK:]


def _front(x, pool_pre, conv_pre, pos0, sb, tl, g_mix, w_u, w_qkv, w_z, w_ba, w_g, w_grp, pool_scale, w_conv8,
           a_log_pad, dt_pad):
    b, l, _ = x.shape
    grid = (b // sb, l // tl)
    tok = lambda width: pl.BlockSpec((sb, tl, width), lambda i, j: (i, j, 0))
    seq = lambda rows, width: pl.BlockSpec((sb, rows, width), lambda i, j: (i, 0, 0))
    full = lambda a: pl.BlockSpec(a.shape, lambda i, j: (0,) * a.ndim)
    weights = (g_mix, w_u, w_qkv, w_z, w_ba, w_g, w_grp, pool_scale, w_conv8, a_log_pad, dt_pad)
    out_shape = (
        jax.ShapeDtypeStruct((b, l, D_POOL), F32),
        jax.ShapeDtypeStruct((b, l, DN_QK), F32),
        jax.ShapeDtypeStruct((b, l, DN_QK), F32),
        jax.ShapeDtypeStruct((b, l, DN_VW), F32),
        jax.ShapeDtypeStruct((b, l, DN_VW), F32),
        jax.ShapeDtypeStruct((b, l, 2 * D_MODEL), F32),
        jax.ShapeDtypeStruct((b, l, LANES), F32),
        jax.ShapeDtypeStruct((b, POOL_PREFIX_ROWS, D_POOL), F32),
        jax.ShapeDtypeStruct((b, CONV_PREFIX_ROWS, CONV_CH), F32),
    )
    return pl.pallas_call(
        functools.partial(_front_kernel, pos0),
        out_shape=out_shape,
        grid=grid,
        in_specs=[tok(D_MODEL), seq(POOL_PREFIX_ROWS, D_POOL), seq(CONV_PREFIX_ROWS, CONV_CH)] + [full(a) for a in weights],
        out_specs=(tok(D_POOL), tok(DN_QK), tok(DN_QK), tok(DN_VW), tok(DN_VW), tok(2 * D_MODEL), tok(LANES),
                   seq(POOL_PREFIX_ROWS, D_POOL), seq(CONV_PREFIX_ROWS, CONV_CH)),
        scratch_shapes=[pltpu.VMEM((sb, POOL_PREFIX_ROWS, D_POOL), F32), pltpu.VMEM((sb, CONV_PREFIX_ROWS, CONV_CH), F32)],
        compiler_params=pltpu.CompilerParams(dimension_semantics=("arbitrary", "arbitrary"), vmem_limit_bytes=VMEM_LIMIT),
        name="mixer_front",
    )(x, pool_pre, conv_pre, *weights)


def _delta_kernel(q_ref, k_ref, v_ref, z_ref, bg_ref, s0_ref, g_out_ref, yb_ref, s_out_ref, s_scr):
    ci = pl.program_id(1)

    @pl.when(ci == 0)
    def _():
        s_scr[...] = s0_ref[0]

    row = lax.broadcasted_iota(jnp.int32, (CHUNK, CHUNK), 0)
    col = lax.broadcasted_iota(jnp.int32, (CHUNK, CHUNK), 1)
    incl = row >= col
    strict = row > col
    bg = bg_ref[0]
    gc_all = _mm_f32(incl.astype(F32), bg)
    gc_rows = gc_all.T
    g_out = g_out_ref[...]

    for hh in range(DN_HEADS):
        sl = slice(hh * DN_DK, (hh + 1) * DN_DK)
        q = q_ref[0, :, sl] * (DN_DK ** -0.5)
        k = k_ref[0, :, sl]
        v = v_ref[0, :, sl]
        beta = bg[:, hh:hh + 1]
        gcol = gc_all[:, DN_HEADS + hh:DN_HEADS + hh + 1]
        grow = gc_rows[DN_HEADS + hh:DN_HEADS + hh + 1, :]
        decay = jnp.exp(jnp.where(incl, gcol - grow, -jnp.inf))
        egc = jnp.exp(gcol)
        kb = k * beta
        a = jnp.where(strict, _mm_nt_f32(kb, k) * decay, 0.0)
        xs = jnp.concatenate([v * beta, kb * egc], axis=1)
        pw = -a
        span = 1
        while True:
            xs = xs + _mm_f32(pw, xs)
            span *= 2
            if span >= CHUNK:
                break
            pw = _mm_f32(pw, pw)
        u_v = xs[:, :DN_DV]
        w_k = xs[:, DN_DV:]
        qk = _mm_nt_f32(q, k) * decay
        gl = gcol[CHUNK - 1:CHUNK, :]
        q_dec = q * egc
        k_dec = k * jnp.exp(gl - gcol)
        s = s_scr[hh]
        v_new = u_v - _mm_f32(w_k, s)
        o = _mm_f32(q_dec, s) + _mm_f32(qk, v_new)
        s_new = s * jnp.exp(gl) + _mm_tn_f32(k_dec, v_new)
        s_scr[hh] = s_new
        s_out_ref[0, hh] = s_new
        yb_ref[0, :, sl] = _rms(o, g_out) * _silu(z_ref[0, :, sl])


def _delta(q, k, v, z, bg, s0, g_out):
    b, l, _ = q.shape
    nc = l // CHUNK
    tok = lambda width: pl.BlockSpec((1, CHUNK, width), lambda i, j: (i, j, 0))
    st = pl.BlockSpec((1, DN_HEADS, DN_DK, DN_DV), lambda i, j: (i, 0, 0, 0))
    return pl.pallas_call(
        _delta_kernel,
        out_shape=(jax.ShapeDtypeStruct((b, l, DN_VW), F32), jax.ShapeDtypeStruct((b, DN_HEADS, DN_DK, DN_DV), F32)),
        grid=(b, nc),
        in_specs=[tok(DN_QK), tok(DN_QK), tok(DN_VW), tok(DN_VW), tok(LANES), st,
                  pl.BlockSpec((1, DN_DV), lambda i, j: (0, 0))],
        out_specs=(tok(DN_VW), st),
        scratch_shapes=[pltpu.VMEM((DN_HEADS, DN_DK, DN_DV), F32)],
        compiler_params=pltpu.CompilerParams(dimension_semantics=("arbitrary", "arbitrary"), vmem_limit_bytes=VMEM_LIMIT),
        name="delta_rule",
    )(q, k, v, z, bg, s0, g_out)


def _merge_kernel(x_ref, ya_ref, yb_ref, graw_ref, w_up_pool_ref, w_up_dn_ref, w_out_ref, g_ffn_ref, x2_ref, xn_ref):
    graw = graw_ref[...]
    ga = _sigmoid(graw[:, :D_MODEL])
    gb = _sigmoid(graw[:, D_MODEL:])
    merged = ga * _mm(ya_ref[...], w_up_pool_ref[...]) + gb * _mm(yb_ref[...], w_up_dn_ref[...])
    x2 = x_ref[...] + _mm(merged, w_out_ref[...])
    x2_ref[...] = x2
    xn_ref[...] = _rms(x2, g_ffn_ref[...])


def _merge(x, ya, yb, graw, w_up_pool, w_up_dn, w_out, g_ffn, tm):
    t = x.shape[0]
    tok = lambda width: pl.BlockSpec((tm, width), lambda i: (i, 0))
    full = lambda a: pl.BlockSpec(a.shape, lambda i: (0,) * a.ndim)
    weights = (w_up_pool, w_up_dn, w_out, g_ffn)
    return pl.pallas_call(
        _merge_kernel,
        out_shape=(jax.ShapeDtypeStruct((t, D_MODEL), F32), jax.ShapeDtypeStruct((t, D_MODEL), F32)),
        grid=(t // tm,),
        in_specs=[tok(D_MODEL), tok(D_POOL), tok(DN_VW), tok(2 * D_MODEL)] + [full(a) for a in weights],
        out_specs=(tok(D_MODEL), tok(D_MODEL)),
        compiler_params=pltpu.CompilerParams(dimension_semantics=("arbitrary",), vmem_limit_bytes=VMEM_LIMIT),
        name="branch_merge",
    )(x, ya, yb, graw, *weights)


def _top16_rows(s):
    n = s.shape[0]
    iota = lax.broadcasted_iota(jnp.int32, s.shape, 0)
    vals, idxs = [], []
    for _ in range(PEER_TOPK):
        m = jnp.max(s, axis=0, keepdims=True)
        idx = jnp.min(jnp.where(s == m, iota, n), axis=0, keepdims=True)
        vals.append(m)
        idxs.append(idx)
        s = jnp.where(iota == idx, -jnp.inf, s)
    return jnp.concatenate(vals, axis=0), jnp.concatenate(idxs, axis=0)


def _take_rows(table, idx):
    out = jnp.zeros_like(table)
    for a in range(PEER_TOPK):
        out = jnp.where(idx == a, table[a:a + 1, :], out)
    return out


def _route_kernel(xn_ref, wq_ref, keys_ref, experts_ref, gates_ref):
    half = PEER_DQ // 2
    q = jnp.dot(xn_ref[...].astype(BF16), wq_ref[...], preferred_element_type=F32).astype(BF16)
    experts, gates = [], []
    for hh in range(PEER_HEADS):
        q1 = q[:, hh * PEER_DQ:hh * PEER_DQ + half]
        q2 = q[:, hh * PEER_DQ + half:(hh + 1) * PEER_DQ]
        nt = (((1,), (1,)), ((), ()))
        s1 = lax.dot_general(keys_ref[0, hh], q1, nt, preferred_element_type=F32)
        s2 = lax.dot_general(keys_ref[1, hh], q2, nt, preferred_element_type=F32)
        v1, i1 = _top16_rows(s1)
        v2, i2 = _top16_rows(s2)
        cand = jnp.concatenate([v1[a:a + 1, :] + v2 for a in range(PEER_TOPK)], axis=0)
        cv, ci = _top16_rows(cand)
        e1 = _take_rows(i1, ci // PEER_TOPK)
        e2 = _take_rows(i2, ci % PEER_TOPK)
        experts.append(e1 * PEER_KEYS + e2)
        ex = jnp.exp(cv - cv[0:1, :])
        gates.append(ex / jnp.sum(ex, axis=0, keepdims=True))
    experts_ref[...] = jnp.concatenate(experts, axis=0).T
    gates_ref[...] = jnp.concatenate(gates, axis=0).T


def _route(xn, wq, keys, tm):
    t = xn.shape[0]
    return pl.pallas_call(
        _route_kernel,
        out_shape=(jax.ShapeDtypeStruct((t, PEER_PICKS), jnp.int32), jax.ShapeDtypeStruct((t, PEER_PICKS), F32)),
        grid=(t // tm,),
        in_specs=[pl.BlockSpec((tm, D_MODEL), lambda i: (i, 0)),
                  pl.BlockSpec(wq.shape, lambda i: (0, 0)),
                  pl.BlockSpec(keys.shape, lambda i: (0, 0, 0, 0))],
        out_specs=(pl.BlockSpec((tm, PEER_PICKS), lambda i: (i, 0)), pl.BlockSpec((tm, PEER_PICKS), lambda i: (i, 0))),
        compiler_params=pltpu.CompilerParams(dimension_semantics=("arbitrary",), vmem_limit_bytes=VMEM_LIMIT),
        name="peer_route",
    )(xn, wq, keys)


EXPERT_TOKENS = 16


def _experts_kernel(idx_ref, idx_next_ref, gates_ref, xn_ref, x2_ref, g_final_ref, tab_ref, out_ref, buf, sem):
    i = pl.program_id(0)
    n = pl.num_programs(0)
    tb = EXPERT_TOKENS
    lane = lax.broadcasted_iota(jnp.int32, (PEER_HEADS, PEER_TOPK), 1)
    g_final = g_final_ref[...]

    def issue_token(ref, t, sl):
        for h in range(PEER_HEADS):
            for k in range(PEER_TOPK):
                e = ref[0, t, h * PEER_TOPK + k]
                pltpu.async_copy(tab_ref.at[e], buf.at[sl, t, k, pl.ds(h, 1), :], sem.at[sl], priority=k % 2)

    def wait_slot(sl):
        for t in range(tb):
            pltpu.make_async_copy(buf.at[sl, t], buf.at[sl, t], sem.at[sl]).wait()

    def compute_token(t, sl):
        xb = jnp.broadcast_to(xn_ref[t:t + 1, :], (PEER_HEADS, D_MODEL))
        act = jnp.zeros((PEER_HEADS, PEER_TOPK), F32)
        for k in range(PEER_TOPK):
            s = jnp.sum(buf[sl, t, k, :, :D_MODEL] * xb, axis=-1, keepdims=True)
            act = jnp.where(lane == k, s, act)
        coef = gates_ref[t] * (0.5 * act * (1.0 + lax.erf(act * (0.5 ** 0.5))))
        acc = jnp.zeros((PEER_HEADS, D_MODEL), F32)
        for k in range(PEER_TOPK):
            acc = acc + coef[:, k:k + 1] * buf[sl, t, k, :, D_MODEL:]
        y = jnp.sum(acc, axis=0, keepdims=True)
        out_ref[t:t + 1, :] = _rms(x2_ref[t:t + 1, :] + y, g_final)

    @pl.when(i == 0)
    def _():
        for t in range(tb):
            issue_token(idx_ref, t, 0)

    def step(sl):
        wait_slot(sl)
        for t in range(tb):
            issue_token(idx_next_ref, t, 1 - sl)
            compute_token(t, sl)

        @pl.when(i == n - 1)
        def _():
            wait_slot(1 - sl)

    @pl.when(i % 2 == 0)
    def _():
        step(0)

    @pl.when(i % 2 == 1)
    def _():
        step(1)


def _experts(experts, gates, xn, x2, g_final, table):
    t = xn.shape[0]
    tb = EXPERT_TOKENS
    nb = t // tb
    idx = experts.reshape(nb, tb, PEER_PICKS)
    gates = gates.reshape(t, PEER_HEADS, PEER_TOPK)
    tok = lambda width: pl.BlockSpec((tb, width), lambda i: (i, 0))
    return pl.pallas_call(
        _experts_kernel,
        out_shape=jax.ShapeDtypeStruct((t, D_MODEL), F32),
        grid=(nb,),
        in_specs=[pl.BlockSpec((1, tb, PEER_PICKS), lambda i: (i, 0, 0), memory_space=pltpu.SMEM),
                  pl.BlockSpec((1, tb, PEER_PICKS), lambda i: (jnp.minimum(i + 1, nb - 1), 0, 0), memory_space=pltpu.SMEM),
                  pl.BlockSpec((tb, PEER_HEADS, PEER_TOPK), lambda i: (i, 0, 0)), tok(D_MODEL), tok(D_MODEL),
                  pl.BlockSpec((1, D_MODEL), lambda i: (0, 0)),
                  pl.BlockSpec(memory_space=pl.ANY)],
        out_specs=tok(D_MODEL),
        scratch_shapes=[pltpu.VMEM((2, tb, PEER_TOPK, PEER_HEADS, 2 * D_MODEL), F32), pltpu.SemaphoreType.DMA((2,))],
        compiler_params=pltpu.CompilerParams(dimension_semantics=("arbitrary",), vmem_limit_bytes=VMEM_LIMIT),
        name="peer_experts",
    )(idx, idx, gates, xn, x2, g_final, table)


def _pad_rows_front(a, rows):
    return jnp.pad(a, ((0, 0), (rows - a.shape[1], 0), (0, 0)))


MERGE_ROWS = 512
ROUTE_ROWS = 256
FRONT_ROWS = 512


def _group(x, pool_prev, conv_prev, s_prev, pos0, p):
    b, l, _ = x.shape
    tl = min(FRONT_ROWS, l)
    sb = min(b, FRONT_ROWS // tl)
    tm_merge = min(MERGE_ROWS, b * l)
    tm_route = min(ROUTE_ROWS, b * l)
    ya, q, k, v, z, graw, bg, pool_tail, conv_tail = _front(
        x, _pad_rows_front(pool_prev, POOL_PREFIX_ROWS), _pad_rows_front(conv_prev, CONV_PREFIX_ROWS), pos0, sb, tl,
        p["g_mix"], p["w_u"], p["w_qkv"], p["w_z"], p["w_ba"], p["w_g"], p["w_grp"], p["pool_scale"], p["w_conv8"],
        p["a_log_pad"], p["dt_pad"])
    yb, s_new = _delta(q, k, v, z, bg, s_prev, p["g_dn_out"])
    t = b * l
    flat = lambda a: a.reshape(t, a.shape[-1])
    x2, xn = _merge(flat(x), flat(ya), flat(yb), flat(graw), p["w_up_pool"], p["w_up_dn"], p["w_out"], p["g_ffn"], tm_merge)
    experts, gates = _route(xn, p["w_peer_q"], p["peer_keys"], tm_route)
    y = _experts(experts, gates, xn, x2, p["g_final"], p["peer_table"])
    return (y.reshape(b, l, D_MODEL), pool_tail[:, POOL_PREFIX_ROWS - POOL_STATE:],
            conv_tail[:, CONV_PREFIX_ROWS - (CONV_W - 1):], s_new)


def kernel(x_prompt, x_sample, cache_pool, state_dn_conv, state_dn, g_mix, w_in, w_pool_grp, pool_scale, w_conv, a_log,
           dt_bias, g_dn_out, w_up_pool, w_up_dn, w_out, g_ffn, w_peer_q, peer_sub_keys, peer_u, peer_v, g_final):
    depth = w_in.shape[0]
    assert depth == 1
    bp = x_prompt.shape[0]
    lane_pad = lambda a, off: jnp.pad(a.astype(F32)[None, :], ((0, 0), (off, LANES - off - a.shape[0])))
    w = w_in[0]
    params = {
        "g_mix": g_mix[0][None, :],
        "w_u": w[:, OFF_U:OFF_QKV].astype(BF16),
        "w_qkv": w[:, OFF_QKV:OFF_Z].astype(BF16),
        "w_z": w[:, OFF_Z:OFF_B].astype(BF16),
        "w_ba": jnp.pad(w[:, OFF_B:OFF_G], ((0, 0), (0, LANES - 2 * DN_HEADS))).astype(BF16),
        "w_g": w[:, OFF_G:].astype(BF16),
        "w_grp": w_pool_grp[0].astype(BF16),
        "pool_scale": pool_scale[0][None, :],
        "w_conv8": jnp.pad(w_conv[0], ((0, SUBLANES - CONV_W), (0, 0))),
        "a_log_pad": lane_pad(a_log[0], DN_HEADS),
        "dt_pad": lane_pad(dt_bias[0], DN_HEADS),
        "g_dn_out": g_dn_out[0][None, :],
        "w_up_pool": w_up_pool[0].astype(BF16),
        "w_up_dn": w_up_dn[0].astype(BF16),
        "w_out": w_out[0].astype(BF16),
        "g_ffn": g_ffn[0][None, :],
        "w_peer_q": w_peer_q[0].reshape(D_MODEL, PEER_HEADS * PEER_DQ).astype(BF16),
        "peer_keys": peer_sub_keys[0].astype(BF16),
        "peer_table": jnp.concatenate([peer_u[0][:, None, :], peer_v[0][:, None, :]], axis=2),
        "g_final": g_final[None, :],
    }
    zeros = lambda *shape: jnp.zeros(shape, F32)
    yp, pool_p, conv_p, dn_p = _group(
        x_prompt, zeros(bp, POOL_STATE, D_POOL), zeros(bp, CONV_W - 1, CONV_CH), zeros(bp, DN_HEADS, DN_DK, DN_DV),
        0, params)
    ys, pool_s, conv_s, dn_s = _group(
        x_sample, cache_pool[0], state_dn_conv[0], state_dn[0].astype(F32),
        PAST_LEN, params)
    return (yp, ys, pool_p[None], conv_p[None], dn_p[None].astype(state_dn.dtype),
            pool_s[None], conv_s[None], dn_s[None].astype(state_dn.dtype))
```

```python
import functools

import jax
import jax.numpy as jnp
from jax import lax
from jax.experimental import pallas as pl
from jax.experimental.pallas import tpu as pltpu

D_MODEL = 1024
CHUNK = 64
D_POOL = 512
POOL_WINDOWS = (2, 4, 8, 16)
POOL_GROUP = 128
POOL_STATE = 15
DN_HEADS = 4
DN_DK = 128
DN_DV = 128
DN_QK = DN_HEADS * DN_DK
DN_VW = DN_HEADS * DN_DV
CONV_W = 4
CONV_CH = 2 * DN_QK + DN_VW
OFF_U = 0
OFF_QKV = OFF_U + D_POOL
OFF_Z = OFF_QKV + CONV_CH
OFF_B = OFF_Z + DN_VW
OFF_A = OFF_B + DN_HEADS
OFF_G = OFF_A + DN_HEADS
PEER_HEADS = 8
PEER_KEYS = 128
PEER_DQ = 256
PEER_TOPK = 16
PEER_PICKS = PEER_HEADS * PEER_TOPK
EPS = 1e-6
PAST_LEN = 4096

LANES = 128
SUBLANES = 8
POOL_PREFIX_ROWS = 16
CONV_PREFIX_ROWS = 8
VMEM_LIMIT = 56 * 1024 * 1024

F32 = jnp.float32
BF16 = jnp.bfloat16
HIGHEST = lax.Precision.HIGHEST


def _mm(a, b):
    return jnp.dot(a.astype(BF16), b.astype(BF16), preferred_element_type=F32)


def _mm_f32(a, b):
    return jnp.dot(a, b, precision=HIGHEST, preferred_element_type=F32)


def _mm_nt_f32(a, b):
    return lax.dot_general(a, b, (((1,), (1,)), ((), ())), precision=HIGHEST, preferred_element_type=F32)


def _mm_tn_f32(a, b):
    return lax.dot_general(a, b, (((0,), (0,)), ((), ())), precision=HIGHEST, preferred_element_type=F32)


def _sigmoid(x):
    return 1.0 / (1.0 + jnp.exp(-x))


def _silu(x):
    return x * _sigmoid(x)


def _softplus(x):
    return jnp.maximum(x, 0.0) + jnp.log1p(jnp.exp(-jnp.abs(x)))


def _rms(x, g):
    return x * lax.rsqrt(jnp.mean(x * x, axis=-1, keepdims=True) + EPS) * g


def _front_kernel(pos0, x_ref, pool_pre_ref, conv_pre_ref, g_mix_ref, w_u_ref, w_qkv_ref, w_z_ref, w_ba_ref, w_g_ref,
                  w_grp_ref, pool_scale_ref, w_conv_ref, a_log_ref, dt_ref,
                  ya_ref, q_ref, k_ref, v_ref, z_ref, graw_ref, bg_ref, pool_tail_ref, conv_tail_ref,
                  carry_u, carry_c):
    sb, tl, _ = x_ref.shape
    li = pl.program_id(1)

    @pl.when(li == 0)
    def _():
        carry_u[...] = pool_pre_ref[...]
        carry_c[...] = conv_pre_ref[...]

    x = x_ref[...].reshape(sb * tl, D_MODEL)
    h = _rms(x, g_mix_ref[...]).astype(BF16)
    u = jnp.dot(h, w_u_ref[...], preferred_element_type=F32)
    qkv = jnp.dot(h, w_qkv_ref[...], preferred_element_type=F32)
    z_ref[...] = jnp.dot(h, w_z_ref[...], preferred_element_type=F32).reshape(sb, tl, DN_VW)
    graw_ref[...] = jnp.dot(h, w_g_ref[...], preferred_element_type=F32).reshape(sb, tl, 2 * D_MODEL)
    ba = jnp.dot(h, w_ba_ref[...], preferred_element_type=F32)
    lane = lax.broadcasted_iota(jnp.int32, ba.shape, 1)
    beta = _sigmoid(ba)
    g = -jnp.exp(a_log_ref[...]) * _softplus(ba + dt_ref[...])
    bg_ref[...] = jnp.where(lane < DN_HEADS, beta, g).reshape(sb, tl, LANES)

    row = lax.broadcasted_iota(jnp.int32, (tl, POOL_GROUP), 0)
    pos1 = pos0 + li * tl + row + 1
    w_conv = w_conv_ref[...]

    for s in range(sb):
        u_s = u[s * tl:(s + 1) * tl]
        ext = jnp.concatenate([carry_u[s], u_s], axis=0)
        mixed = []
        for gi, w in enumerate(POOL_WINDOWS):
            acc = ext[:, gi * POOL_GROUP:(gi + 1) * POOL_GROUP]
            span = 1
            while span < w:
                acc = acc + pltpu.roll(acc, span, axis=0)
                span *= 2
            win = acc[POOL_PREFIX_ROWS:]
            cnt = jnp.minimum(pos1, w).astype(F32)
            pooled = win / cnt - u_s[:, gi * POOL_GROUP:(gi + 1) * POOL_GROUP]
            mixed.append(_mm(pooled, w_grp_ref[gi]))
        ya_ref[s] = jnp.concatenate(mixed, axis=1) * pool_scale_ref[...]
        pool_tail_ref[s] = ext[tl:]
        carry_u[s] = ext[tl:]

        c_s = qkv[s * tl:(s + 1) * tl]
        cext = jnp.concatenate([carry_c[s], c_s], axis=0)
        y = c_s * w_conv[CONV_W - 1:CONV_W]
        for j in range(1, CONV_W):
            y = y + pltpu.roll(cext, j, axis=0)[CONV_PREFIX_ROWS:] * w_conv[CONV_W - 1 - j:CONV_W - j]
        y = _silu(y)
        conv_tail_ref[s] = cext[tl:]
        carry_c[s] = cext[tl:]
        for hh in range(DN_HEADS):
            sl = slice(hh * DN_DK, (hh + 1) * DN_DK)
            qh = y[:, sl]
            q_ref[s, :, sl] = qh * lax.rsqrt(jnp.sum(qh * qh, axis=-1, keepdims=True) + EPS)
            kh = y[:, DN_QK + hh * DN_DK:DN_QK + (hh + 1) * DN_DK]
            k_ref[s, :, sl] = kh * lax.rsqrt(jnp.sum(kh * kh, axis=-1, keepdims=True) + EPS)
        v_ref[s] = y[:, 2 * DN_QK:]


def _front(x, pool_pre, conv_pre, pos0, sb, tl, g_mix, w_u, w_qkv, w_z, w_ba, w_g, w_grp, pool_scale, w_conv8,
           a_log_pad, dt_pad):
    b, l, _ = x.shape
    grid = (b // sb, l // tl)
    tok = lambda width: pl.BlockSpec((sb, tl, width), lambda i, j: (i, j, 0))
    seq = lambda rows, width: pl.BlockSpec((sb, rows, width), lambda i, j: (i, 0, 0))
    full = lambda a: pl.BlockSpec(a.shape, lambda i, j: (0,) * a.ndim)
    weights = (g_mix, w_u, w_qkv, w_z, w_ba, w_g, w_grp, pool_scale, w_conv8, a_log_pad, dt_pad)
    out_shape = (
        jax.ShapeDtypeStruct((b, l, D_POOL), F32),
        jax.ShapeDtypeStruct((b, l, DN_QK), F32),
        jax.ShapeDtypeStruct((b, l, DN_QK), F32),
        jax.ShapeDtypeStruct((b, l, DN_VW), F32),
        jax.ShapeDtypeStruct((b, l, DN_VW), F32),
        jax.ShapeDtypeStruct((b, l, 2 * D_MODEL), F32),
        jax.ShapeDtypeStruct((b, l, LANES), F32),
        jax.ShapeDtypeStruct((b, POOL_PREFIX_ROWS, D_POOL), F32),
        jax.ShapeDtypeStruct((b, CONV_PREFIX_ROWS, CONV_CH), F32),
    )
    return pl.pallas_call(
        functools.partial(_front_kernel, pos0),
        out_shape=out_shape,
        grid=grid,
        in_specs=[tok(D_MODEL), seq(POOL_PREFIX_ROWS, D_POOL), seq(CONV_PREFIX_ROWS, CONV_CH)] + [full(a) for a in weights],
        out_specs=(tok(D_POOL), tok(DN_QK), tok(DN_QK), tok(DN_VW), tok(DN_VW), tok(2 * D_MODEL), tok(LANES),
                   seq(POOL_PREFIX_ROWS, D_POOL), seq(CONV_PREFIX_ROWS, CONV_CH)),
        scratch_shapes=[pltpu.VMEM((sb, POOL_PREFIX_ROWS, D_POOL), F32), pltpu.VMEM((sb, CONV_PREFIX_ROWS, CONV_CH), F32)],
        compiler_params=pltpu.CompilerParams(dimension_semantics=("arbitrary", "arbitrary"), vmem_limit_bytes=VMEM_LIMIT),
        name="mixer_front",
    )(x, pool_pre, conv_pre, *weights)


_NN = (((1,), (0,)), ((), ()))
_NT = (((1,), (1,)), ((), ()))
_TN = (((0,), (0,)), ((), ()))


def _dot_bf16(a, b, dims):
    return lax.dot_general(a.astype(BF16), b.astype(BF16), dims, preferred_element_type=F32)


def _mm_nt(a, b):
    return _dot_bf16(a, b, _NT)


def _mm_chain(a, b):
    return _dot_bf16(a, b, _NN)


def _mm_state(a, b):
    return _dot_bf16(a, b, _NN)


def _mm_tn_state(a, b):
    return _dot_bf16(a, b, _TN)


def _delta_prep_kernel(q_ref, k_ref, v_ref, bg_ref, uv_ref, wk_ref, qd_ref, kd_ref, qk_ref, gl_ref):
    sb = q_ref.shape[0]
    ncb = q_ref.shape[1] // CHUNK
    row = lax.broadcasted_iota(jnp.int32, (CHUNK, CHUNK), 0)
    col = lax.broadcasted_iota(jnp.int32, (CHUNK, CHUNK), 1)
    incl = row >= col
    strict = row > col
    tri = incl.astype(F32)
    xs, pw = [], []
    for s in range(sb):
        for c in range(ncb):
            rows = slice(c * CHUNK, (c + 1) * CHUNK)
            bg = bg_ref[s, rows, :]
            gc_all = _mm_f32(tri, bg)
            gc_rows = gc_all.T
            gl_ref[s, c * SUBLANES:(c + 1) * SUBLANES, :] = gc_all[CHUNK - SUBLANES:, :]
            for hh in range(DN_HEADS):
                sl = slice(hh * DN_DK, (hh + 1) * DN_DK)
                q = q_ref[s, rows, sl] * (DN_DK ** -0.5)
                k = k_ref[s, rows, sl]
                beta = bg[:, hh:hh + 1]
                gcol = gc_all[:, DN_HEADS + hh:DN_HEADS + hh + 1]
                grow = gc_rows[DN_HEADS + hh:DN_HEADS + hh + 1, :]
                decay = jnp.exp(jnp.where(incl, gcol - grow, -jnp.inf))
                egc = jnp.exp(gcol)
                kb = k * beta
                pw.append(-jnp.where(strict, _mm_nt(kb, k) * decay, 0.0))
                xs.append(jnp.concatenate([v_ref[s, rows, sl] * beta, kb * egc], axis=1))
                qk_ref[s, rows, hh * CHUNK:(hh + 1) * CHUNK] = _mm_nt(q, k) * decay
                gl = gcol[CHUNK - 1:CHUNK, :]
                qd_ref[s, rows, sl] = q * egc
                kd_ref[s, rows, sl] = k * jnp.exp(gl - gcol)
    span = 1
    while True:
        xs = [x + _mm_chain(p, x) for p, x in zip(pw, xs)]
        span *= 2
        if span >= CHUNK:
            break
        pw = [_mm_chain(p, p) for p in pw]
    i = 0
    for s in range(sb):
        for c in range(ncb):
            rows = slice(c * CHUNK, (c + 1) * CHUNK)
            for hh in range(DN_HEADS):
                sl = slice(hh * DN_DK, (hh + 1) * DN_DK)
                uv_ref[s, rows, sl] = xs[i][:, :DN_DV]
                wk_ref[s, rows, sl] = xs[i][:, DN_DV:]
                i += 1


def _delta_scan_kernel(uv_ref, wk_ref, qd_ref, kd_ref, qk_ref, gl_ref, z_ref, s0_ref, g_out_ref, yb_ref, s_out_ref, s_scr):
    ci = pl.program_id(1)
    bb = uv_ref.shape[0]

    @pl.when(ci == 0)
    def _():
        s_scr[...] = s0_ref[...]

    g_out = g_out_ref[...]
    probs = [(b, hh) for b in range(bb) for hh in range(DN_HEADS)]
    lanes = lambda hh: slice(hh * DN_DK, (hh + 1) * DN_DK)
    s_old = [s_scr[b, hh] for b, hh in probs]
    v_new = [uv_ref[b, :, lanes(hh)] - _mm_state(wk_ref[b, :, lanes(hh)], s) for (b, hh), s in zip(probs, s_old)]
    o_state = [_mm_state(qd_ref[b, :, lanes(hh)], s) for (b, hh), s in zip(probs, s_old)]
    for (b, hh), s, vn, os_ in zip(probs, s_old, v_new, o_state):
        o = os_ + _mm_state(qk_ref[b, :, hh * CHUNK:(hh + 1) * CHUNK], vn)
        egl = jnp.exp(gl_ref[b, SUBLANES - 1:SUBLANES, DN_HEADS + hh:DN_HEADS + hh + 1])
        s_new = s * egl + _mm_tn_state(kd_ref[b, :, lanes(hh)], vn)
        s_scr[b, hh] = s_new
        s_out_ref[b, hh] = s_new
        yb_ref[b, :, lanes(hh)] = _rms(o, g_out) * _silu(z_ref[b, :, lanes(hh)])


DELTA_PREP_CHUNKS = 4
DELTA_SCAN_SEQS = 8


def _delta(q, k, v, z, bg, s0, g_out):
    b, l, _ = q.shape
    nc = l // CHUNK
    ncb = min(DELTA_PREP_CHUNKS, nc)
    sb = min(b, DELTA_PREP_CHUNKS // ncb)
    rows = ncb * CHUNK
    tok = lambda width: pl.BlockSpec((sb, rows, width), lambda i, j: (i, j, 0))
    glspec = pl.BlockSpec((sb, ncb * SUBLANES, LANES), lambda i, j: (i, j, 0))
    wide = lambda width: jax.ShapeDtypeStruct((b, l, width), F32)
    uv, wk, qd, kd, qk, gl = pl.pallas_call(
        _delta_prep_kernel,
        out_shape=(wide(DN_VW), wide(DN_QK), wide(DN_QK), wide(DN_QK), wide(DN_HEADS * CHUNK),
                   jax.ShapeDtypeStruct((b, nc * SUBLANES, LANES), F32)),
        grid=(b // sb, nc // ncb),
        in_specs=[tok(DN_QK), tok(DN_QK), tok(DN_VW), tok(LANES)],
        out_specs=(tok(DN_VW), tok(DN_QK), tok(DN_QK), tok(DN_QK), tok(DN_HEADS * CHUNK), glspec),
        compiler_params=pltpu.CompilerParams(dimension_semantics=("arbitrary", "arbitrary"), vmem_limit_bytes=VMEM_LIMIT),
        name="delta_prep",
    )(q, k, v, bg)

    bb = min(DELTA_SCAN_SEQS, b)
    ctok = lambda width: pl.BlockSpec((bb, CHUNK, width), lambda i, j: (i, j, 0))
    st = pl.BlockSpec((bb, DN_HEADS, DN_DK, DN_DV), lambda i, j: (i, 0, 0, 0))
    return pl.pallas_call(
        _delta_scan_kernel,
        out_shape=(wide(DN_VW), jax.ShapeDtypeStruct((b, DN_HEADS, DN_DK, DN_DV), F32)),
        grid=(b // bb, nc),
        in_specs=[ctok(DN_VW), ctok(DN_QK), ctok(DN_QK), ctok(DN_QK), ctok(DN_HEADS * CHUNK),
                  pl.BlockSpec((bb, SUBLANES, LANES), lambda i, j: (i, j, 0)), ctok(DN_VW), st,
                  pl.BlockSpec((1, DN_DV), lambda i, j: (0, 0))],
        out_specs=(ctok(DN_VW), st),
        scratch_shapes=[pltpu.VMEM((bb, DN_HEADS, DN_DK, DN_DV), F32)],
        compiler_params=pltpu.CompilerParams(dimension_semantics=("arbitrary", "arbitrary"), vmem_limit_bytes=VMEM_LIMIT),
        name="delta_scan",
    )(uv, wk, qd, kd, qk, gl, z, s0, g_out)


def _merge_kernel(x_ref, ya_ref, yb_ref, graw_ref, w_up_pool_ref, w_up_dn_ref, w_out_ref, g_ffn_ref, x2_ref, xn_ref):
    graw = graw_ref[...]
    ga = _sigmoid(graw[:, :D_MODEL])
    gb = _sigmoid(graw[:, D_MODEL:])
    merged = ga * _mm(ya_ref[...], w_up_pool_ref[...]) + gb * _mm(yb_ref[...], w_up_dn_ref[...])
    x2 = x_ref[...] + _mm(merged, w_out_ref[...])
    x2_ref[...] = x2
    xn_ref[...] = _rms(x2, g_ffn_ref[...])


def _merge(x, ya, yb, graw, w_up_pool, w_up_dn, w_out, g_ffn, tm):
    t = x.shape[0]
    tok = lambda width: pl.BlockSpec((tm, width), lambda i: (i, 0))
    full = lambda a: pl.BlockSpec(a.shape, lambda i: (0,) * a.ndim)
    weights = (w_up_pool, w_up_dn, w_out, g_ffn)
    return pl.pallas_call(
        _merge_kernel,
        out_shape=(jax.ShapeDtypeStruct((t, D_MODEL), F32), jax.ShapeDtypeStruct((t, D_MODEL), F32)),
        grid=(t // tm,),
        in_specs=[tok(D_MODEL), tok(D_POOL), tok(DN_VW), tok(2 * D_MODEL)] + [full(a) for a in weights],
        out_specs=(tok(D_MODEL), tok(D_MODEL)),
        compiler_params=pltpu.CompilerParams(dimension_semantics=("arbitrary",), vmem_limit_bytes=VMEM_LIMIT),
        name="branch_merge",
    )(x, ya, yb, graw, *weights)


def _top16_rows(s):
    n = s.shape[0]
    iota = lax.broadcasted_iota(jnp.int32, s.shape, 0)
    vals, idxs = [], []
    for _ in range(PEER_TOPK):
        m = jnp.max(s, axis=0, keepdims=True)
        idx = jnp.min(jnp.where(s == m, iota, n), axis=0, keepdims=True)
        vals.append(m)
        idxs.append(idx)
        s = jnp.where(iota == idx, -jnp.inf, s)
    return jnp.concatenate(vals, axis=0), jnp.concatenate(idxs, axis=0)


def _take_rows(table, idx):
    out = jnp.zeros_like(table)
    for a in range(PEER_TOPK):
        out = jnp.where(idx == a, table[a:a + 1, :], out)
    return out


def _route_kernel(xn_ref, wq_ref, keys_ref, experts_ref, gates_ref):
    half = PEER_DQ // 2
    q = jnp.dot(xn_ref[...].astype(BF16), wq_ref[...], preferred_element_type=F32).astype(BF16)
    experts, gates = [], []
    for hh in range(PEER_HEADS):
        q1 = q[:, hh * PEER_DQ:hh * PEER_DQ + half]
        q2 = q[:, hh * PEER_DQ + half:(hh + 1) * PEER_DQ]
        nt = (((1,), (1,)), ((), ()))
        s1 = lax.dot_general(keys_ref[0, hh], q1, nt, preferred_element_type=F32)
        s2 = lax.dot_general(keys_ref[1, hh], q2, nt, preferred_element_type=F32)
        v1, i1 = _top16_rows(s1)
        v2, i2 = _top16_rows(s2)
        cand = jnp.concatenate([v1[a:a + 1, :] + v2 for a in range(PEER_TOPK)], axis=0)
        cv, ci = _top16_rows(cand)
        e1 = _take_rows(i1, ci // PEER_TOPK)
        e2 = _take_rows(i2, ci % PEER_TOPK)
        experts.append(e1 * PEER_KEYS + e2)
        ex = jnp.exp(cv - cv[0:1, :])
        gates.append(ex / jnp.sum(ex, axis=0, keepdims=True))
    experts_ref[...] = jnp.concatenate(experts, axis=0).T
    gates_ref[...] = jnp.concatenate(gates, axis=0).T


def _route(xn, wq, keys, tm):
    t = xn.shape[0]
    return pl.pallas_call(
        _route_kernel,
        out_shape=(jax.ShapeDtypeStruct((t, PEER_PICKS), jnp.int32), jax.ShapeDtypeStruct((t, PEER_PICKS), F32)),
        grid=(t // tm,),
        in_specs=[pl.BlockSpec((tm, D_MODEL), lambda i: (i, 0)),
                  pl.BlockSpec(wq.shape, lambda i: (0, 0)),
                  pl.BlockSpec(keys.shape, lambda i: (0, 0, 0, 0))],
        out_specs=(pl.BlockSpec((tm, PEER_PICKS), lambda i: (i, 0)), pl.BlockSpec((tm, PEER_PICKS), lambda i: (i, 0))),
        compiler_params=pltpu.CompilerParams(dimension_semantics=("arbitrary",), vmem_limit_bytes=VMEM_LIMIT),
        name="peer_route",
    )(xn, wq, keys)


EXPERT_TOKENS = 16


def _experts_kernel(idx_ref, idx_next_ref, gates_ref, xn_ref, x2_ref, g_final_ref, tab_ref, out_ref, buf, sem):
    i = pl.program_id(0)
    n = pl.num_programs(0)
    tb = EXPERT_TOKENS
    lane = lax.broadcasted_iota(jnp.int32, (PEER_HEADS, PEER_TOPK), 1)
    g_final = g_final_ref[...]

    def issue_token(ref, t, sl):
        for h in range(PEER_HEADS):
            for k in range(PEER_TOPK):
                e = ref[0, t, h * PEER_TOPK + k]
                pltpu.async_copy(tab_ref.at[e], buf.at[sl, t, k, pl.ds(h, 1), :], sem.at[sl], priority=k % 2)

    def wait_slot(sl):
        for t in range(tb):
            pltpu.make_async_copy(buf.at[sl, t], buf.at[sl, t], sem.at[sl]).wait()

    def compute_token(t, sl):
        xb = jnp.broadcast_to(xn_ref[t:t + 1, :], (PEER_HEADS, D_MODEL))
        act = jnp.zeros((PEER_HEADS, PEER_TOPK), F32)
        for k in range(PEER_TOPK):
            s = jnp.sum(buf[sl, t, k, :, :D_MODEL] * xb, axis=-1, keepdims=True)
            act = jnp.where(lane == k, s, act)
        coef = gates_ref[t] * (0.5 * act * (1.0 + lax.erf(act * (0.5 ** 0.5))))
        acc = jnp.zeros((PEER_HEADS, D_MODEL), F32)
        for k in range(PEER_TOPK):
            acc = acc + coef[:, k:k + 1] * buf[sl, t, k, :, D_MODEL:]
        y = jnp.sum(acc, axis=0, keepdims=True)
        out_ref[t:t + 1, :] = _rms(x2_ref[t:t + 1, :] + y, g_final)

    @pl.when(i == 0)
    def _():
        for t in range(tb):
            issue_token(idx_ref, t, 0)

    def step(sl):
        wait_slot(sl)
        for t in range(tb):
            issue_token(idx_next_ref, t, 1 - sl)
            compute_token(t, sl)

        @pl.when(i == n - 1)
        def _():
            wait_slot(1 - sl)

    @pl.when(i % 2 == 0)
    def _():
        step(0)

    @pl.when(i % 2 == 1)
    def _():
        step(1)


def _experts(experts, gates, xn, x2, g_final, table):
    t = xn.shape[0]
    tb = EXPERT_TOKENS
    nb = t // tb
    idx = experts.reshape(nb, tb, PEER_PICKS)
    gates = gates.reshape(t, PEER_HEADS, PEER_TOPK)
    tok = lambda width: pl.BlockSpec((tb, width), lambda i: (i, 0))
    return pl.pallas_call(
        _experts_kernel,
        out_shape=jax.ShapeDtypeStruct((t, D_MODEL), F32),
        grid=(nb,),
        in_specs=[pl.BlockSpec((1, tb, PEER_PICKS), lambda i: (i, 0, 0), memory_space=pltpu.SMEM),
                  pl.BlockSpec((1, tb, PEER_PICKS), lambda i: (jnp.minimum(i + 1, nb - 1), 0, 0), memory_space=pltpu.SMEM),
                  pl.BlockSpec((tb, PEER_HEADS, PEER_TOPK), lambda i: (i, 0, 0)), tok(D_MODEL), tok(D_MODEL),
                  pl.BlockSpec((1, D_MODEL), lambda i: (0, 0)),
                  pl.BlockSpec(memory_space=pl.ANY)],
        out_specs=tok(D_MODEL),
        scratch_shapes=[pltpu.VMEM((2, tb, PEER_TOPK, PEER_HEADS, 2 * D_MODEL), F32), pltpu.SemaphoreType.DMA((2,))],
        compiler_params=pltpu.CompilerParams(dimension_semantics=("arbitrary",), vmem_limit_bytes=VMEM_LIMIT),
        name="peer_experts",
    )(idx, idx, gates, xn, x2, g_final, table)


def _pad_rows_front(a, rows):
    return jnp.pad(a, ((0, 0), (rows - a.shape[1], 0), (0, 0)))


MERGE_ROWS = 512
ROUTE_ROWS = 256
FRONT_ROWS = 512


def _group(x, pool_prev, conv_prev, s_prev, pos0, p):
    b, l, _ = x.shape
    tl = min(FRONT_ROWS, l)
    sb = min(b, FRONT_ROWS // tl)
    tm_merge = min(MERGE_ROWS, b * l)
    tm_route = min(ROUTE_ROWS, b * l)
    ya, q, k, v, z, graw, bg, pool_tail, conv_tail = _front(
        x, _pad_rows_front(pool_prev, POOL_PREFIX_ROWS), _pad_rows_front(conv_prev, CONV_PREFIX_ROWS), pos0, sb, tl,
        p["g_mix"], p["w_u"], p["w_qkv"], p["w_z"], p["w_ba"], p["w_g"], p["w_grp"], p["pool_scale"], p["w_conv8"],
        p["a_log_pad"], p["dt_pad"])
    yb, s_new = _delta(q, k, v, z, bg, s_prev, p["g_dn_out"])
    t = b * l
    flat = lambda a: a.reshape(t, a.shape[-1])
    x2, xn = _merge(flat(x), flat(ya), flat(yb), flat(graw), p["w_up_pool"], p["w_up_dn"], p["w_out"], p["g_ffn"], tm_merge)
    experts, gates = _route(xn, p["w_peer_q"], p["peer_keys"], tm_route)
    y = _experts(experts, gates, xn, x2, p["g_final"], p["peer_table"])
    return (y.reshape(b, l, D_MODEL), pool_tail[:, POOL_PREFIX_ROWS - POOL_STATE:],
            conv_tail[:, CONV_PREFIX_ROWS - (CONV_W - 1):], s_new)


def kernel(x_prompt, x_sample, cache_pool, state_dn_conv, state_dn, g_mix, w_in, w_pool_grp, pool_scale, w_conv, a_log,
           dt_bias, g_dn_out, w_up_pool, w_up_dn, w_out, g_ffn, w_peer_q, peer_sub_keys, peer_u, peer_v, g_final):
    depth = w_in.shape[0]
    assert depth == 1
    bp = x_prompt.shape[0]
    lane_pad = lambda a, off: jnp.pad(a.astype(F32)[None, :], ((0, 0), (off, LANES - off - a.shape[0])))
    w = w_in[0]
    params = {
        "g_mix": g_mix[0][None, :],
        "w_u": w[:, OFF_U:OFF_QKV].astype(BF16),
        "w_qkv": w[:, OFF_QKV:OFF_Z].astype(BF16),
        "w_z": w[:, OFF_Z:OFF_B].astype(BF16),
        "w_ba": jnp.pad(w[:, OFF_B:OFF_G], ((0, 0), (0, LANES - 2 * DN_HEADS))).astype(BF16),
        "w_g": w[:, OFF_G:].astype(BF16),
        "w_grp": w_pool_grp[0].astype(BF16),
        "pool_scale": pool_scale[0][None, :],
        "w_conv8": jnp.pad(w_conv[0], ((0, SUBLANES - CONV_W), (0, 0))),
        "a_log_pad": lane_pad(a_log[0], DN_HEADS),
        "dt_pad": lane_pad(dt_bias[0], DN_HEADS),
        "g_dn_out": g_dn_out[0][None, :],
        "w_up_pool": w_up_pool[0].astype(BF16),
        "w_up_dn": w_up_dn[0].astype(BF16),
        "w_out": w_out[0].astype(BF16),
        "g_ffn": g_ffn[0][None, :],
        "w_peer_q": w_peer_q[0].reshape(D_MODEL, PEER_HEADS * PEER_DQ).astype(BF16),
        "peer_keys": peer_sub_keys[0].astype(BF16),
        "peer_table": jnp.concatenate([peer_u[0][:, None, :], peer_v[0][:, None, :]], axis=2),
        "g_final": g_final[None, :],
    }
    zeros = lambda *shape: jnp.zeros(shape, F32)
    yp, pool_p, conv_p, dn_p = _group(
        x_prompt, zeros(bp, POOL_STATE, D_POOL), zeros(bp, CONV_W - 1, CONV_CH), zeros(bp, DN_HEADS, DN_DK, DN_DV),
        0, params)
    ys, pool_s, conv_s, dn_s = _group(
        x_sample, cache_pool[0], state_dn_conv[0], state_dn[0].astype(F32),
        PAST_LEN, params)
    return (yp, ys, pool_p[None], conv_p[None], dn_p[None].astype(state_dn.dtype),
            pool_s[None], conv_s[None], dn_s[None].astype(state_dn.dtype))
```

```python
import functools

import jax
import jax.numpy as jnp
from jax import lax
from jax.experimental import pallas as pl
from jax.experimental.pallas import tpu as pltpu

D_MODEL = 1024
CHUNK = 64
D_POOL = 512
POOL_WINDOWS = (2, 4, 8, 16)
POOL_GROUP = 128
POOL_STATE = 15
DN_HEADS = 4
DN_DK = 128
DN_DV = 128
DN_QK = DN_HEADS * DN_DK
DN_VW = DN_HEADS * DN_DV
CONV_W = 4
CONV_CH = 2 * DN_QK + DN_VW
OFF_U = 0
OFF_QKV = OFF_U + D_POOL
OFF_Z = OFF_QKV + CONV_CH
OFF_B = OFF_Z + DN_VW
OFF_A = OFF_B + DN_HEADS
OFF_G = OFF_A + DN_HEADS
PEER_HEADS = 8
PEER_KEYS = 128
PEER_DQ = 256
PEER_TOPK = 16
PEER_PICKS = PEER_HEADS * PEER_TOPK
EPS = 1e-6
PAST_LEN = 4096

LANES = 128
SUBLANES = 8
POOL_PREFIX_ROWS = 16
CONV_PREFIX_ROWS = 8
VMEM_LIMIT = 56 * 1024 * 1024

F32 = jnp.float32
BF16 = jnp.bfloat16
HIGHEST = lax.Precision.HIGHEST


def _mm(a, b):
    return jnp.dot(a.astype(BF16), b.astype(BF16), preferred_element_type=F32)


def _mm_f32(a, b):
    return jnp.dot(a, b, precision=HIGHEST, preferred_element_type=F32)


def _mm_nt_f32(a, b):
    return lax.dot_general(a, b, (((1,), (1,)), ((), ())), precision=HIGHEST, preferred_element_type=F32)


def _mm_tn_f32(a, b):
    return lax.dot_general(a, b, (((0,), (0,)), ((), ())), precision=HIGHEST, preferred_element_type=F32)


def _sigmoid(x):
    return 1.0 / (1.0 + jnp.exp(-x))


def _silu(x):
    return x * _sigmoid(x)


def _softplus(x):
    return jnp.maximum(x, 0.0) + jnp.log1p(jnp.exp(-jnp.abs(x)))


def _rms(x, g):
    return x * lax.rsqrt(jnp.mean(x * x, axis=-1, keepdims=True) + EPS) * g


def _front_kernel(pos0, x_ref, pool_pre_ref, conv_pre_ref, g_mix_ref, w_u_ref, w_qkv_ref, w_z_ref, w_ba_ref, w_g_ref,
                  w_grp_ref, pool_scale_ref, w_conv_ref, a_log_ref, dt_ref,
                  ya_ref, q_ref, k_ref, v_ref, z_ref, graw_ref, bg_ref, pool_tail_ref, conv_tail_ref,
                  carry_u, carry_c):
    sb, tl, _ = x_ref.shape
    li = pl.program_id(1)

    @pl.when(li == 0)
    def _():
        carry_u[...] = pool_pre_ref[...]
        carry_c[...] = conv_pre_ref[...]

    x = x_ref[...].reshape(sb * tl, D_MODEL)
    h = _rms(x, g_mix_ref[...]).astype(BF16)
    u = jnp.dot(h, w_u_ref[...], preferred_element_type=F32)
    qkv = jnp.dot(h, w_qkv_ref[...], preferred_element_type=F32)
    z_ref[...] = jnp.dot(h, w_z_ref[...], preferred_element_type=F32).reshape(sb, tl, DN_VW)
    graw_ref[...] = jnp.dot(h, w_g_ref[...], preferred_element_type=F32).reshape(sb, tl, 2 * D_MODEL)
    ba = jnp.dot(h, w_ba_ref[...], preferred_element_type=F32)
    lane = lax.broadcasted_iota(jnp.int32, ba.shape, 1)
    beta = _sigmoid(ba)
    g = -jnp.exp(a_log_ref[...]) * _softplus(ba + dt_ref[...])
    bg_ref[...] = jnp.where(lane < DN_HEADS, beta, g).reshape(sb, tl, LANES)

    row = lax.broadcasted_iota(jnp.int32, (tl, POOL_GROUP), 0)
    pos1 = pos0 + li * tl + row + 1
    w_conv = w_conv_ref[...]

    for s in range(sb):
        u_s = u[s * tl:(s + 1) * tl]
        ext = jnp.concatenate([carry_u[s], u_s], axis=0)
        mixed = []
        for gi, w in enumerate(POOL_WINDOWS):
            acc = ext[:, gi * POOL_GROUP:(gi + 1) * POOL_GROUP]
            span = 1
            while span < w:
                acc = acc + pltpu.roll(acc, span, axis=0)
                span *= 2
            win = acc[POOL_PREFIX_ROWS:]
            cnt = jnp.minimum(pos1, w).astype(F32)
            pooled = win / cnt - u_s[:, gi * POOL_GROUP:(gi + 1) * POOL_GROUP]
            mixed.append(_mm(pooled, w_grp_ref[gi]))
        ya_ref[s] = jnp.concatenate(mixed, axis=1) * pool_scale_ref[...]
        pool_tail_ref[s] = ext[tl:]
        carry_u[s] = ext[tl:]

        c_s = qkv[s * tl:(s + 1) * tl]
        cext = jnp.concatenate([carry_c[s], c_s], axis=0)
        y = c_s * w_conv[CONV_W - 1:CONV_W]
        for j in range(1, CONV_W):
            y = y + pltpu.roll(cext, j, axis=0)[CONV_PREFIX_ROWS:] * w_conv[CONV_W - 1 - j:CONV_W - j]
        y = _silu(y)
        conv_tail_ref[s] = cext[tl:]
        carry_c[s] = cext[tl:]
        for hh in range(DN_HEADS):
            sl = slice(hh * DN_DK, (hh + 1) * DN_DK)
            qh = y[:, sl]
            q_ref[s, :, sl] = qh * lax.rsqrt(jnp.sum(qh * qh, axis=-1, keepdims=True) + EPS)
            kh = y[:, DN_QK + hh * DN_DK:DN_QK + (hh + 1) * DN_DK]
            k_ref[s, :, sl] = kh * lax.rsqrt(jnp.sum(kh * kh, axis=-1, keepdims=True) + EPS)
        v_ref[s] = y[:, 2 * DN_QK:]


def _front(x, pool_pre, conv_pre, pos0, sb, tl, g_mix, w_u, w_qkv, w_z, w_ba, w_g, w_grp, pool_scale, w_conv8,
           a_log_pad, dt_pad):
    b, l, _ = x.shape
    grid = (b // sb, l // tl)
    tok = lambda width: pl.BlockSpec((sb, tl, width), lambda i, j: (i, j, 0))
    seq = lambda rows, width: pl.BlockSpec((sb, rows, width), lambda i, j: (i, 0, 0))
    full = lambda a: pl.BlockSpec(a.shape, lambda i, j: (0,) * a.ndim)
    weights = (g_mix, w_u, w_qkv, w_z, w_ba, w_g, w_grp, pool_scale, w_conv8, a_log_pad, dt_pad)
    out_shape = (
        jax.ShapeDtypeStruct((b, l, D_POOL), F32),
        jax.ShapeDtypeStruct((b, l, DN_QK), F32),
        jax.ShapeDtypeStruct((b, l, DN_QK), F32),
        jax.ShapeDtypeStruct((b, l, DN_VW), F32),
        jax.ShapeDtypeStruct((b, l, DN_VW), F32),
        jax.ShapeDtypeStruct((b, l, 2 * D_MODEL), F32),
        jax.ShapeDtypeStruct((b, l, LANES), F32),
        jax.ShapeDtypeStruct((b, POOL_PREFIX_ROWS, D_POOL), F32),
        jax.ShapeDtypeStruct((b, CONV_PREFIX_ROWS, CONV_CH), F32),
    )
    return pl.pallas_call(
        functools.partial(_front_kernel, pos0),
        out_shape=out_shape,
        grid=grid,
        in_specs=[tok(D_MODEL), seq(POOL_PREFIX_ROWS, D_POOL), seq(CONV_PREFIX_ROWS, CONV_CH)] + [full(a) for a in weights],
        out_specs=(tok(D_POOL), tok(DN_QK), tok(DN_QK), tok(DN_VW), tok(DN_VW), tok(2 * D_MODEL), tok(LANES),
                   seq(POOL_PREFIX_ROWS, D_POOL), seq(CONV_PREFIX_ROWS, CONV_CH)),
        scratch_shapes=[pltpu.VMEM((sb, POOL_PREFIX_ROWS, D_POOL), F32), pltpu.VMEM((sb, CONV_PREFIX_ROWS, CONV_CH), F32)],
        compiler_params=pltpu.CompilerParams(dimension_semantics=("arbitrary", "arbitrary"), vmem_limit_bytes=VMEM_LIMIT),
        name="mixer_front",
    )(x, pool_pre, conv_pre, *weights)


_NN = (((1,), (0,)), ((), ()))
_NT = (((1,), (1,)), ((), ()))
_TN = (((0,), (0,)), ((), ()))


def _dot_bf16(a, b, dims):
    return lax.dot_general(a.astype(BF16), b.astype(BF16), dims, preferred_element_type=F32)


def _mm_nt(a, b):
    return _dot_bf16(a, b, _NT)


def _mm_chain(a, b):
    return _dot_bf16(a, b, _NN)


def _mm_state(a, b):
    return _dot_bf16(a, b, _NN)


def _mm_tn_state(a, b):
    return _dot_bf16(a, b, _TN)


def _delta_prep_kernel(q_ref, k_ref, v_ref, bg_ref, uv_ref, wk_ref, qd_ref, kd_ref, qk_ref, gl_ref):
    sb = q_ref.shape[0]
    ncb = q_ref.shape[1] // CHUNK
    row = lax.broadcasted_iota(jnp.int32, (CHUNK, CHUNK), 0)
    col = lax.broadcasted_iota(jnp.int32, (CHUNK, CHUNK), 1)
    incl = row >= col
    strict = row > col
    tri = incl.astype(F32)
    xs, pw = [], []
    for s in range(sb):
        for c in range(ncb):
            rows = slice(c * CHUNK, (c + 1) * CHUNK)
            bg = bg_ref[s, rows, :]
            gc_all = _mm_f32(tri, bg)
            gc_rows = gc_all.T
            gl_ref[s, c * SUBLANES:(c + 1) * SUBLANES, :] = gc_all[CHUNK - SUBLANES:, :]
            for hh in range(DN_HEADS):
                sl = slice(hh * DN_DK, (hh + 1) * DN_DK)
                q = q_ref[s, rows, sl] * (DN_DK ** -0.5)
                k = k_ref[s, rows, sl]
                beta = bg[:, hh:hh + 1]
                gcol = gc_all[:, DN_HEADS + hh:DN_HEADS + hh + 1]
                grow = gc_rows[DN_HEADS + hh:DN_HEADS + hh + 1, :]
                decay = jnp.exp(jnp.where(incl, gcol - grow, -jnp.inf))
                egc = jnp.exp(gcol)
                kb = k * beta
                pw.append(-jnp.where(strict, _mm_nt(kb, k) * decay, 0.0))
                xs.append(jnp.concatenate([v_ref[s, rows, sl] * beta, kb * egc], axis=1))
                qk_ref[s, rows, hh * CHUNK:(hh + 1) * CHUNK] = _mm_nt(q, k) * decay
                gl = gcol[CHUNK - 1:CHUNK, :]
                qd_ref[s, rows, sl] = q * egc
                kd_ref[s, rows, sl] = k * jnp.exp(gl - gcol)
    span = 1
    while True:
        xs = [x + _mm_chain(p, x) for p, x in zip(pw, xs)]
        span *= 2
        if span >= CHUNK:
            break
        pw = [_mm_chain(p, p) for p in pw]
    i = 0
    for s in range(sb):
        for c in range(ncb):
            rows = slice(c * CHUNK, (c + 1) * CHUNK)
            for hh in range(DN_HEADS):
                sl = slice(hh * DN_DK, (hh + 1) * DN_DK)
                uv_ref[s, rows, sl] = xs[i][:, :DN_DV]
                wk_ref[s, rows, sl] = xs[i][:, DN_DV:]
                i += 1


def _delta_scan_kernel(uv_ref, wk_ref, qd_ref, kd_ref, qk_ref, gl_ref, z_ref, s0_ref, g_out_ref, yb_ref, s_out_ref, s_scr):
    ci = pl.program_id(1)
    bb = uv_ref.shape[0]

    @pl.when(ci == 0)
    def _():
        s_scr[...] = s0_ref[...]

    g_out = g_out_ref[...]
    probs = [(b, hh) for b in range(bb) for hh in range(DN_HEADS)]
    lanes = lambda hh: slice(hh * DN_DK, (hh + 1) * DN_DK)
    s_old = [s_scr[b, hh] for b, hh in probs]
    v_new = [uv_ref[b, :, lanes(hh)] - _mm_state(wk_ref[b, :, lanes(hh)], s) for (b, hh), s in zip(probs, s_old)]
    o_state = [_mm_state(qd_ref[b, :, lanes(hh)], s) for (b, hh), s in zip(probs, s_old)]
    for (b, hh), s, vn, os_ in zip(probs, s_old, v_new, o_state):
        o = os_ + _mm_state(qk_ref[b, :, hh * CHUNK:(hh + 1) * CHUNK], vn)
        egl = jnp.exp(gl_ref[b, SUBLANES - 1:SUBLANES, DN_HEADS + hh:DN_HEADS + hh + 1])
        s_new = s * egl + _mm_tn_state(kd_ref[b, :, lanes(hh)], vn)
        s_scr[b, hh] = s_new
        s_out_ref[b, hh] = s_new
        yb_ref[b, :, lanes(hh)] = _rms(o, g_out) * _silu(z_ref[b, :, lanes(hh)])


DELTA_PREP_CHUNKS = 4
DELTA_SCAN_SEQS = 8


def _delta(q, k, v, z, bg, s0, g_out):
    b, l, _ = q.shape
    nc = l // CHUNK
    ncb = min(DELTA_PREP_CHUNKS, nc)
    sb = min(b, DELTA_PREP_CHUNKS // ncb)
    rows = ncb * CHUNK
    tok = lambda width: pl.BlockSpec((sb, rows, width), lambda i, j: (i, j, 0))
    glspec = pl.BlockSpec((sb, ncb * SUBLANES, LANES), lambda i, j: (i, j, 0))
    wide = lambda width: jax.ShapeDtypeStruct((b, l, width), F32)
    uv, wk, qd, kd, qk, gl = pl.pallas_call(
        _delta_prep_kernel,
        out_shape=(wide(DN_VW), wide(DN_QK), wide(DN_QK), wide(DN_QK), wide(DN_HEADS * CHUNK),
                   jax.ShapeDtypeStruct((b, nc * SUBLANES, LANES), F32)),
        grid=(b // sb, nc // ncb),
        in_specs=[tok(DN_QK), tok(DN_QK), tok(DN_VW), tok(LANES)],
        out_specs=(tok(DN_VW), tok(DN_QK), tok(DN_QK), tok(DN_QK), tok(DN_HEADS * CHUNK), glspec),
        compiler_params=pltpu.CompilerParams(dimension_semantics=("arbitrary", "arbitrary"), vmem_limit_bytes=VMEM_LIMIT),
        name="delta_prep",
    )(q, k, v, bg)

    bb = min(DELTA_SCAN_SEQS, b)
    ctok = lambda width: pl.BlockSpec((bb, CHUNK, width), lambda i, j: (i, j, 0))
    st = pl.BlockSpec((bb, DN_HEADS, DN_DK, DN_DV), lambda i, j: (i, 0, 0, 0))
    return pl.pallas_call(
        _delta_scan_kernel,
        out_shape=(wide(DN_VW), jax.ShapeDtypeStruct((b, DN_HEADS, DN_DK, DN_DV), F32)),
        grid=(b // bb, nc),
        in_specs=[ctok(DN_VW), ctok(DN_QK), ctok(DN_QK), ctok(DN_QK), ctok(DN_HEADS * CHUNK),
                  pl.BlockSpec((bb, SUBLANES, LANES), lambda i, j: (i, j, 0)), ctok(DN_VW), st,
                  pl.BlockSpec((1, DN_DV), lambda i, j: (0, 0))],
        out_specs=(ctok(DN_VW), st),
        scratch_shapes=[pltpu.VMEM((bb, DN_HEADS, DN_DK, DN_DV), F32)],
        compiler_params=pltpu.CompilerParams(dimension_semantics=("arbitrary", "arbitrary"), vmem_limit_bytes=VMEM_LIMIT),
        name="delta_scan",
    )(uv, wk, qd, kd, qk, gl, z, s0, g_out)


def _merge_kernel(x_ref, ya_ref, yb_ref, graw_ref, w_up_pool_ref, w_up_dn_ref, w_out_ref, g_ffn_ref, x2_ref, xn_ref):
    graw = graw_ref[...]
    ga = _sigmoid(graw[:, :D_MODEL])
    gb = _sigmoid(graw[:, D_MODEL:])
    merged = ga * _mm(ya_ref[...], w_up_pool_ref[...]) + gb * _mm(yb_ref[...], w_up_dn_ref[...])
    x2 = x_ref[...] + _mm(merged, w_out_ref[...])
    x2_ref[...] = x2
    xn_ref[...] = _rms(x2, g_ffn_ref[...])


def _merge(x, ya, yb, graw, w_up_pool, w_up_dn, w_out, g_ffn, tm):
    t = x.shape[0]
    tok = lambda width: pl.BlockSpec((tm, width), lambda i: (i, 0))
    full = lambda a: pl.BlockSpec(a.shape, lambda i: (0,) * a.ndim)
    weights = (w_up_pool, w_up_dn, w_out, g_ffn)
    return pl.pallas_call(
        _merge_kernel,
        out_shape=(jax.ShapeDtypeStruct((t, D_MODEL), F32), jax.ShapeDtypeStruct((t, D_MODEL), F32)),
        grid=(t // tm,),
        in_specs=[tok(D_MODEL), tok(D_POOL), tok(DN_VW), tok(2 * D_MODEL)] + [full(a) for a in weights],
        out_specs=(tok(D_MODEL), tok(D_MODEL)),
        compiler_params=pltpu.CompilerParams(dimension_semantics=("arbitrary",), vmem_limit_bytes=VMEM_LIMIT),
        name="branch_merge",
    )(x, ya, yb, graw, *weights)


def _top16_rows(s):
    n = s.shape[0]
    iota = lax.broadcasted_iota(jnp.int32, s.shape, 0)
    vals, idxs = [], []
    for _ in range(PEER_TOPK):
        m = jnp.max(s, axis=0, keepdims=True)
        idx = jnp.min(jnp.where(s == m, iota, n), axis=0, keepdims=True)
        vals.append(m)
        idxs.append(idx)
        s = jnp.where(iota == idx, -jnp.inf, s)
    return jnp.concatenate(vals, axis=0), jnp.concatenate(idxs, axis=0)


def _take_rows(table, idx):
    out = jnp.zeros_like(table)
    for a in range(PEER_TOPK):
        out = jnp.where(idx == a, table[a:a + 1, :], out)
    return out


def _route_kernel(xn_ref, wq_ref, keys_ref, experts_ref, gates_ref):
    half = PEER_DQ // 2
    q = jnp.dot(xn_ref[...].astype(BF16), wq_ref[...], preferred_element_type=F32).astype(BF16)
    experts, gates = [], []
    for hh in range(PEER_HEADS):
        q1 = q[:, hh * PEER_DQ:hh * PEER_DQ + half]
        q2 = q[:, hh * PEER_DQ + half:(hh + 1) * PEER_DQ]
        nt = (((1,), (1,)), ((), ()))
        s1 = lax.dot_general(keys_ref[0, hh], q1, nt, preferred_element_type=F32)
        s2 = lax.dot_general(keys_ref[1, hh], q2, nt, preferred_element_type=F32)
        v1, i1 = _top16_rows(s1)
        v2, i2 = _top16_rows(s2)
        cand = jnp.concatenate([v1[a:a + 1, :] + v2 for a in range(PEER_TOPK)], axis=0)
        cv, ci = _top16_rows(cand)
        e1 = _take_rows(i1, ci // PEER_TOPK)
        e2 = _take_rows(i2, ci % PEER_TOPK)
        experts.append(e1 * PEER_KEYS + e2)
        ex = jnp.exp(cv - cv[0:1, :])
        gates.append(ex / jnp.sum(ex, axis=0, keepdims=True))
    experts_ref[...] = jnp.concatenate(experts, axis=0).T
    gates_ref[...] = jnp.concatenate(gates, axis=0).T


def _route(xn, wq, keys, tm):
    t = xn.shape[0]
    return pl.pallas_call(
        _route_kernel,
        out_shape=(jax.ShapeDtypeStruct((t, PEER_PICKS), jnp.int32), jax.ShapeDtypeStruct((t, PEER_PICKS), F32)),
        grid=(t // tm,),
        in_specs=[pl.BlockSpec((tm, D_MODEL), lambda i: (i, 0)),
                  pl.BlockSpec(wq.shape, lambda i: (0, 0)),
                  pl.BlockSpec(keys.shape, lambda i: (0, 0, 0, 0))],
        out_specs=(pl.BlockSpec((tm, PEER_PICKS), lambda i: (i, 0)), pl.BlockSpec((tm, PEER_PICKS), lambda i: (i, 0))),
        compiler_params=pltpu.CompilerParams(dimension_semantics=("arbitrary",), vmem_limit_bytes=VMEM_LIMIT),
        name="peer_route",
    )(xn, wq, keys)


EXPERT_TOKENS = 16


def _experts_kernel(idx_ref, idx_next_ref, gates_ref, xn_ref, x2_ref, g_final_ref, tab_ref, out_ref, buf_a, buf_b, cb, sem):
    bufs = (buf_a, buf_b)
    i = pl.program_id(0)
    n = pl.num_programs(0)
    tb = EXPERT_TOKENS
    lane = lax.broadcasted_iota(jnp.int32, (PEER_HEADS, PEER_TOPK), 1)
    g_final = g_final_ref[...]

    def issue_token(ref, t, sl):
        for h in range(PEER_HEADS):
            for k in range(PEER_TOPK):
                e = ref[0, t, h * PEER_TOPK + k]
                pltpu.async_copy(tab_ref.at[e], bufs[sl].at[t, k, h], sem.at[sl], priority=k % 2)

    def wait_slot(sl):
        for t in range(tb):
            pltpu.make_async_copy(bufs[sl].at[t], bufs[sl].at[t], sem.at[sl]).wait()

    def compute_token(t, sl):
        x = xn_ref[t]
        sub = lax.broadcasted_iota(jnp.int32, (PEER_HEADS, PEER_TOPK), 0)
        act = jnp.zeros((PEER_HEADS, PEER_TOPK), F32)
        for h in range(PEER_HEADS):
            part = jnp.zeros((SUBLANES, PEER_TOPK), F32)
            for k in range(PEER_TOPK):
                part = jnp.where(lane == k, jnp.sum(bufs[sl][t, k, h, :SUBLANES, :] * x, axis=-1, keepdims=True), part)
            act = jnp.where(sub == h, jnp.sum(part, axis=0, keepdims=True), act)
        coef = gates_ref[t] * (0.5 * act * (1.0 + lax.erf(act * (0.5 ** 0.5))))
        for k in range(PEER_TOPK):
            cb[t, k] = jnp.broadcast_to(coef[:, k:k + 1], (PEER_HEADS, LANES))
        accs = [jnp.zeros((SUBLANES, LANES), F32) for _ in range(4)]
        for k in range(PEER_TOPK):
            for h in range(PEER_HEADS):
                c = jnp.broadcast_to(cb[t, k, h:h + 1, :], (SUBLANES, LANES))
                accs[h % 4] = accs[h % 4] + c * bufs[sl][t, k, h, SUBLANES:, :]
        z = x2_ref[t] + ((accs[0] + accs[1]) + (accs[2] + accs[3]))
        ms = jnp.sum(jnp.sum(z * z, axis=-1, keepdims=True), axis=0, keepdims=True) * (1.0 / D_MODEL)
        out_ref[t] = z * lax.rsqrt(ms + EPS) * g_final

    @pl.when(i == 0)
    def _():
        for t in range(tb):
            issue_token(idx_ref, t, 0)

    def step(sl):
        wait_slot(sl)
        for t in range(tb):
            issue_token(idx_next_ref, t, 1 - sl)
            compute_token(t, sl)

        @pl.when(i == n - 1)
        def _():
            wait_slot(1 - sl)

    @pl.when(i % 2 == 0)
    def _():
        step(0)

    @pl.when(i % 2 == 1)
    def _():
        step(1)


def _experts(experts, gates, xn, x2, g_final, table):
    t = xn.shape[0]
    tb = EXPERT_TOKENS
    nb = t // tb
    idx = experts.reshape(nb, tb, PEER_PICKS)
    gates = gates.reshape(t, PEER_HEADS, PEER_TOPK)
    tiles = lambda a: a.reshape(a.shape[0], SUBLANES, LANES)
    tok = pl.BlockSpec((tb, SUBLANES, LANES), lambda i: (i, 0, 0))
    out = pl.pallas_call(
        _experts_kernel,
        out_shape=jax.ShapeDtypeStruct((t, SUBLANES, LANES), F32),
        grid=(nb,),
        in_specs=[pl.BlockSpec((1, tb, PEER_PICKS), lambda i: (i, 0, 0), memory_space=pltpu.SMEM),
                  pl.BlockSpec((1, tb, PEER_PICKS), lambda i: (jnp.minimum(i + 1, nb - 1), 0, 0), memory_space=pltpu.SMEM),
                  pl.BlockSpec((tb, PEER_HEADS, PEER_TOPK), lambda i: (i, 0, 0)), tok, tok,
                  pl.BlockSpec((SUBLANES, LANES), lambda i: (0, 0)),
                  pl.BlockSpec(memory_space=pl.ANY)],
        out_specs=tok,
        scratch_shapes=[pltpu.VMEM((tb, PEER_TOPK, PEER_HEADS, 2 * SUBLANES, LANES), F32),
                        pltpu.VMEM((tb, PEER_TOPK, PEER_HEADS, 2 * SUBLANES, LANES), F32),
                        pltpu.VMEM((tb, PEER_TOPK, PEER_HEADS, LANES), F32),
                        pltpu.SemaphoreType.DMA((2,))],
        compiler_params=pltpu.CompilerParams(dimension_semantics=("arbitrary",), vmem_limit_bytes=VMEM_LIMIT),
        name="peer_experts",
    )(idx, idx, gates, tiles(xn), tiles(x2), g_final.reshape(SUBLANES, LANES), table)
    return out.reshape(t, D_MODEL)


def _pad_rows_front(a, rows):
    return jnp.pad(a, ((0, 0), (rows - a.shape[1], 0), (0, 0)))


MERGE_ROWS = 512
ROUTE_ROWS = 256
FRONT_ROWS = 512


def _group(x, pool_prev, conv_prev, s_prev, pos0, p):
    b, l, _ = x.shape
    tl = min(FRONT_ROWS, l)
    sb = min(b, FRONT_ROWS // tl)
    tm_merge = min(MERGE_ROWS, b * l)
    tm_route = min(ROUTE_ROWS, b * l)
    ya, q, k, v, z, graw, bg, pool_tail, conv_tail = _front(
        x, _pad_rows_front(pool_prev, POOL_PREFIX_ROWS), _pad_rows_front(conv_prev, CONV_PREFIX_ROWS), pos0, sb, tl,
        p["g_mix"], p["w_u"], p["w_qkv"], p["w_z"], p["w_ba"], p["w_g"], p["w_grp"], p["pool_scale"], p["w_conv8"],
        p["a_log_pad"], p["dt_pad"])
    yb, s_new = _delta(q, k, v, z, bg, s_prev, p["g_dn_out"])
    t = b * l
    flat = lambda a: a.reshape(t, a.shape[-1])
    x2, xn = _merge(flat(x), flat(ya), flat(yb), flat(graw), p["w_up_pool"], p["w_up_dn"], p["w_out"], p["g_ffn"], tm_merge)
    experts, gates = _route(xn, p["w_peer_q"], p["peer_keys"], tm_route)
    y = _experts(experts, gates, xn, x2, p["g_final"], p["peer_table"])
    return (y.reshape(b, l, D_MODEL), pool_tail[:, POOL_PREFIX_ROWS - POOL_STATE:],
            conv_tail[:, CONV_PREFIX_ROWS - (CONV_W - 1):], s_new)


def kernel(x_prompt, x_sample, cache_pool, state_dn_conv, state_dn, g_mix, w_in, w_pool_grp, pool_scale, w_conv, a_log,
           dt_bias, g_dn_out, w_up_pool, w_up_dn, w_out, g_ffn, w_peer_q, peer_sub_keys, peer_u, peer_v, g_final):
    depth = w_in.shape[0]
    assert depth == 1
    bp = x_prompt.shape[0]
    lane_pad = lambda a, off: jnp.pad(a.astype(F32)[None, :], ((0, 0), (off, LANES - off - a.shape[0])))
    w = w_in[0]
    params = {
        "g_mix": g_mix[0][None, :],
        "w_u": w[:, OFF_U:OFF_QKV].astype(BF16),
        "w_qkv": w[:, OFF_QKV:OFF_Z].astype(BF16),
        "w_z": w[:, OFF_Z:OFF_B].astype(BF16),
        "w_ba": jnp.pad(w[:, OFF_B:OFF_G], ((0, 0), (0, LANES - 2 * DN_HEADS))).astype(BF16),
        "w_g": w[:, OFF_G:].astype(BF16),
        "w_grp": w_pool_grp[0].astype(BF16),
        "pool_scale": pool_scale[0][None, :],
        "w_conv8": jnp.pad(w_conv[0], ((0, SUBLANES - CONV_W), (0, 0))),
        "a_log_pad": lane_pad(a_log[0], DN_HEADS),
        "dt_pad": lane_pad(dt_bias[0], DN_HEADS),
        "g_dn_out": g_dn_out[0][None, :],
        "w_up_pool": w_up_pool[0].astype(BF16),
        "w_up_dn": w_up_dn[0].astype(BF16),
        "w_out": w_out[0].astype(BF16),
        "g_ffn": g_ffn[0][None, :],
        "w_peer_q": w_peer_q[0].reshape(D_MODEL, PEER_HEADS * PEER_DQ).astype(BF16),
        "peer_keys": peer_sub_keys[0].astype(BF16),
        "peer_table": jnp.concatenate([peer_u[0].reshape(-1, SUBLANES, LANES), peer_v[0].reshape(-1, SUBLANES, LANES)], axis=1),
        "g_final": g_final[None, :],
    }
    zeros = lambda *shape: jnp.zeros(shape, F32)
    yp, pool_p, conv_p, dn_p = _group(
        x_prompt, zeros(bp, POOL_STATE, D_POOL), zeros(bp, CONV_W - 1, CONV_CH), zeros(bp, DN_HEADS, DN_DK, DN_DV),
        0, params)
    ys, pool_s, conv_s, dn_s = _group(
        x_sample, cache_pool[0], state_dn_conv[0], state_dn[0].astype(F32),
        PAST_LEN, params)
    return (yp, ys, pool_p[None], conv_p[None], dn_p[None].astype(state_dn.dtype),
            pool_s[None], conv_s[None], dn_s[None].astype(state_dn.dtype))
```

```python
import functools

import jax
import jax.numpy as jnp
from jax import lax
from jax.experimental import pallas as pl
from jax.experimental.pallas import tpu as pltpu

D_MODEL = 1024
CHUNK = 64
D_POOL = 512
POOL_WINDOWS = (2, 4, 8, 16)
POOL_GROUP = 128
POOL_STATE = 15
DN_HEADS = 4
DN_DK = 128
DN_DV = 128
DN_QK = DN_HEADS * DN_DK
DN_VW = DN_HEADS * DN_DV
CONV_W = 4
CONV_CH = 2 * DN_QK + DN_VW
OFF_U = 0
OFF_QKV = OFF_U + D_POOL
OFF_Z = OFF_QKV + CONV_CH
OFF_B = OFF_Z + DN_VW
OFF_A = OFF_B + DN_HEADS
OFF_G = OFF_A + DN_HEADS
PEER_HEADS = 8
PEER_KEYS = 128
PEER_DQ = 256
PEER_TOPK = 16
PEER_PICKS = PEER_HEADS * PEER_TOPK
EPS = 1e-6
PAST_LEN = 4096

LANES = 128
SUBLANES = 8
POOL_PREFIX_ROWS = 16
CONV_PREFIX_ROWS = 8
VMEM_LIMIT = 56 * 1024 * 1024

F32 = jnp.float32
BF16 = jnp.bfloat16
HIGHEST = lax.Precision.HIGHEST


def _mm(a, b):
    return jnp.dot(a.astype(BF16), b.astype(BF16), preferred_element_type=F32)


def _mm_f32(a, b):
    return jnp.dot(a, b, precision=HIGHEST, preferred_element_type=F32)


def _mm_nt_f32(a, b):
    return lax.dot_general(a, b, (((1,), (1,)), ((), ())), precision=HIGHEST, preferred_element_type=F32)


def _mm_tn_f32(a, b):
    return lax.dot_general(a, b, (((0,), (0,)), ((), ())), precision=HIGHEST, preferred_element_type=F32)


def _sigmoid(x):
    return 1.0 / (1.0 + jnp.exp(-x))


def _silu(x):
    return x * _sigmoid(x)


def _softplus(x):
    return jnp.maximum(x, 0.0) + jnp.log1p(jnp.exp(-jnp.abs(x)))


def _rms(x, g):
    return x * lax.rsqrt(jnp.mean(x * x, axis=-1, keepdims=True) + EPS) * g


def _front_kernel(pos0, x_ref, pool_pre_ref, conv_pre_ref, g_mix_ref, w_u_ref, w_qkv_ref, w_z_ref, w_ba_ref, w_g_ref,
                  w_grp_ref, pool_scale_ref, w_conv_ref, a_log_ref, dt_ref,
                  ya_ref, q_ref, k_ref, v_ref, z_ref, graw_ref, bg_ref, pool_tail_ref, conv_tail_ref,
                  carry_u, carry_c):
    sb, tl, _ = x_ref.shape
    li = pl.program_id(1)

    @pl.when(li == 0)
    def _():
        carry_u[...] = pool_pre_ref[...]
        carry_c[...] = conv_pre_ref[...]

    x = x_ref[...].reshape(sb * tl, D_MODEL)
    h = _rms(x, g_mix_ref[...]).astype(BF16)
    u = jnp.dot(h, w_u_ref[...], preferred_element_type=F32)
    qkv = jnp.dot(h, w_qkv_ref[...], preferred_element_type=F32)
    z_ref[...] = jnp.dot(h, w_z_ref[...], preferred_element_type=F32).reshape(sb, tl, DN_VW)
    graw_ref[...] = jnp.dot(h, w_g_ref[...], preferred_element_type=F32).reshape(sb, tl, 2 * D_MODEL)
    ba = jnp.dot(h, w_ba_ref[...], preferred_element_type=F32)
    lane = lax.broadcasted_iota(jnp.int32, ba.shape, 1)
    beta = _sigmoid(ba)
    g = -jnp.exp(a_log_ref[...]) * _softplus(ba + dt_ref[...])
    bg_ref[...] = jnp.where(lane < DN_HEADS, beta, g).reshape(sb, tl, LANES)

    row = lax.broadcasted_iota(jnp.int32, (tl, POOL_GROUP), 0)
    pos1 = pos0 + li * tl + row + 1
    w_conv = w_conv_ref[...]

    for s in range(sb):
        u_s = u[s * tl:(s + 1) * tl]
        ext = jnp.concatenate([carry_u[s], u_s], axis=0)
        mixed = []
        for gi, w in enumerate(POOL_WINDOWS):
            acc = ext[:, gi * POOL_GROUP:(gi + 1) * POOL_GROUP]
            span = 1
            while span < w:
                acc = acc + pltpu.roll(acc, span, axis=0)
                span *= 2
            win = acc[POOL_PREFIX_ROWS:]
            cnt = jnp.minimum(pos1, w).astype(F32)
            pooled = win / cnt - u_s[:, gi * POOL_GROUP:(gi + 1) * POOL_GROUP]
            mixed.append(_mm(pooled, w_grp_ref[gi]))
        ya_ref[s] = jnp.concatenate(mixed, axis=1) * pool_scale_ref[...]
        pool_tail_ref[s] = ext[tl:]
        carry_u[s] = ext[tl:]

        c_s = qkv[s * tl:(s + 1) * tl]
        cext = jnp.concatenate([carry_c[s], c_s], axis=0)
        y = c_s * w_conv[CONV_W - 1:CONV_W]
        for j in range(1, CONV_W):
            y = y + pltpu.roll(cext, j, axis=0)[CONV_PREFIX_ROWS:] * w_conv[CONV_W - 1 - j:CONV_W - j]
        y = _silu(y)
        conv_tail_ref[s] = cext[tl:]
        carry_c[s] = cext[tl:]
        for hh in range(DN_HEADS):
            sl = slice(hh * DN_DK, (hh + 1) * DN_DK)
            qh = y[:, sl]
            q_ref[s, :, sl] = qh * lax.rsqrt(jnp.sum(qh * qh, axis=-1, keepdims=True) + EPS)
            kh = y[:, DN_QK + hh * DN_DK:DN_QK + (hh + 1) * DN_DK]
            k_ref[s, :, sl] = kh * lax.rsqrt(jnp.sum(kh * kh, axis=-1, keepdims=True) + EPS)
        v_ref[s] = y[:, 2 * DN_QK:]


def _front(x, pool_pre, conv_pre, pos0, sb, tl, g_mix, w_u, w_qkv, w_z, w_ba, w_g, w_grp, pool_scale, w_conv8,
           a_log_pad, dt_pad):
    b, l, _ = x.shape
    grid = (b // sb, l // tl)
    tok = lambda width: pl.BlockSpec((sb, tl, width), lambda i, j: (i, j, 0))
    seq = lambda rows, width: pl.BlockSpec((sb, rows, width), lambda i, j: (i, 0, 0))
    full = lambda a: pl.BlockSpec(a.shape, lambda i, j: (0,) * a.ndim)
    weights = (g_mix, w_u, w_qkv, w_z, w_ba, w_g, w_grp, pool_scale, w_conv8, a_log_pad, dt_pad)
    out_shape = (
        jax.ShapeDtypeStruct((b, l, D_POOL), F32),
        jax.ShapeDtypeStruct((b, l, DN_QK), F32),
        jax.ShapeDtypeStruct((b, l, DN_QK), F32),
        jax.ShapeDtypeStruct((b, l, DN_VW), F32),
        jax.ShapeDtypeStruct((b, l, DN_VW), F32),
        jax.ShapeDtypeStruct((b, l, 2 * D_MODEL), F32),
        jax.ShapeDtypeStruct((b, l, LANES), F32),
        jax.ShapeDtypeStruct((b, POOL_PREFIX_ROWS, D_POOL), F32),
        jax.ShapeDtypeStruct((b, CONV_PREFIX_ROWS, CONV_CH), F32),
    )
    return pl.pallas_call(
        functools.partial(_front_kernel, pos0),
        out_shape=out_shape,
        grid=grid,
        in_specs=[tok(D_MODEL), seq(POOL_PREFIX_ROWS, D_POOL), seq(CONV_PREFIX_ROWS, CONV_CH)] + [full(a) for a in weights],
        out_specs=(tok(D_POOL), tok(DN_QK), tok(DN_QK), tok(DN_VW), tok(DN_VW), tok(2 * D_MODEL), tok(LANES),
                   seq(POOL_PREFIX_ROWS, D_POOL), seq(CONV_PREFIX_ROWS, CONV_CH)),
        scratch_shapes=[pltpu.VMEM((sb, POOL_PREFIX_ROWS, D_POOL), F32), pltpu.VMEM((sb, CONV_PREFIX_ROWS, CONV_CH), F32)],
        compiler_params=pltpu.CompilerParams(dimension_semantics=("arbitrary", "arbitrary"), vmem_limit_bytes=VMEM_LIMIT),
        name="mixer_front",
    )(x, pool_pre, conv_pre, *weights)


_NN = (((1,), (0,)), ((), ()))
_NT = (((1,), (1,)), ((), ()))
_TN = (((0,), (0,)), ((), ()))


def _dot_bf16(a, b, dims):
    return lax.dot_general(a.astype(BF16), b.astype(BF16), dims, preferred_element_type=F32)


def _mm_nt(a, b):
    return _dot_bf16(a, b, _NT)


def _mm_chain(a, b):
    return _dot_bf16(a, b, _NN)


def _mm_state(a, b):
    return _dot_bf16(a, b, _NN)


def _mm_tn_state(a, b):
    return _dot_bf16(a, b, _TN)


def _delta_prep_kernel(q_ref, k_ref, v_ref, bg_ref, uv_ref, wk_ref, qd_ref, kd_ref, qk_ref, gl_ref):
    sb = q_ref.shape[0]
    ncb = q_ref.shape[1] // CHUNK
    row = lax.broadcasted_iota(jnp.int32, (CHUNK, CHUNK), 0)
    col = lax.broadcasted_iota(jnp.int32, (CHUNK, CHUNK), 1)
    incl = row >= col
    strict = row > col
    tri = incl.astype(F32)
    xs, pw = [], []
    for s in range(sb):
        for c in range(ncb):
            rows = slice(c * CHUNK, (c + 1) * CHUNK)
            bg = bg_ref[s, rows, :]
            gc_all = _mm_f32(tri, bg)
            gc_rows = gc_all.T
            gl_ref[s, c * SUBLANES:(c + 1) * SUBLANES, :] = gc_all[CHUNK - SUBLANES:, :]
            for hh in range(DN_HEADS):
                sl = slice(hh * DN_DK, (hh + 1) * DN_DK)
                q = q_ref[s, rows, sl] * (DN_DK ** -0.5)
                k = k_ref[s, rows, sl]
                beta = bg[:, hh:hh + 1]
                gcol = gc_all[:, DN_HEADS + hh:DN_HEADS + hh + 1]
                grow = gc_rows[DN_HEADS + hh:DN_HEADS + hh + 1, :]
                decay = jnp.exp(jnp.where(incl, gcol - grow, -jnp.inf))
                egc = jnp.exp(gcol)
                kb = k * beta
                pw.append(-jnp.where(strict, _mm_nt(kb, k) * decay, 0.0))
                xs.append(jnp.concatenate([v_ref[s, rows, sl] * beta, kb * egc], axis=1))
                qk_ref[s, rows, hh * CHUNK:(hh + 1) * CHUNK] = _mm_nt(q, k) * decay
                gl = gcol[CHUNK - 1:CHUNK, :]
                qd_ref[s, rows, sl] = q * egc
                kd_ref[s, rows, sl] = k * jnp.exp(gl - gcol)
    span = 1
    while True:
        xs = [x + _mm_chain(p, x) for p, x in zip(pw, xs)]
        span *= 2
        if span >= CHUNK:
            break
        pw = [_mm_chain(p, p) for p in pw]
    i = 0
    for s in range(sb):
        for c in range(ncb):
            rows = slice(c * CHUNK, (c + 1) * CHUNK)
            for hh in range(DN_HEADS):
                sl = slice(hh * DN_DK, (hh + 1) * DN_DK)
                uv_ref[s, rows, sl] = xs[i][:, :DN_DV]
                wk_ref[s, rows, sl] = xs[i][:, DN_DV:]
                i += 1


def _delta_scan_kernel(uv_ref, wk_ref, qd_ref, kd_ref, qk_ref, gl_ref, z_ref, s0_ref, g_out_ref, yb_ref, s_out_ref, s_scr):
    ci = pl.program_id(1)
    bb = uv_ref.shape[0]

    @pl.when(ci == 0)
    def _():
        s_scr[...] = s0_ref[...]

    g_out = g_out_ref[...]
    probs = [(b, hh) for b in range(bb) for hh in range(DN_HEADS)]
    lanes = lambda hh: slice(hh * DN_DK, (hh + 1) * DN_DK)
    s_old = [s_scr[b, hh] for b, hh in probs]
    v_new = [uv_ref[b, :, lanes(hh)] - _mm_state(wk_ref[b, :, lanes(hh)], s) for (b, hh), s in zip(probs, s_old)]
    o_state = [_mm_state(qd_ref[b, :, lanes(hh)], s) for (b, hh), s in zip(probs, s_old)]
    for (b, hh), s, vn, os_ in zip(probs, s_old, v_new, o_state):
        o = os_ + _mm_state(qk_ref[b, :, hh * CHUNK:(hh + 1) * CHUNK], vn)
        egl = jnp.exp(gl_ref[b, SUBLANES - 1:SUBLANES, DN_HEADS + hh:DN_HEADS + hh + 1])
        s_new = s * egl + _mm_tn_state(kd_ref[b, :, lanes(hh)], vn)
        s_scr[b, hh] = s_new
        s_out_ref[b, hh] = s_new
        yb_ref[b, :, lanes(hh)] = _rms(o, g_out) * _silu(z_ref[b, :, lanes(hh)])


DELTA_PREP_CHUNKS = 4
DELTA_SCAN_SEQS = 8


def _delta(q, k, v, z, bg, s0, g_out):
    b, l, _ = q.shape
    nc = l // CHUNK
    ncb = min(DELTA_PREP_CHUNKS, nc)
    sb = min(b, DELTA_PREP_CHUNKS // ncb)
    rows = ncb * CHUNK
    tok = lambda width: pl.BlockSpec((sb, rows, width), lambda i, j: (i, j, 0))
    glspec = pl.BlockSpec((sb, ncb * SUBLANES, LANES), lambda i, j: (i, j, 0))
    wide = lambda width: jax.ShapeDtypeStruct((b, l, width), F32)
    uv, wk, qd, kd, qk, gl = pl.pallas_call(
        _delta_prep_kernel,
        out_shape=(wide(DN_VW), wide(DN_QK), wide(DN_QK), wide(DN_QK), wide(DN_HEADS * CHUNK),
                   jax.ShapeDtypeStruct((b, nc * SUBLANES, LANES), F32)),
        grid=(b // sb, nc // ncb),
        in_specs=[tok(DN_QK), tok(DN_QK), tok(DN_VW), tok(LANES)],
        out_specs=(tok(DN_VW), tok(DN_QK), tok(DN_QK), tok(DN_QK), tok(DN_HEADS * CHUNK), glspec),
        compiler_params=pltpu.CompilerParams(dimension_semantics=("arbitrary", "arbitrary"), vmem_limit_bytes=VMEM_LIMIT),
        name="delta_prep",
    )(q, k, v, bg)

    bb = min(DELTA_SCAN_SEQS, b)
    ctok = lambda width: pl.BlockSpec((bb, CHUNK, width), lambda i, j: (i, j, 0))
    st = pl.BlockSpec((bb, DN_HEADS, DN_DK, DN_DV), lambda i, j: (i, 0, 0, 0))
    return pl.pallas_call(
        _delta_scan_kernel,
        out_shape=(wide(DN_VW), jax.ShapeDtypeStruct((b, DN_HEADS, DN_DK, DN_DV), F32)),
        grid=(b // bb, nc),
        in_specs=[ctok(DN_VW), ctok(DN_QK), ctok(DN_QK), ctok(DN_QK), ctok(DN_HEADS * CHUNK),
                  pl.BlockSpec((bb, SUBLANES, LANES), lambda i, j: (i, j, 0)), ctok(DN_VW), st,
                  pl.BlockSpec((1, DN_DV), lambda i, j: (0, 0))],
        out_specs=(ctok(DN_VW), st),
        scratch_shapes=[pltpu.VMEM((bb, DN_HEADS, DN_DK, DN_DV), F32)],
        compiler_params=pltpu.CompilerParams(dimension_semantics=("arbitrary", "arbitrary"), vmem_limit_bytes=VMEM_LIMIT),
        name="delta_scan",
    )(uv, wk, qd, kd, qk, gl, z, s0, g_out)


def _merge_kernel(x_ref, ya_ref, yb_ref, graw_ref, w_up_pool_ref, w_up_dn_ref, w_out_ref, g_ffn_ref, x2_ref, xn_ref):
    graw = graw_ref[...]
    ga = _sigmoid(graw[:, :D_MODEL])
    gb = _sigmoid(graw[:, D_MODEL:])
    merged = ga * _mm(ya_ref[...], w_up_pool_ref[...]) + gb * _mm(yb_ref[...], w_up_dn_ref[...])
    x2 = x_ref[...] + _mm(merged, w_out_ref[...])
    x2_ref[...] = x2
    xn_ref[...] = _rms(x2, g_ffn_ref[...])


def _merge(x, ya, yb, graw, w_up_pool, w_up_dn, w_out, g_ffn, tm):
    t = x.shape[0]
    tok = lambda width: pl.BlockSpec((tm, width), lambda i: (i, 0))
    full = lambda a: pl.BlockSpec(a.shape, lambda i: (0,) * a.ndim)
    weights = (w_up_pool, w_up_dn, w_out, g_ffn)
    return pl.pallas_call(
        _merge_kernel,
        out_shape=(jax.ShapeDtypeStruct((t, D_MODEL), F32), jax.ShapeDtypeStruct((t, D_MODEL), F32)),
        grid=(t // tm,),
        in_specs=[tok(D_MODEL), tok(D_POOL), tok(DN_VW), tok(2 * D_MODEL)] + [full(a) for a in weights],
        out_specs=(tok(D_MODEL), tok(D_MODEL)),
        compiler_params=pltpu.CompilerParams(dimension_semantics=("arbitrary",), vmem_limit_bytes=VMEM_LIMIT),
        name="branch_merge",
    )(x, ya, yb, graw, *weights)


def _top16_rows(s, ids=None):
    if ids is None:
        ids = lax.broadcasted_iota(jnp.int32, s.shape, 0)
    ids = ids.astype(F32)
    vals, idxs = [], []
    for _ in range(PEER_TOPK):
        m = jnp.max(s, axis=0, keepdims=True)
        idx = jnp.min(jnp.where(s == m, ids, jnp.inf), axis=0, keepdims=True)
        vals.append(m)
        idxs.append(idx)
        s = jnp.where(ids == idx, -jnp.inf, s)
    return jnp.concatenate(vals, axis=0), jnp.concatenate(idxs, axis=0).astype(jnp.int32)


def _pair_candidates(v1, v2):
    tokens = v1.shape[1]
    sub = lax.broadcasted_iota(jnp.int32, (SUBLANES, tokens), 0)
    vals = [v1[0:1] + v2[0:SUBLANES], v1[0:1] + v2[SUBLANES:]]
    ids = [sub, sub + SUBLANES]
    for a in range(1, SUBLANES):
        vals.append(v1[a:a + 1] + v2[0:SUBLANES])
        ids.append(sub + a * PEER_TOPK)
    vals.append(v1[SUBLANES:] + v2[0:1])
    ids.append((sub + SUBLANES) * PEER_TOPK)
    return jnp.concatenate(vals, axis=0), jnp.concatenate(ids, axis=0)


def _take_rows(table, idx):
    out = jnp.zeros_like(table)
    for a in range(PEER_TOPK):
        out = jnp.where(idx == a, table[a:a + 1, :], out)
    return out


def _route_kernel(xn_ref, wq_ref, keys_ref, experts_ref, gates_ref):
    half = PEER_DQ // 2
    q = jnp.dot(xn_ref[...].astype(BF16), wq_ref[...], preferred_element_type=F32).astype(BF16)
    experts, gates = [], []
    for hh in range(PEER_HEADS):
        q1 = q[:, hh * PEER_DQ:hh * PEER_DQ + half]
        q2 = q[:, hh * PEER_DQ + half:(hh + 1) * PEER_DQ]
        nt = (((1,), (1,)), ((), ()))
        s1 = lax.dot_general(keys_ref[0, hh], q1, nt, preferred_element_type=F32)
        s2 = lax.dot_general(keys_ref[1, hh], q2, nt, preferred_element_type=F32)
        v1, i1 = _top16_rows(s1)
        v2, i2 = _top16_rows(s2)
        cv, ci = _top16_rows(*_pair_candidates(v1, v2))
        e1 = _take_rows(i1, ci // PEER_TOPK)
        e2 = _take_rows(i2, ci % PEER_TOPK)
        experts.append(e1 * PEER_KEYS + e2)
        ex = jnp.exp(cv - cv[0:1, :])
        gates.append(ex / jnp.sum(ex, axis=0, keepdims=True))
    experts_ref[...] = jnp.concatenate(experts, axis=0).T
    gates_ref[...] = jnp.concatenate(gates, axis=0).T


def _route(xn, wq, keys, tm):
    t = xn.shape[0]
    return pl.pallas_call(
        _route_kernel,
        out_shape=(jax.ShapeDtypeStruct((t, PEER_PICKS), jnp.int32), jax.ShapeDtypeStruct((t, PEER_PICKS), F32)),
        grid=(t // tm,),
        in_specs=[pl.BlockSpec((tm, D_MODEL), lambda i: (i, 0)),
                  pl.BlockSpec(wq.shape, lambda i: (0, 0)),
                  pl.BlockSpec(keys.shape, lambda i: (0, 0, 0, 0))],
        out_specs=(pl.BlockSpec((tm, PEER_PICKS), lambda i: (i, 0)), pl.BlockSpec((tm, PEER_PICKS), lambda i: (i, 0))),
        compiler_params=pltpu.CompilerParams(dimension_semantics=("arbitrary",), vmem_limit_bytes=VMEM_LIMIT),
        name="peer_route",
    )(xn, wq, keys)


EXPERT_TOKENS = 16


def _experts_kernel(idx_ref, idx_next_ref, gates_ref, xn_ref, x2_ref, g_final_ref, tab_ref, out_ref, buf_a, buf_b, cb, sem):
    bufs = (buf_a, buf_b)
    i = pl.program_id(0)
    n = pl.num_programs(0)
    tb = EXPERT_TOKENS
    lane = lax.broadcasted_iota(jnp.int32, (PEER_HEADS, PEER_TOPK), 1)
    g_final = g_final_ref[...]

    def issue_token(ref, t, sl):
        for h in range(PEER_HEADS):
            for k in range(PEER_TOPK):
                e = ref[0, t, h * PEER_TOPK + k]
                pltpu.async_copy(tab_ref.at[e], bufs[sl].at[t, k, h], sem.at[sl], priority=k % 2)

    def wait_slot(sl):
        for t in range(tb):
            pltpu.make_async_copy(bufs[sl].at[t], bufs[sl].at[t], sem.at[sl]).wait()

    tile_row = lax.broadcasted_iota(jnp.int32, (SUBLANES, LANES), 0)

    def as_tile(ref, t):
        tile = jnp.broadcast_to(ref[t:t + 1, :LANES], (SUBLANES, LANES))
        for j in range(1, SUBLANES):
            tile = jnp.where(tile_row == j, jnp.broadcast_to(ref[t:t + 1, j * LANES:(j + 1) * LANES], (SUBLANES, LANES)), tile)
        return tile

    def compute_token(t, sl):
        x = as_tile(xn_ref, t)
        sub = lax.broadcasted_iota(jnp.int32, (PEER_HEADS, PEER_TOPK), 0)
        act = jnp.zeros((PEER_HEADS, PEER_TOPK), F32)
        for h in range(PEER_HEADS):
            part = jnp.zeros((SUBLANES, PEER_TOPK), F32)
            for k in range(PEER_TOPK):
                part = jnp.where(lane == k, jnp.sum(bufs[sl][t, k, h, :SUBLANES, :] * x, axis=-1, keepdims=True), part)
            act = jnp.where(sub == h, jnp.sum(part, axis=0, keepdims=True), act)
        coef = gates_ref[t] * (0.5 * act * (1.0 + lax.erf(act * (0.5 ** 0.5))))
        for k in range(PEER_TOPK):
            cb[t, k] = jnp.broadcast_to(coef[:, k:k + 1], (PEER_HEADS, LANES))
        accs = [jnp.zeros((SUBLANES, LANES), F32) for _ in range(4)]
        for k in range(PEER_TOPK):
            for h in range(PEER_HEADS):
                c = jnp.broadcast_to(cb[t, k, h:h + 1, :], (SUBLANES, LANES))
                accs[h % 4] = accs[h % 4] + c * bufs[sl][t, k, h, SUBLANES:, :]
        z = as_tile(x2_ref, t) + ((accs[0] + accs[1]) + (accs[2] + accs[3]))
        ms = jnp.sum(jnp.sum(z * z, axis=-1, keepdims=True), axis=0, keepdims=True) * (1.0 / D_MODEL)
        res = z * lax.rsqrt(ms + EPS) * g_final
        for j in range(SUBLANES):
            out_ref[t:t + 1, j * LANES:(j + 1) * LANES] = res[j:j + 1, :]

    @pl.when(i == 0)
    def _():
        for t in range(tb):
            issue_token(idx_ref, t, 0)

    def step(sl):
        wait_slot(sl)
        for t in range(tb):
            issue_token(idx_next_ref, t, 1 - sl)
            compute_token(t, sl)

        @pl.when(i == n - 1)
        def _():
            wait_slot(1 - sl)

    @pl.when(i % 2 == 0)
    def _():
        step(0)

    @pl.when(i % 2 == 1)
    def _():
        step(1)


def _experts(experts, gates, xn, x2, g_final, table):
    t = xn.shape[0]
    tb = EXPERT_TOKENS
    nb = t // tb
    idx = experts.reshape(nb, tb, PEER_PICKS)
    gates = gates.reshape(t, PEER_HEADS, PEER_TOPK)
    tok = pl.BlockSpec((tb, D_MODEL), lambda i: (i, 0))
    return pl.pallas_call(
        _experts_kernel,
        out_shape=jax.ShapeDtypeStruct((t, D_MODEL), F32),
        grid=(nb,),
        in_specs=[pl.BlockSpec((1, tb, PEER_PICKS), lambda i: (i, 0, 0), memory_space=pltpu.SMEM),
                  pl.BlockSpec((1, tb, PEER_PICKS), lambda i: (jnp.minimum(i + 1, nb - 1), 0, 0), memory_space=pltpu.SMEM),
                  pl.BlockSpec((tb, PEER_HEADS, PEER_TOPK), lambda i: (i, 0, 0)), tok, tok,
                  pl.BlockSpec((SUBLANES, LANES), lambda i: (0, 0)),
                  pl.BlockSpec(memory_space=pl.ANY)],
        out_specs=tok,
        scratch_shapes=[pltpu.VMEM((tb, PEER_TOPK, PEER_HEADS, 2 * SUBLANES, LANES), F32),
                        pltpu.VMEM((tb, PEER_TOPK, PEER_HEADS, 2 * SUBLANES, LANES), F32),
                        pltpu.VMEM((tb, PEER_TOPK, PEER_HEADS, LANES), F32),
                        pltpu.SemaphoreType.DMA((2,))],
        compiler_params=pltpu.CompilerParams(dimension_semantics=("arbitrary",), vmem_limit_bytes=VMEM_LIMIT),
        name="peer_experts",
    )(idx, idx, gates, xn, x2, g_final.reshape(SUBLANES, LANES), table)


def _pad_rows_front(a, rows):
    return jnp.pad(a, ((0, 0), (rows - a.shape[1], 0), (0, 0)))


MERGE_ROWS = 512
ROUTE_ROWS = 256
FRONT_ROWS = 512


def _group(x, pool_prev, conv_prev, s_prev, pos0, p):
    b, l, _ = x.shape
    tl = min(FRONT_ROWS, l)
    sb = min(b, FRONT_ROWS // tl)
    tm_merge = min(MERGE_ROWS, b * l)
    tm_route = min(ROUTE_ROWS, b * l)
    ya, q, k, v, z, graw, bg, pool_tail, conv_tail = _front(
        x, _pad_rows_front(pool_prev, POOL_PREFIX_ROWS), _pad_rows_front(conv_prev, CONV_PREFIX_ROWS), pos0, sb, tl,
        p["g_mix"], p["w_u"], p["w_qkv"], p["w_z"], p["w_ba"], p["w_g"], p["w_grp"], p["pool_scale"], p["w_conv8"],
        p["a_log_pad"], p["dt_pad"])
    yb, s_new = _delta(q, k, v, z, bg, s_prev, p["g_dn_out"])
    t = b * l
    flat = lambda a: a.reshape(t, a.shape[-1])
    x2, xn = _merge(flat(x), flat(ya), flat(yb), flat(graw), p["w_up_pool"], p["w_up_dn"], p["w_out"], p["g_ffn"], tm_merge)
    experts, gates = _route(xn, p["w_peer_q"], p["peer_keys"], tm_route)
    y = _experts(experts, gates, xn, x2, p["g_final"], p["peer_table"])
    return (y.reshape(b, l, D_MODEL), pool_tail[:, POOL_PREFIX_ROWS - POOL_STATE:],
            conv_tail[:, CONV_PREFIX_ROWS - (CONV_W - 1):], s_new)


def kernel(x_prompt, x_sample, cache_pool, state_dn_conv, state_dn, g_mix, w_in, w_pool_grp, pool_scale, w_conv, a_log,
           dt_bias, g_dn_out, w_up_pool, w_up_dn, w_out, g_ffn, w_peer_q, peer_sub_keys, peer_u, peer_v, g_final):
    depth = w_in.shape[0]
    assert depth == 1
    bp = x_prompt.shape[0]
    lane_pad = lambda a, off: jnp.pad(a.astype(F32)[None, :], ((0, 0), (off, LANES - off - a.shape[0])))
    w = w_in[0]
    params = {
        "g_mix": g_mix[0][None, :],
        "w_u": w[:, OFF_U:OFF_QKV].astype(BF16),
        "w_qkv": w[:, OFF_QKV:OFF_Z].astype(BF16),
        "w_z": w[:, OFF_Z:OFF_B].astype(BF16),
        "w_ba": jnp.pad(w[:, OFF_B:OFF_G], ((0, 0), (0, LANES - 2 * DN_HEADS))).astype(BF16),
        "w_g": w[:, OFF_G:].astype(BF16),
        "w_grp": w_pool_grp[0].astype(BF16),
        "pool_scale": pool_scale[0][None, :],
        "w_conv8": jnp.pad(w_conv[0], ((0, SUBLANES - CONV_W), (0, 0))),
        "a_log_pad": lane_pad(a_log[0], DN_HEADS),
        "dt_pad": lane_pad(dt_bias[0], DN_HEADS),
        "g_dn_out": g_dn_out[0][None, :],
        "w_up_pool": w_up_pool[0].astype(BF16),
        "w_up_dn": w_up_dn[0].astype(BF16),
        "w_out": w_out[0].astype(BF16),
        "g_ffn": g_ffn[0][None, :],
        "w_peer_q": w_peer_q[0].reshape(D_MODEL, PEER_HEADS * PEER_DQ).astype(BF16),
        "peer_keys": peer_sub_keys[0].astype(BF16),
        "peer_table": jnp.concatenate([peer_u[0].reshape(-1, SUBLANES, LANES), peer_v[0].reshape(-1, SUBLANES, LANES)], axis=1),
        "g_final": g_final[None, :],
    }
    zeros = lambda *shape: jnp.zeros(shape, F32)
    yp, pool_p, conv_p, dn_p = _group(
        x_prompt, zeros(bp, POOL_STATE, D_POOL), zeros(bp, CONV_W - 1, CONV_CH), zeros(bp, DN_HEADS, DN_DK, DN_DV),
        0, params)
    ys, pool_s, conv_s, dn_s = _group(
        x_sample, cache_pool[0], state_dn_conv[0], state_dn[0].astype(F32),
        PAST_LEN, params)
    return (yp, ys, pool_p[None], conv_p[None], dn_p[None].astype(state_dn.dtype),
            pool_s[None], conv_s[None], dn_s[None].astype(state_dn.dtype))
```

```python
import functools

import jax
import jax.numpy as jnp
from jax import lax
from jax.experimental import pallas as pl
from jax.experimental.pallas import tpu as pltpu

D_MODEL = 1024
CHUNK = 64
D_POOL = 512
POOL_WINDOWS = (2, 4, 8, 16)
POOL_GROUP = 128
POOL_STATE = 15
DN_HEADS = 4
DN_DK = 128
DN_DV = 128
DN_QK = DN_HEADS * DN_DK
DN_VW = DN_HEADS * DN_DV
CONV_W = 4
CONV_CH = 2 * DN_QK + DN_VW
OFF_U = 0
OFF_QKV = OFF_U + D_POOL
OFF_Z = OFF_QKV + CONV_CH
OFF_B = OFF_Z + DN_VW
OFF_A = OFF_B + DN_HEADS
OFF_G = OFF_A + DN_HEADS
PEER_HEADS = 8
PEER_KEYS = 128
PEER_DQ = 256
PEER_TOPK = 16
PEER_PICKS = PEER_HEADS * PEER_TOPK
EPS = 1e-6
PAST_LEN = 4096

LANES = 128
SUBLANES = 8
POOL_PREFIX_ROWS = 16
CONV_PREFIX_ROWS = 8
VMEM_LIMIT = 56 * 1024 * 1024

F32 = jnp.float32
BF16 = jnp.bfloat16
HIGHEST = lax.Precision.HIGHEST


def _mm(a, b):
    return jnp.dot(a.astype(BF16), b.astype(BF16), preferred_element_type=F32)


def _mm_f32(a, b):
    return jnp.dot(a, b, precision=HIGHEST, preferred_element_type=F32)


def _mm_nt_f32(a, b):
    return lax.dot_general(a, b, (((1,), (1,)), ((), ())), precision=HIGHEST, preferred_element_type=F32)


def _mm_tn_f32(a, b):
    return lax.dot_general(a, b, (((0,), (0,)), ((), ())), precision=HIGHEST, preferred_element_type=F32)


def _sigmoid(x):
    return 1.0 / (1.0 + jnp.exp(-x))


def _silu(x):
    return x * _sigmoid(x)


def _softplus(x):
    return jnp.maximum(x, 0.0) + jnp.log1p(jnp.exp(-jnp.abs(x)))


def _rms(x, g):
    return x * lax.rsqrt(jnp.mean(x * x, axis=-1, keepdims=True) + EPS) * g


def _front_kernel(pos0, x_ref, pool_pre_ref, conv_pre_ref, g_mix_ref, w_u_ref, w_qkv_ref, w_z_ref, w_ba_ref, w_g_ref,
                  w_grp_ref, pool_scale_ref, w_conv_ref, a_log_ref, dt_ref,
                  ya_ref, q_ref, k_ref, v_ref, z_ref, graw_ref, bg_ref, pool_tail_ref, conv_tail_ref,
                  carry_u, carry_c):
    sb, tl, _ = x_ref.shape
    li = pl.program_id(1)

    @pl.when(li == 0)
    def _():
        carry_u[...] = pool_pre_ref[...]
        carry_c[...] = conv_pre_ref[...]

    x = x_ref[...].reshape(sb * tl, D_MODEL)
    h = _rms(x, g_mix_ref[...]).astype(BF16)
    u = jnp.dot(h, w_u_ref[...], preferred_element_type=F32)
    qkv = jnp.dot(h, w_qkv_ref[...], preferred_element_type=F32)
    z_ref[...] = jnp.dot(h, w_z_ref[...], preferred_element_type=F32).reshape(sb, tl, DN_VW)
    graw_ref[...] = jnp.dot(h, w_g_ref[...], preferred_element_type=F32).reshape(sb, tl, 2 * D_MODEL)
    ba = jnp.dot(h, w_ba_ref[...], preferred_element_type=F32)
    lane = lax.broadcasted_iota(jnp.int32, ba.shape, 1)
    beta = _sigmoid(ba)
    g = -jnp.exp(a_log_ref[...]) * _softplus(ba + dt_ref[...])
    bg_ref[...] = jnp.where(lane < DN_HEADS, beta, g).reshape(sb, tl, LANES)

    row = lax.broadcasted_iota(jnp.int32, (tl, POOL_GROUP), 0)
    pos1 = pos0 + li * tl + row + 1
    w_conv = w_conv_ref[...]

    for s in range(sb):
        u_s = u[s * tl:(s + 1) * tl]
        ext = jnp.concatenate([carry_u[s], u_s], axis=0)
        mixed = []
        for gi, w in enumerate(POOL_WINDOWS):
            acc = ext[:, gi * POOL_GROUP:(gi + 1) * POOL_GROUP]
            span = 1
            while span < w:
                acc = acc + pltpu.roll(acc, span, axis=0)
                span *= 2
            win = acc[POOL_PREFIX_ROWS:]
            cnt = jnp.minimum(pos1, w).astype(F32)
            pooled = win / cnt - u_s[:, gi * POOL_GROUP:(gi + 1) * POOL_GROUP]
            mixed.append(_mm(pooled, w_grp_ref[gi]))
        ya_ref[s] = jnp.concatenate(mixed, axis=1) * pool_scale_ref[...]
        pool_tail_ref[s] = ext[tl:]
        carry_u[s] = ext[tl:]

        c_s = qkv[s * tl:(s + 1) * tl]
        cext = jnp.concatenate([carry_c[s], c_s], axis=0)
        y = c_s * w_conv[CONV_W - 1:CONV_W]
        for j in range(1, CONV_W):
            y = y + pltpu.roll(cext, j, axis=0)[CONV_PREFIX_ROWS:] * w_conv[CONV_W - 1 - j:CONV_W - j]
        y = _silu(y)
        conv_tail_ref[s] = cext[tl:]
        carry_c[s] = cext[tl:]
        for hh in range(DN_HEADS):
            sl = slice(hh * DN_DK, (hh + 1) * DN_DK)
            qh = y[:, sl]
            q_ref[s, :, sl] = qh * lax.rsqrt(jnp.sum(qh * qh, axis=-1, keepdims=True) + EPS)
            kh = y[:, DN_QK + hh * DN_DK:DN_QK + (hh + 1) * DN_DK]
            k_ref[s, :, sl] = kh * lax.rsqrt(jnp.sum(kh * kh, axis=-1, keepdims=True) + EPS)
        v_ref[s] = y[:, 2 * DN_QK:]


def _front(x, pool_pre, conv_pre, pos0, sb, tl, g_mix, w_u, w_qkv, w_z, w_ba, w_g, w_grp, pool_scale, w_conv8,
           a_log_pad, dt_pad):
    b, l, _ = x.shape
    grid = (b // sb, l // tl)
    tok = lambda width: pl.BlockSpec((sb, tl, width), lambda i, j: (i, j, 0))
    seq = lambda rows, width: pl.BlockSpec((sb, rows, width), lambda i, j: (i, 0, 0))
    full = lambda a: pl.BlockSpec(a.shape, lambda i, j: (0,) * a.ndim)
    weights = (g_mix, w_u, w_qkv, w_z, w_ba, w_g, w_grp, pool_scale, w_conv8, a_log_pad, dt_pad)
    out_shape = (
        jax.ShapeDtypeStruct((b, l, D_POOL), F32),
        jax.ShapeDtypeStruct((b, l, DN_QK), F32),
        jax.ShapeDtypeStruct((b, l, DN_QK), F32),
        jax.ShapeDtypeStruct((b, l, DN_VW), F32),
        jax.ShapeDtypeStruct((b, l, DN_VW), F32),
        jax.ShapeDtypeStruct((b, l, 2 * D_MODEL), F32),
        jax.ShapeDtypeStruct((b, l, LANES), F32),
        jax.ShapeDtypeStruct((b, POOL_PREFIX_ROWS, D_POOL), F32),
        jax.ShapeDtypeStruct((b, CONV_PREFIX_ROWS, CONV_CH), F32),
    )
    return pl.pallas_call(
        functools.partial(_front_kernel, pos0),
        out_shape=out_shape,
        grid=grid,
        in_specs=[tok(D_MODEL), seq(POOL_PREFIX_ROWS, D_POOL), seq(CONV_PREFIX_ROWS, CONV_CH)] + [full(a) for a in weights],
        out_specs=(tok(D_POOL), tok(DN_QK), tok(DN_QK), tok(DN_VW), tok(DN_VW), tok(2 * D_MODEL), tok(LANES),
                   seq(POOL_PREFIX_ROWS, D_POOL), seq(CONV_PREFIX_ROWS, CONV_CH)),
        scratch_shapes=[pltpu.VMEM((sb, POOL_PREFIX_ROWS, D_POOL), F32), pltpu.VMEM((sb, CONV_PREFIX_ROWS, CONV_CH), F32)],
        compiler_params=pltpu.CompilerParams(dimension_semantics=("arbitrary", "arbitrary"), vmem_limit_bytes=VMEM_LIMIT),
        name="mixer_front",
    )(x, pool_pre, conv_pre, *weights)


_NN = (((1,), (0,)), ((), ()))
_NT = (((1,), (1,)), ((), ()))
_TN = (((0,), (0,)), ((), ()))


def _dot_bf16(a, b, dims):
    return lax.dot_general(a.astype(BF16), b.astype(BF16), dims, preferred_element_type=F32)


def _mm_nt(a, b):
    return _dot_bf16(a, b, _NT)


def _mm_chain(a, b):
    return _dot_bf16(a, b, _NN)


def _mm_state(a, b):
    return _dot_bf16(a, b, _NN)


def _mm_tn_state(a, b):
    return _dot_bf16(a, b, _TN)


def _delta_prep_kernel(q_ref, k_ref, v_ref, bg_ref, uv_ref, wk_ref, qd_ref, kd_ref, qk_ref, gl_ref):
    sb = q_ref.shape[0]
    ncb = q_ref.shape[1] // CHUNK
    row = lax.broadcasted_iota(jnp.int32, (CHUNK, CHUNK), 0)
    col = lax.broadcasted_iota(jnp.int32, (CHUNK, CHUNK), 1)
    incl = row >= col
    strict = row > col
    tri = incl.astype(F32)
    xs, pw = [], []
    for s in range(sb):
        for c in range(ncb):
            rows = slice(c * CHUNK, (c + 1) * CHUNK)
            bg = bg_ref[s, rows, :]
            gc_all = _mm_f32(tri, bg)
            gc_rows = gc_all.T
            gl_ref[s, c * SUBLANES:(c + 1) * SUBLANES, :] = gc_all[CHUNK - SUBLANES:, :]
            for hh in range(DN_HEADS):
                sl = slice(hh * DN_DK, (hh + 1) * DN_DK)
                q = q_ref[s, rows, sl] * (DN_DK ** -0.5)
                k = k_ref[s, rows, sl]
                beta = bg[:, hh:hh + 1]
                gcol = gc_all[:, DN_HEADS + hh:DN_HEADS + hh + 1]
                grow = gc_rows[DN_HEADS + hh:DN_HEADS + hh + 1, :]
                decay = jnp.exp(jnp.where(incl, gcol - grow, -jnp.inf))
                egc = jnp.exp(gcol)
                kb = k * beta
                pw.append(-jnp.where(strict, _mm_nt(kb, k) * decay, 0.0))
                xs.append(jnp.concatenate([v_ref[s, rows, sl] * beta, kb * egc], axis=1))
                qk_ref[s, rows, hh * CHUNK:(hh + 1) * CHUNK] = _mm_nt(q, k) * decay
                gl = gcol[CHUNK - 1:CHUNK, :]
                qd_ref[s, rows, sl] = q * egc
                kd_ref[s, rows, sl] = k * jnp.exp(gl - gcol)
    span = 1
    while True:
        xs = [x + _mm_chain(p, x) for p, x in zip(pw, xs)]
        span *= 2
        if span >= CHUNK:
            break
        pw = [_mm_chain(p, p) for p in pw]
    i = 0
    for s in range(sb):
        for c in range(ncb):
            rows = slice(c * CHUNK, (c + 1) * CHUNK)
            for hh in range(DN_HEADS):
                sl = slice(hh * DN_DK, (hh + 1) * DN_DK)
                uv_ref[s, rows, sl] = xs[i][:, :DN_DV]
                wk_ref[s, rows, sl] = xs[i][:, DN_DV:]
                i += 1


def _delta_scan_kernel(uv_ref, wk_ref, qd_ref, kd_ref, qk_ref, gl_ref, z_ref, s0_ref, g_out_ref, yb_ref, s_out_ref, s_scr):
    ci = pl.program_id(1)
    bb = uv_ref.shape[0]

    @pl.when(ci == 0)
    def _():
        s_scr[...] = s0_ref[...]

    g_out = g_out_ref[...]
    probs = [(b, hh) for b in range(bb) for hh in range(DN_HEADS)]
    lanes = lambda hh: slice(hh * DN_DK, (hh + 1) * DN_DK)
    s_old = [s_scr[b, hh] for b, hh in probs]
    v_new = [uv_ref[b, :, lanes(hh)] - _mm_state(wk_ref[b, :, lanes(hh)], s) for (b, hh), s in zip(probs, s_old)]
    o_state = [_mm_state(qd_ref[b, :, lanes(hh)], s) for (b, hh), s in zip(probs, s_old)]
    for (b, hh), s, vn, os_ in zip(probs, s_old, v_new, o_state):
        o = os_ + _mm_state(qk_ref[b, :, hh * CHUNK:(hh + 1) * CHUNK], vn)
        egl = jnp.exp(gl_ref[b, SUBLANES - 1:SUBLANES, DN_HEADS + hh:DN_HEADS + hh + 1])
        s_new = s * egl + _mm_tn_state(kd_ref[b, :, lanes(hh)], vn)
        s_scr[b, hh] = s_new
        s_out_ref[b, hh] = s_new
        yb_ref[b, :, lanes(hh)] = _rms(o, g_out) * _silu(z_ref[b, :, lanes(hh)])


DELTA_PREP_CHUNKS = 4
DELTA_SCAN_SEQS = 8


def _delta(q, k, v, z, bg, s0, g_out):
    b, l, _ = q.shape
    nc = l // CHUNK
    ncb = min(DELTA_PREP_CHUNKS, nc)
    sb = min(b, DELTA_PREP_CHUNKS // ncb)
    rows = ncb * CHUNK
    tok = lambda width: pl.BlockSpec((sb, rows, width), lambda i, j: (i, j, 0))
    glspec = pl.BlockSpec((sb, ncb * SUBLANES, LANES), lambda i, j: (i, j, 0))
    wide = lambda width: jax.ShapeDtypeStruct((b, l, width), F32)
    uv, wk, qd, kd, qk, gl = pl.pallas_call(
        _delta_prep_kernel,
        out_shape=(wide(DN_VW), wide(DN_QK), wide(DN_QK), wide(DN_QK), wide(DN_HEADS * CHUNK),
                   jax.ShapeDtypeStruct((b, nc * SUBLANES, LANES), F32)),
        grid=(b // sb, nc // ncb),
        in_specs=[tok(DN_QK), tok(DN_QK), tok(DN_VW), tok(LANES)],
        out_specs=(tok(DN_VW), tok(DN_QK), tok(DN_QK), tok(DN_QK), tok(DN_HEADS * CHUNK), glspec),
        compiler_params=pltpu.CompilerParams(dimension_semantics=("arbitrary", "arbitrary"), vmem_limit_bytes=VMEM_LIMIT),
        name="delta_prep",
    )(q, k, v, bg)

    bb = min(DELTA_SCAN_SEQS, b)
    ctok = lambda width: pl.BlockSpec((bb, CHUNK, width), lambda i, j: (i, j, 0))
    st = pl.BlockSpec((bb, DN_HEADS, DN_DK, DN_DV), lambda i, j: (i, 0, 0, 0))
    return pl.pallas_call(
        _delta_scan_kernel,
        out_shape=(wide(DN_VW), jax.ShapeDtypeStruct((b, DN_HEADS, DN_DK, DN_DV), F32)),
        grid=(b // bb, nc),
        in_specs=[ctok(DN_VW), ctok(DN_QK), ctok(DN_QK), ctok(DN_QK), ctok(DN_HEADS * CHUNK),
                  pl.BlockSpec((bb, SUBLANES, LANES), lambda i, j: (i, j, 0)), ctok(DN_VW), st,
                  pl.BlockSpec((1, DN_DV), lambda i, j: (0, 0))],
        out_specs=(ctok(DN_VW), st),
        scratch_shapes=[pltpu.VMEM((bb, DN_HEADS, DN_DK, DN_DV), F32)],
        compiler_params=pltpu.CompilerParams(dimension_semantics=("arbitrary", "arbitrary"), vmem_limit_bytes=VMEM_LIMIT),
        name="delta_scan",
    )(uv, wk, qd, kd, qk, gl, z, s0, g_out)


def _merge_kernel(x_ref, ya_ref, yb_ref, graw_ref, w_up_pool_ref, w_up_dn_ref, w_out_ref, g_ffn_ref, x2_ref, xn_ref):
    graw = graw_ref[...]
    ga = _sigmoid(graw[:, :D_MODEL])
    gb = _sigmoid(graw[:, D_MODEL:])
    merged = ga * _mm(ya_ref[...], w_up_pool_ref[...]) + gb * _mm(yb_ref[...], w_up_dn_ref[...])
    x2 = x_ref[...] + _mm(merged, w_out_ref[...])
    x2_ref[...] = x2
    xn_ref[...] = _rms(x2, g_ffn_ref[...])


def _merge(x, ya, yb, graw, w_up_pool, w_up_dn, w_out, g_ffn, tm):
    t = x.shape[0]
    tok = lambda width: pl.BlockSpec((tm, width), lambda i: (i, 0))
    full = lambda a: pl.BlockSpec(a.shape, lambda i: (0,) * a.ndim)
    weights = (w_up_pool, w_up_dn, w_out, g_ffn)
    return pl.pallas_call(
        _merge_kernel,
        out_shape=(jax.ShapeDtypeStruct((t, D_MODEL), F32), jax.ShapeDtypeStruct((t, D_MODEL), F32)),
        grid=(t // tm,),
        in_specs=[tok(D_MODEL), tok(D_POOL), tok(DN_VW), tok(2 * D_MODEL)] + [full(a) for a in weights],
        out_specs=(tok(D_MODEL), tok(D_MODEL)),
        compiler_params=pltpu.CompilerParams(dimension_semantics=("arbitrary",), vmem_limit_bytes=VMEM_LIMIT),
        name="branch_merge",
    )(x, ya, yb, graw, *weights)


def _top16_rows(s, ids=None):
    if ids is None:
        ids = lax.broadcasted_iota(jnp.int32, s.shape, 0)
    ids = ids.astype(F32)
    vals, idxs = [], []
    for _ in range(PEER_TOPK):
        m = jnp.max(s, axis=0, keepdims=True)
        idx = jnp.min(jnp.where(s == m, ids, jnp.inf), axis=0, keepdims=True)
        vals.append(m)
        idxs.append(idx)
        s = jnp.where(ids == idx, -jnp.inf, s)
    return jnp.concatenate(vals, axis=0), jnp.concatenate(idxs, axis=0).astype(jnp.int32)


def _pair_candidates(v1, v2):
    tokens = v1.shape[1]
    sub = lax.broadcasted_iota(jnp.int32, (SUBLANES, tokens), 0)
    vals = [v1[0:1] + v2[0:SUBLANES], v1[0:1] + v2[SUBLANES:]]
    ids = [sub, sub + SUBLANES]
    for a in range(1, SUBLANES):
        vals.append(v1[a:a + 1] + v2[0:SUBLANES])
        ids.append(sub + a * PEER_TOPK)
    vals.append(v1[SUBLANES:] + v2[0:1])
    ids.append((sub + SUBLANES) * PEER_TOPK)
    return jnp.concatenate(vals, axis=0), jnp.concatenate(ids, axis=0)


def _take_rows(table, idx):
    out = jnp.zeros_like(table)
    for a in range(PEER_TOPK):
        out = jnp.where(idx == a, table[a:a + 1, :], out)
    return out


def _route_kernel(xn_ref, wq_ref, keys_ref, experts_ref, gates_ref):
    half = PEER_DQ // 2
    q = jnp.dot(xn_ref[...].astype(BF16), wq_ref[...], preferred_element_type=F32).astype(BF16)
    experts, gates = [], []
    for hh in range(PEER_HEADS):
        q1 = q[:, hh * PEER_DQ:hh * PEER_DQ + half]
        q2 = q[:, hh * PEER_DQ + half:(hh + 1) * PEER_DQ]
        nt = (((1,), (1,)), ((), ()))
        s1 = lax.dot_general(keys_ref[0, hh], q1, nt, preferred_element_type=F32)
        s2 = lax.dot_general(keys_ref[1, hh], q2, nt, preferred_element_type=F32)
        v1, i1 = _top16_rows(s1)
        v2, i2 = _top16_rows(s2)
        cv, ci = _top16_rows(*_pair_candidates(v1, v2))
        e1 = _take_rows(i1, ci // PEER_TOPK)
        e2 = _take_rows(i2, ci % PEER_TOPK)
        experts.append(e1 * PEER_KEYS + e2)
        ex = jnp.exp(cv - cv[0:1, :])
        gates.append(ex / jnp.sum(ex, axis=0, keepdims=True))
    experts_ref[...] = jnp.concatenate(experts, axis=0).T
    gates_ref[...] = jnp.concatenate(gates, axis=0).T


def _route(xn, wq, keys, tm):
    t = xn.shape[0]
    return pl.pallas_call(
        _route_kernel,
        out_shape=(jax.ShapeDtypeStruct((t, PEER_PICKS), jnp.int32), jax.ShapeDtypeStruct((t, PEER_PICKS), F32)),
        grid=(t // tm,),
        in_specs=[pl.BlockSpec((tm, D_MODEL), lambda i: (i, 0)),
                  pl.BlockSpec(wq.shape, lambda i: (0, 0)),
                  pl.BlockSpec(keys.shape, lambda i: (0, 0, 0, 0))],
        out_specs=(pl.BlockSpec((tm, PEER_PICKS), lambda i: (i, 0)), pl.BlockSpec((tm, PEER_PICKS), lambda i: (i, 0))),
        compiler_params=pltpu.CompilerParams(dimension_semantics=("arbitrary",), vmem_limit_bytes=VMEM_LIMIT),
        name="peer_route",
    )(xn, wq, keys)


EXPERT_TOKENS = 16


def _experts_kernel(idx_ref, idx_next_ref, gates_ref, xn_ref, x2_ref, g_final_ref, tab_ref, out_ref, buf_a, buf_b, cb, sem):
    bufs = (buf_a, buf_b)
    i = pl.program_id(0)
    n = pl.num_programs(0)
    tb = EXPERT_TOKENS
    lane = lax.broadcasted_iota(jnp.int32, (PEER_HEADS, PEER_TOPK), 1)
    g_final = g_final_ref[...]

    def issue_token(ref, t, sl):
        for h in range(PEER_HEADS):
            for k in range(PEER_TOPK):
                e = ref[0, t, h * PEER_TOPK + k]
                pltpu.async_copy(tab_ref.at[e], bufs[sl].at[t, k, h], sem.at[sl, t], priority=k % 2)

    def wait_token(t, sl):
        pltpu.make_async_copy(bufs[sl].at[t], bufs[sl].at[t], sem.at[sl, t]).wait()

    def compute_token(t, sl):
        x = xn_ref[t]
        sub = lax.broadcasted_iota(jnp.int32, (PEER_HEADS, PEER_TOPK), 0)
        act = jnp.zeros((PEER_HEADS, PEER_TOPK), F32)
        for h in range(PEER_HEADS):
            part = jnp.zeros((SUBLANES, PEER_TOPK), F32)
            for k in range(PEER_TOPK):
                part = jnp.where(lane == k, jnp.sum(bufs[sl][t, k, h, :SUBLANES, :] * x, axis=-1, keepdims=True), part)
            act = jnp.where(sub == h, jnp.sum(part, axis=0, keepdims=True), act)
        coef = gates_ref[t] * (0.5 * act * (1.0 + lax.erf(act * (0.5 ** 0.5))))
        for k in range(PEER_TOPK):
            cb[t, k] = jnp.broadcast_to(coef[:, k:k + 1], (PEER_HEADS, LANES))
        accs = [jnp.zeros((SUBLANES, LANES), F32) for _ in range(4)]
        for k in range(PEER_TOPK):
            for h in range(PEER_HEADS):
                c = jnp.broadcast_to(cb[t, k, h:h + 1, :], (SUBLANES, LANES))
                accs[h % 4] = accs[h % 4] + c * bufs[sl][t, k, h, SUBLANES:, :]
        z = x2_ref[t] + ((accs[0] + accs[1]) + (accs[2] + accs[3]))
        ms = jnp.sum(jnp.sum(z * z, axis=-1, keepdims=True), axis=0, keepdims=True) * (1.0 / D_MODEL)
        out_ref[t] = z * lax.rsqrt(ms + EPS) * g_final

    @pl.when(i == 0)
    def _():
        for t in range(tb):
            issue_token(idx_ref, t, 0)

    def step(sl):
        for t in range(tb):
            issue_token(idx_next_ref, t, 1 - sl)
            wait_token(t, sl)
            compute_token(t, sl)

        @pl.when(i == n - 1)
        def _():
            for t in range(tb):
                wait_token(t, 1 - sl)

    @pl.when(i % 2 == 0)
    def _():
        step(0)

    @pl.when(i % 2 == 1)
    def _():
        step(1)


def _experts(experts, gates, xn, x2, g_final, table):
    t = xn.shape[0]
    tb = EXPERT_TOKENS
    nb = t // tb
    idx = experts.reshape(nb, tb, PEER_PICKS)
    gates = gates.reshape(t, PEER_HEADS, PEER_TOPK)
    tiles = lambda a: a.reshape(a.shape[0], SUBLANES, LANES)
    tok = pl.BlockSpec((tb, SUBLANES, LANES), lambda i: (i, 0, 0))
    out = pl.pallas_call(
        _experts_kernel,
        out_shape=jax.ShapeDtypeStruct((t, SUBLANES, LANES), F32),
        grid=(nb,),
        in_specs=[pl.BlockSpec((1, tb, PEER_PICKS), lambda i: (i, 0, 0), memory_space=pltpu.SMEM),
                  pl.BlockSpec((1, tb, PEER_PICKS), lambda i: (jnp.minimum(i + 1, nb - 1), 0, 0), memory_space=pltpu.SMEM),
                  pl.BlockSpec((tb, PEER_HEADS, PEER_TOPK), lambda i: (i, 0, 0)), tok, tok,
                  pl.BlockSpec((SUBLANES, LANES), lambda i: (0, 0)),
                  pl.BlockSpec(memory_space=pl.ANY)],
        out_specs=tok,
        scratch_shapes=[pltpu.VMEM((tb, PEER_TOPK, PEER_HEADS, 2 * SUBLANES, LANES), F32),
                        pltpu.VMEM((tb, PEER_TOPK, PEER_HEADS, 2 * SUBLANES, LANES), F32),
                        pltpu.VMEM((tb, PEER_TOPK, PEER_HEADS, LANES), F32),
                        pltpu.SemaphoreType.DMA((2, tb))],
        compiler_params=pltpu.CompilerParams(dimension_semantics=("arbitrary",), vmem_limit_bytes=VMEM_LIMIT),
        name="peer_experts",
    )(idx, idx, gates, tiles(xn), tiles(x2), g_final.reshape(SUBLANES, LANES), table)
    return out.reshape(t, D_MODEL)


def _pad_rows_front(a, rows):
    return jnp.pad(a, ((0, 0), (rows - a.shape[1], 0), (0, 0)))


MERGE_ROWS = 512
ROUTE_ROWS = 256
FRONT_ROWS = 512


def _group(x, pool_prev, conv_prev, s_prev, pos0, p):
    b, l, _ = x.shape
    tl = min(FRONT_ROWS, l)
    sb = min(b, FRONT_ROWS // tl)
    tm_merge = min(MERGE_ROWS, b * l)
    tm_route = min(ROUTE_ROWS, b * l)
    ya, q, k, v, z, graw, bg, pool_tail, conv_tail = _front(
        x, _pad_rows_front(pool_prev, POOL_PREFIX_ROWS), _pad_rows_front(conv_prev, CONV_PREFIX_ROWS), pos0, sb, tl,
        p["g_mix"], p["w_u"], p["w_qkv"], p["w_z"], p["w_ba"], p["w_g"], p["w_grp"], p["pool_scale"], p["w_conv8"],
        p["a_log_pad"], p["dt_pad"])
    yb, s_new = _delta(q, k, v, z, bg, s_prev, p["g_dn_out"])
    t = b * l
    flat = lambda a: a.reshape(t, a.shape[-1])
    x2, xn = _merge(flat(x), flat(ya), flat(yb), flat(graw), p["w_up_pool"], p["w_up_dn"], p["w_out"], p["g_ffn"], tm_merge)
    experts, gates = _route(xn, p["w_peer_q"], p["peer_keys"], tm_route)
    y = _experts(experts, gates, xn, x2, p["g_final"], p["peer_table"])
    return (y.reshape(b, l, D_MODEL), pool_tail[:, POOL_PREFIX_ROWS - POOL_STATE:],
            conv_tail[:, CONV_PREFIX_ROWS - (CONV_W - 1):], s_new)


def kernel(x_prompt, x_sample, cache_pool, state_dn_conv, state_dn, g_mix, w_in, w_pool_grp, pool_scale, w_conv, a_log,
           dt_bias, g_dn_out, w_up_pool, w_up_dn, w_out, g_ffn, w_peer_q, peer_sub_keys, peer_u, peer_v, g_final):
    depth = w_in.shape[0]
    assert depth == 1
    bp = x_prompt.shape[0]
    lane_pad = lambda a, off: jnp.pad(a.astype(F32)[None, :], ((0, 0), (off, LANES - off - a.shape[0])))
    w = w_in[0]
    params = {
        "g_mix": g_mix[0][None, :],
        "w_u": w[:, OFF_U:OFF_QKV].astype(BF16),
        "w_qkv": w[:, OFF_QKV:OFF_Z].astype(BF16),
        "w_z": w[:, OFF_Z:OFF_B].astype(BF16),
        "w_ba": jnp.pad(w[:, OFF_B:OFF_G], ((0, 0), (0, LANES - 2 * DN_HEADS))).astype(BF16),
        "w_g": w[:, OFF_G:].astype(BF16),
        "w_grp": w_pool_grp[0].astype(BF16),
        "pool_scale": pool_scale[0][None, :],
        "w_conv8": jnp.pad(w_conv[0], ((0, SUBLANES - CONV_W), (0, 0))),
        "a_log_pad": lane_pad(a_log[0], DN_HEADS),
        "dt_pad": lane_pad(dt_bias[0], DN_HEADS),
        "g_dn_out": g_dn_out[0][None, :],
        "w_up_pool": w_up_pool[0].astype(BF16),
        "w_up_dn": w_up_dn[0].astype(BF16),
        "w_out": w_out[0].astype(BF16),
        "g_ffn": g_ffn[0][None, :],
        "w_peer_q": w_peer_q[0].reshape(D_MODEL, PEER_HEADS * PEER_DQ).astype(BF16),
        "peer_keys": peer_sub_keys[0].astype(BF16),
        "peer_table": jnp.concatenate([peer_u[0].reshape(-1, SUBLANES, LANES), peer_v[0].reshape(-1, SUBLANES, LANES)], axis=1),
        "g_final": g_final[None, :],
    }
    zeros = lambda *shape: jnp.zeros(shape, F32)
    yp, pool_p, conv_p, dn_p = _group(
        x_prompt, zeros(bp, POOL_STATE, D_POOL), zeros(bp, CONV_W - 1, CONV_CH), zeros(bp, DN_HEADS, DN_DK, DN_DV),
        0, params)
    ys, pool_s, conv_s, dn_s = _group(
        x_sample, cache_pool[0], state_dn_conv[0], state_dn[0].astype(F32),
        PAST_LEN, params)
    return (yp, ys, pool_p[None], conv_p[None], dn_p[None].astype(state_dn.dtype),
            pool_s[None], conv_s[None], dn_s[None].astype(state_dn.dtype))
```

```python
import functools

import jax
import jax.numpy as jnp
from jax import lax
from jax.experimental import pallas as pl
from jax.experimental.pallas import tpu as pltpu

D_MODEL = 1024
CHUNK = 64
D_POOL = 512
POOL_WINDOWS = (2, 4, 8, 16)
POOL_GROUP = 128
POOL_STATE = 15
DN_HEADS = 4
DN_DK = 128
DN_DV = 128
DN_QK = DN_HEADS * DN_DK
DN_VW = DN_HEADS * DN_DV
CONV_W = 4
CONV_CH = 2 * DN_QK + DN_VW
OFF_U = 0
OFF_QKV = OFF_U + D_POOL
OFF_Z = OFF_QKV + CONV_CH
OFF_B = OFF_Z + DN_VW
OFF_A = OFF_B + DN_HEADS
OFF_G = OFF_A + DN_HEADS
PEER_HEADS = 8
PEER_KEYS = 128
PEER_DQ = 256
PEER_TOPK = 16
PEER_PICKS = PEER_HEADS * PEER_TOPK
EPS = 1e-6
PAST_LEN = 4096

LANES = 128
SUBLANES = 8
POOL_PREFIX_ROWS = 16
CONV_PREFIX_ROWS = 8
VMEM_LIMIT = 56 * 1024 * 1024

F32 = jnp.float32
BF16 = jnp.bfloat16
HIGHEST = lax.Precision.HIGHEST


def _mm(a, b):
    return jnp.dot(a.astype(BF16), b.astype(BF16), preferred_element_type=F32)


def _mm_f32(a, b):
    return jnp.dot(a, b, precision=HIGHEST, preferred_element_type=F32)


def _mm_nt_f32(a, b):
    return lax.dot_general(a, b, (((1,), (1,)), ((), ())), precision=HIGHEST, preferred_element_type=F32)


def _mm_tn_f32(a, b):
    return lax.dot_general(a, b, (((0,), (0,)), ((), ())), precision=HIGHEST, preferred_element_type=F32)


def _sigmoid(x):
    return 1.0 / (1.0 + jnp.exp(-x))


def _silu(x):
    return x * _sigmoid(x)


def _softplus(x):
    return jnp.maximum(x, 0.0) + jnp.log1p(jnp.exp(-jnp.abs(x)))


def _rms(x, g):
    return x * lax.rsqrt(jnp.mean(x * x, axis=-1, keepdims=True) + EPS) * g


def _front_kernel(pos0, x_ref, pool_pre_ref, conv_pre_ref, g_mix_ref, w_u_ref, w_qkv_ref, w_z_ref, w_ba_ref, w_g_ref,
                  w_grp_ref, pool_scale_ref, w_conv_ref, a_log_ref, dt_ref,
                  ya_ref, q_ref, k_ref, v_ref, z_ref, graw_ref, bg_ref, pool_tail_ref, conv_tail_ref,
                  carry_u, carry_c):
    sb, tl, _ = x_ref.shape
    li = pl.program_id(1)

    @pl.when(li == 0)
    def _():
        carry_u[...] = pool_pre_ref[...]
        carry_c[...] = conv_pre_ref[...]

    x = x_ref[...].reshape(sb * tl, D_MODEL)
    h = _rms(x, g_mix_ref[...]).astype(BF16)
    u = jnp.dot(h, w_u_ref[...], preferred_element_type=F32)
    qkv = jnp.dot(h, w_qkv_ref[...], preferred_element_type=F32)
    z_ref[...] = jnp.dot(h, w_z_ref[...], preferred_element_type=F32).reshape(sb, tl, DN_VW)
    graw_ref[...] = jnp.dot(h, w_g_ref[...], preferred_element_type=F32).reshape(sb, tl, 2 * D_MODEL)
    ba = jnp.dot(h, w_ba_ref[...], preferred_element_type=F32)
    lane = lax.broadcasted_iota(jnp.int32, ba.shape, 1)
    beta = _sigmoid(ba)
    g = -jnp.exp(a_log_ref[...]) * _softplus(ba + dt_ref[...])
    bg_ref[...] = jnp.where(lane < DN_HEADS, beta, g).reshape(sb, tl, LANES)

    row = lax.broadcasted_iota(jnp.int32, (tl, POOL_GROUP), 0)
    pos1 = pos0 + li * tl + row + 1
    w_conv = w_conv_ref[...]

    for s in range(sb):
        u_s = u[s * tl:(s + 1) * tl]
        ext = jnp.concatenate([carry_u[s], u_s], axis=0)
        mixed = []
        for gi, w in enumerate(POOL_WINDOWS):
            acc = ext[:, gi * POOL_GROUP:(gi + 1) * POOL_GROUP]
            span = 1
            while span < w:
                acc = acc + pltpu.roll(acc, span, axis=0)
                span *= 2
            win = acc[POOL_PREFIX_ROWS:]
            cnt = jnp.minimum(pos1, w).astype(F32)
            pooled = win / cnt - u_s[:, gi * POOL_GROUP:(gi + 1) * POOL_GROUP]
            mixed.append(_mm(pooled, w_grp_ref[gi]))
        ya_ref[s] = jnp.concatenate(mixed, axis=1) * pool_scale_ref[...]
        pool_tail_ref[s] = ext[tl:]
        carry_u[s] = ext[tl:]

        c_s = qkv[s * tl:(s + 1) * tl]
        cext = jnp.concatenate([carry_c[s], c_s], axis=0)
        y = c_s * w_conv[CONV_W - 1:CONV_W]
        for j in range(1, CONV_W):
            y = y + pltpu.roll(cext, j, axis=0)[CONV_PREFIX_ROWS:] * w_conv[CONV_W - 1 - j:CONV_W - j]
        y = _silu(y)
        conv_tail_ref[s] = cext[tl:]
        carry_c[s] = cext[tl:]
        for hh in range(DN_HEADS):
            sl = slice(hh * DN_DK, (hh + 1) * DN_DK)
            qh = y[:, sl]
            q_ref[s, :, sl] = qh * lax.rsqrt(jnp.sum(qh * qh, axis=-1, keepdims=True) + EPS)
            kh = y[:, DN_QK + hh * DN_DK:DN_QK + (hh + 1) * DN_DK]
            k_ref[s, :, sl] = kh * lax.rsqrt(jnp.sum(kh * kh, axis=-1, keepdims=True) + EPS)
        v_ref[s] = y[:, 2 * DN_QK:]


def _front(x, pool_pre, conv_pre, pos0, sb, tl, g_mix, w_u, w_qkv, w_z, w_ba, w_g, w_grp, pool_scale, w_conv8,
           a_log_pad, dt_pad):
    b, l, _ = x.shape
    grid = (b // sb, l // tl)
    tok = lambda width: pl.BlockSpec((sb, tl, width), lambda i, j: (i, j, 0))
    seq = lambda rows, width: pl.BlockSpec((sb, rows, width), lambda i, j: (i, 0, 0))
    full = lambda a: pl.BlockSpec(a.shape, lambda i, j: (0,) * a.ndim)
    weights = (g_mix, w_u, w_qkv, w_z, w_ba, w_g, w_grp, pool_scale, w_conv8, a_log_pad, dt_pad)
    out_shape = (
        jax.ShapeDtypeStruct((b, l, D_POOL), F32),
        jax.ShapeDtypeStruct((b, l, DN_QK), F32),
        jax.ShapeDtypeStruct((b, l, DN_QK), F32),
        jax.ShapeDtypeStruct((b, l, DN_VW), F32),
        jax.ShapeDtypeStruct((b, l, DN_VW), F32),
        jax.ShapeDtypeStruct((b, l, 2 * D_MODEL), F32),
        jax.ShapeDtypeStruct((b, l, LANES), F32),
        jax.ShapeDtypeStruct((b, POOL_PREFIX_ROWS, D_POOL), F32),
        jax.ShapeDtypeStruct((b, CONV_PREFIX_ROWS, CONV_CH), F32),
    )
    return pl.pallas_call(
        functools.partial(_front_kernel, pos0),
        out_shape=out_shape,
        grid=grid,
        in_specs=[tok(D_MODEL), seq(POOL_PREFIX_ROWS, D_POOL), seq(CONV_PREFIX_ROWS, CONV_CH)] + [full(a) for a in weights],
        out_specs=(tok(D_POOL), tok(DN_QK), tok(DN_QK), tok(DN_VW), tok(DN_VW), tok(2 * D_MODEL), tok(LANES),
                   seq(POOL_PREFIX_ROWS, D_POOL), seq(CONV_PREFIX_ROWS, CONV_CH)),
        scratch_shapes=[pltpu.VMEM((sb, POOL_PREFIX_ROWS, D_POOL), F32), pltpu.VMEM((sb, CONV_PREFIX_ROWS, CONV_CH), F32)],
        compiler_params=pltpu.CompilerParams(dimension_semantics=("arbitrary", "arbitrary"), vmem_limit_bytes=VMEM_LIMIT),
        name="mixer_front",
    )(x, pool_pre, conv_pre, *weights)


_NN = (((1,), (0,)), ((), ()))
_NT = (((1,), (1,)), ((), ()))
_TN = (((0,), (0,)), ((), ()))


def _dot_bf16(a, b, dims):
    return lax.dot_general(a.astype(BF16), b.astype(BF16), dims, preferred_element_type=F32)


def _mm_nt(a, b):
    return _dot_bf16(a, b, _NT)


def _mm_chain(a, b):
    return _dot_bf16(a, b, _NN)


def _mm_state(a, b):
    return _dot_bf16(a, b, _NN)


def _mm_tn_state(a, b):
    return _dot_bf16(a, b, _TN)


def _delta_prep_kernel(q_ref, k_ref, v_ref, bg_ref, uv_ref, wk_ref, qd_ref, kd_ref, qk_ref, gl_ref):
    sb = q_ref.shape[0]
    ncb = q_ref.shape[1] // CHUNK
    row = lax.broadcasted_iota(jnp.int32, (CHUNK, CHUNK), 0)
    col = lax.broadcasted_iota(jnp.int32, (CHUNK, CHUNK), 1)
    incl = row >= col
    strict = row > col
    tri = incl.astype(F32)
    xs, pw = [], []
    for s in range(sb):
        for c in range(ncb):
            rows = slice(c * CHUNK, (c + 1) * CHUNK)
            bg = bg_ref[s, rows, :]
            gc_all = _mm_f32(tri, bg)
            gc_rows = gc_all.T
            gl_ref[s, c * SUBLANES:(c + 1) * SUBLANES, :] = gc_all[CHUNK - SUBLANES:, :]
            for hh in range(DN_HEADS):
                sl = slice(hh * DN_DK, (hh + 1) * DN_DK)
                q = q_ref[s, rows, sl] * (DN_DK ** -0.5)
                k = k_ref[s, rows, sl]
                beta = bg[:, hh:hh + 1]
                gcol = gc_all[:, DN_HEADS + hh:DN_HEADS + hh + 1]
                grow = gc_rows[DN_HEADS + hh:DN_HEADS + hh + 1, :]
                decay = jnp.exp(jnp.where(incl, gcol - grow, -jnp.inf))
                egc = jnp.exp(gcol)
                kb = k * beta
                pw.append(-jnp.where(strict, _mm_nt(kb, k) * decay, 0.0))
                xs.append(jnp.concatenate([v_ref[s, rows, sl] * beta, kb * egc], axis=1))
                qk_ref[s, rows, hh * CHUNK:(hh + 1) * CHUNK] = _mm_nt(q, k) * decay
                gl = gcol[CHUNK - 1:CHUNK, :]
                qd_ref[s, rows, sl] = q * egc
                kd_ref[s, rows, sl] = k * jnp.exp(gl - gcol)
    span = 1
    while True:
        xs = [x + _mm_chain(p, x) for p, x in zip(pw, xs)]
        span *= 2
        if span >= CHUNK:
            break
        pw = [_mm_chain(p, p) for p in pw]
    i = 0
    for s in range(sb):
        for c in range(ncb):
            rows = slice(c * CHUNK, (c + 1) * CHUNK)
            for hh in range(DN_HEADS):
                sl = slice(hh * DN_DK, (hh + 1) * DN_DK)
                uv_ref[s, rows, sl] = xs[i][:, :DN_DV]
                wk_ref[s, rows, sl] = xs[i][:, DN_DV:]
                i += 1


def _delta_scan_kernel(uv_ref, wk_ref, qd_ref, kd_ref, qk_ref, gl_ref, z_ref, s0_ref, g_out_ref, yb_ref, s_out_ref, s_scr):
    ci = pl.program_id(1)
    bb = uv_ref.shape[0]

    @pl.when(ci == 0)
    def _():
        s_scr[...] = s0_ref[...]

    g_out = g_out_ref[...]
    probs = [(b, hh) for b in range(bb) for hh in range(DN_HEADS)]
    lanes = lambda hh: slice(hh * DN_DK, (hh + 1) * DN_DK)
    s_old = [s_scr[b, hh] for b, hh in probs]
    v_new = [uv_ref[b, :, lanes(hh)] - _mm_state(wk_ref[b, :, lanes(hh)], s) for (b, hh), s in zip(probs, s_old)]
    o_state = [_mm_state(qd_ref[b, :, lanes(hh)], s) for (b, hh), s in zip(probs, s_old)]
    for (b, hh), s, vn, os_ in zip(probs, s_old, v_new, o_state):
        o = os_ + _mm_state(qk_ref[b, :, hh * CHUNK:(hh + 1) * CHUNK], vn)
        egl = jnp.exp(gl_ref[b, SUBLANES - 1:SUBLANES, DN_HEADS + hh:DN_HEADS + hh + 1])
        s_new = s * egl + _mm_tn_state(kd_ref[b, :, lanes(hh)], vn)
        s_scr[b, hh] = s_new
        s_out_ref[b, hh] = s_new
        yb_ref[b, :, lanes(hh)] = _rms(o, g_out) * _silu(z_ref[b, :, lanes(hh)])


DELTA_PREP_CHUNKS = 4
DELTA_SCAN_SEQS = 8


def _delta(q, k, v, z, bg, s0, g_out):
    b, l, _ = q.shape
    nc = l // CHUNK
    ncb = min(DELTA_PREP_CHUNKS, nc)
    sb = min(b, DELTA_PREP_CHUNKS // ncb)
    rows = ncb * CHUNK
    tok = lambda width: pl.BlockSpec((sb, rows, width), lambda i, j: (i, j, 0))
    glspec = pl.BlockSpec((sb, ncb * SUBLANES, LANES), lambda i, j: (i, j, 0))
    wide = lambda width: jax.ShapeDtypeStruct((b, l, width), F32)
    uv, wk, qd, kd, qk, gl = pl.pallas_call(
        _delta_prep_kernel,
        out_shape=(wide(DN_VW), wide(DN_QK), wide(DN_QK), wide(DN_QK), wide(DN_HEADS * CHUNK),
                   jax.ShapeDtypeStruct((b, nc * SUBLANES, LANES), F32)),
        grid=(b // sb, nc // ncb),
        in_specs=[tok(DN_QK), tok(DN_QK), tok(DN_VW), tok(LANES)],
        out_specs=(tok(DN_VW), tok(DN_QK), tok(DN_QK), tok(DN_QK), tok(DN_HEADS * CHUNK), glspec),
        compiler_params=pltpu.CompilerParams(dimension_semantics=("arbitrary", "arbitrary"), vmem_limit_bytes=VMEM_LIMIT),
        name="delta_prep",
    )(q, k, v, bg)

    bb = min(DELTA_SCAN_SEQS, b)
    ctok = lambda width: pl.BlockSpec((bb, CHUNK, width), lambda i, j: (i, j, 0))
    st = pl.BlockSpec((bb, DN_HEADS, DN_DK, DN_DV), lambda i, j: (i, 0, 0, 0))
    return pl.pallas_call(
        _delta_scan_kernel,
        out_shape=(wide(DN_VW), jax.ShapeDtypeStruct((b, DN_HEADS, DN_DK, DN_DV), F32)),
        grid=(b // bb, nc),
        in_specs=[ctok(DN_VW), ctok(DN_QK), ctok(DN_QK), ctok(DN_QK), ctok(DN_HEADS * CHUNK),
                  pl.BlockSpec((bb, SUBLANES, LANES), lambda i, j: (i, j, 0)), ctok(DN_VW), st,
                  pl.BlockSpec((1, DN_DV), lambda i, j: (0, 0))],
        out_specs=(ctok(DN_VW), st),
        scratch_shapes=[pltpu.VMEM((bb, DN_HEADS, DN_DK, DN_DV), F32)],
        compiler_params=pltpu.CompilerParams(dimension_semantics=("arbitrary", "arbitrary"), vmem_limit_bytes=VMEM_LIMIT),
        name="delta_scan",
    )(uv, wk, qd, kd, qk, gl, z, s0, g_out)


def _merge_kernel(x_ref, ya_ref, yb_ref, graw_ref, w_up_pool_ref, w_up_dn_ref, w_out_ref, g_ffn_ref, x2_ref, xn_ref):
    graw = graw_ref[...]
    ga = _sigmoid(graw[:, :D_MODEL])
    gb = _sigmoid(graw[:, D_MODEL:])
    merged = ga * _mm(ya_ref[...], w_up_pool_ref[...]) + gb * _mm(yb_ref[...], w_up_dn_ref[...])
    x2 = x_ref[...] + _mm(merged, w_out_ref[...])
    x2_ref[...] = x2
    xn_ref[...] = _rms(x2, g_ffn_ref[...])


def _merge(x, ya, yb, graw, w_up_pool, w_up_dn, w_out, g_ffn, tm):
    t = x.shape[0]
    tok = lambda width: pl.BlockSpec((tm, width), lambda i: (i, 0))
    full = lambda a: pl.BlockSpec(a.shape, lambda i: (0,) * a.ndim)
    weights = (w_up_pool, w_up_dn, w_out, g_ffn)
    return pl.pallas_call(
        _merge_kernel,
        out_shape=(jax.ShapeDtypeStruct((t, D_MODEL), F32), jax.ShapeDtypeStruct((t, D_MODEL), F32)),
        grid=(t // tm,),
        in_specs=[tok(D_MODEL), tok(D_POOL), tok(DN_VW), tok(2 * D_MODEL)] + [full(a) for a in weights],
        out_specs=(tok(D_MODEL), tok(D_MODEL)),
        compiler_params=pltpu.CompilerParams(dimension_semantics=("arbitrary",), vmem_limit_bytes=VMEM_LIMIT),
        name="branch_merge",
    )(x, ya, yb, graw, *weights)


def _top16_rows(s, ids=None):
    if ids is None:
        ids = lax.broadcasted_iota(jnp.int32, s.shape, 0)
    ids = ids.astype(F32)
    vals, idxs = [], []
    for _ in range(PEER_TOPK):
        m = jnp.max(s, axis=0, keepdims=True)
        idx = jnp.min(jnp.where(s == m, ids, jnp.inf), axis=0, keepdims=True)
        vals.append(m)
        idxs.append(idx)
        s = jnp.where(ids == idx, -jnp.inf, s)
    return jnp.concatenate(vals, axis=0), jnp.concatenate(idxs, axis=0).astype(jnp.int32)


def _pair_candidates(v1, v2):
    tokens = v1.shape[1]
    sub = lax.broadcasted_iota(jnp.int32, (SUBLANES, tokens), 0)
    vals = [v1[0:1] + v2[0:SUBLANES], v1[0:1] + v2[SUBLANES:]]
    ids = [sub, sub + SUBLANES]
    for a in range(1, SUBLANES):
        vals.append(v1[a:a + 1] + v2[0:SUBLANES])
        ids.append(sub + a * PEER_TOPK)
    vals.append(v1[SUBLANES:] + v2[0:1])
    ids.append((sub + SUBLANES) * PEER_TOPK)
    return jnp.concatenate(vals, axis=0), jnp.concatenate(ids, axis=0)


def _take_rows(table, idx):
    out = jnp.zeros_like(table)
    for a in range(PEER_TOPK):
        out = jnp.where(idx == a, table[a:a + 1, :], out)
    return out


def _route_kernel(xn_ref, wq_ref, keys_ref, experts_ref, gates_ref):
    half = PEER_DQ // 2
    q = jnp.dot(xn_ref[...].astype(BF16), wq_ref[...], preferred_element_type=F32).astype(BF16)
    experts, gates = [], []
    for hh in range(PEER_HEADS):
        q1 = q[:, hh * PEER_DQ:hh * PEER_DQ + half]
        q2 = q[:, hh * PEER_DQ + half:(hh + 1) * PEER_DQ]
        nt = (((1,), (1,)), ((), ()))
        s1 = lax.dot_general(keys_ref[0, hh], q1, nt, preferred_element_type=F32)
        s2 = lax.dot_general(keys_ref[1, hh], q2, nt, preferred_element_type=F32)
        v1, i1 = _top16_rows(s1)
        v2, i2 = _top16_rows(s2)
        cv, ci = _top16_rows(*_pair_candidates(v1, v2))
        e1 = _take_rows(i1, ci // PEER_TOPK)
        e2 = _take_rows(i2, ci % PEER_TOPK)
        experts.append(e1 * PEER_KEYS + e2)
        ex = jnp.exp(cv - cv[0:1, :])
        gates.append(ex / jnp.sum(ex, axis=0, keepdims=True))
    experts_ref[...] = jnp.concatenate(experts, axis=0).T
    gates_ref[...] = jnp.concatenate(gates, axis=0).T


def _route(xn, wq, keys, tm):
    t = xn.shape[0]
    return pl.pallas_call(
        _route_kernel,
        out_shape=(jax.ShapeDtypeStruct((t, PEER_PICKS), jnp.int32), jax.ShapeDtypeStruct((t, PEER_PICKS), F32)),
        grid=(t // tm,),
        in_specs=[pl.BlockSpec((tm, D_MODEL), lambda i: (i, 0)),
                  pl.BlockSpec(wq.shape, lambda i: (0, 0)),
                  pl.BlockSpec(keys.shape, lambda i: (0, 0, 0, 0))],
        out_specs=(pl.BlockSpec((tm, PEER_PICKS), lambda i: (i, 0)), pl.BlockSpec((tm, PEER_PICKS), lambda i: (i, 0))),
        compiler_params=pltpu.CompilerParams(dimension_semantics=("arbitrary",), vmem_limit_bytes=VMEM_LIMIT),
        name="peer_route",
    )(xn, wq, keys)


EXPERT_TOKENS = 16
EXPERT_ISSUE_TOKENS = 14


def _experts_kernel(idx_ref, idx_next_ref, gates_ref, xn_ref, x2_ref, g_final_ref, tab_ref, out_ref, buf_a, buf_b, cb, sem):
    bufs = (buf_a, buf_b)
    i = pl.program_id(0)
    n = pl.num_programs(0)
    tb = EXPERT_TOKENS
    lane = lax.broadcasted_iota(jnp.int32, (PEER_HEADS, PEER_TOPK), 1)
    g_final = g_final_ref[...]

    def issue_picks(ref, sl, lo, hi):
        for f in range(lo, min(hi, tb * PEER_PICKS)):
            t, h, k = f // PEER_PICKS, (f // PEER_TOPK) % PEER_HEADS, f % PEER_TOPK
            e = ref[0, t, h * PEER_TOPK + k]
            pltpu.async_copy(tab_ref.at[e], bufs[sl].at[t, k, h], sem.at[sl], priority=k % 2)

    def wait_slot(sl):
        for t in range(tb):
            pltpu.make_async_copy(bufs[sl].at[t], bufs[sl].at[t], sem.at[sl]).wait()

    def gate_coefficients(t, sl):
        x = xn_ref[t]
        sub = lax.broadcasted_iota(jnp.int32, (PEER_HEADS, PEER_TOPK), 0)
        act = jnp.zeros((PEER_HEADS, PEER_TOPK), F32)
        for h in range(PEER_HEADS):
            part = jnp.zeros((SUBLANES, PEER_TOPK), F32)
            for k in range(PEER_TOPK):
                part = jnp.where(lane == k, jnp.sum(bufs[sl][t, k, h, :SUBLANES, :] * x, axis=-1, keepdims=True), part)
            act = jnp.where(sub == h, jnp.sum(part, axis=0, keepdims=True), act)
        coef = gates_ref[t] * (0.5 * act * (1.0 + lax.erf(act * (0.5 ** 0.5))))
        for k in range(PEER_TOPK):
            cb[t, k] = jnp.broadcast_to(coef[:, k:k + 1], (PEER_HEADS, LANES))

    def mix_token(t, sl):
        accs = [jnp.zeros((SUBLANES, LANES), F32) for _ in range(4)]
        for k in range(PEER_TOPK):
            for h in range(PEER_HEADS):
                c = jnp.broadcast_to(cb[t, k, h:h + 1, :], (SUBLANES, LANES))
                accs[h % 4] = accs[h % 4] + c * bufs[sl][t, k, h, SUBLANES:, :]
        z = x2_ref[t] + ((accs[0] + accs[1]) + (accs[2] + accs[3]))
        ms = jnp.sum(jnp.sum(z * z, axis=-1, keepdims=True), axis=0, keepdims=True) * (1.0 / D_MODEL)
        out_ref[t] = z * lax.rsqrt(ms + EPS) * g_final

    @pl.when(i == 0)
    def _():
        issue_picks(idx_ref, 0, 0, tb * PEER_PICKS)

    per_token = -(-tb * PEER_PICKS // EXPERT_ISSUE_TOKENS)

    def step(sl):
        wait_slot(sl)
        gate_coefficients(0, sl)
        for t in range(tb):
            @pl.when(i + t >= 0)
            def _():
                issue_picks(idx_next_ref, 1 - sl, t * per_token, (t + 1) * per_token)
                if t + 1 < tb:
                    gate_coefficients(t + 1, sl)
                mix_token(t, sl)

        @pl.when(i == n - 1)
        def _():
            wait_slot(1 - sl)

    @pl.when(i % 2 == 0)
    def _():
        step(0)

    @pl.when(i % 2 == 1)
    def _():
        step(1)


def _experts(experts, gates, xn, x2, g_final, table):
    t = xn.shape[0]
    tb = EXPERT_TOKENS
    nb = t // tb
    idx = experts.reshape(nb, tb, PEER_PICKS)
    gates = gates.reshape(t, PEER_HEADS, PEER_TOPK)
    tiles = lambda a: a.reshape(a.shape[0], SUBLANES, LANES)
    tok = pl.BlockSpec((tb, SUBLANES, LANES), lambda i: (i, 0, 0))
    out = pl.pallas_call(
        _experts_kernel,
        out_shape=jax.ShapeDtypeStruct((t, SUBLANES, LANES), F32),
        grid=(nb,),
        in_specs=[pl.BlockSpec((1, tb, PEER_PICKS), lambda i: (i, 0, 0), memory_space=pltpu.SMEM),
                  pl.BlockSpec((1, tb, PEER_PICKS), lambda i: (jnp.minimum(i + 1, nb - 1), 0, 0), memory_space=pltpu.SMEM),
                  pl.BlockSpec((tb, PEER_HEADS, PEER_TOPK), lambda i: (i, 0, 0)), tok, tok,
                  pl.BlockSpec((SUBLANES, LANES), lambda i: (0, 0)),
                  pl.BlockSpec(memory_space=pl.ANY)],
        out_specs=tok,
        scratch_shapes=[pltpu.VMEM((tb, PEER_TOPK, PEER_HEADS, 2 * SUBLANES, LANES), F32),
                        pltpu.VMEM((tb, PEER_TOPK, PEER_HEADS, 2 * SUBLANES, LANES), F32),
                        pltpu.VMEM((tb, PEER_TOPK, PEER_HEADS, LANES), F32),
                        pltpu.SemaphoreType.DMA((2,))],
        compiler_params=pltpu.CompilerParams(dimension_semantics=("arbitrary",), vmem_limit_bytes=VMEM_LIMIT),
        name="peer_experts",
    )(idx, idx, gates, tiles(xn), tiles(x2), g_final.reshape(SUBLANES, LANES), table)
    return out.reshape(t, D_MODEL)


def _pad_rows_front(a, rows):
    return jnp.pad(a, ((0, 0), (rows - a.shape[1], 0), (0, 0)))


MERGE_ROWS = 512
ROUTE_ROWS = 256
FRONT_ROWS = 512


def _group(x, pool_prev, conv_prev, s_prev, pos0, p):
    b, l, _ = x.shape
    tl = min(FRONT_ROWS, l)
    sb = min(b, FRONT_ROWS // tl)
    tm_merge = min(MERGE_ROWS, b * l)
    tm_route = min(ROUTE_ROWS, b * l)
    ya, q, k, v, z, graw, bg, pool_tail, conv_tail = _front(
        x, _pad_rows_front(pool_prev, POOL_PREFIX_ROWS), _pad_rows_front(conv_prev, CONV_PREFIX_ROWS), pos0, sb, tl,
        p["g_mix"], p["w_u"], p["w_qkv"], p["w_z"], p["w_ba"], p["w_g"], p["w_grp"], p["pool_scale"], p["w_conv8"],
        p["a_log_pad"], p["dt_pad"])
    yb, s_new = _delta(q, k, v, z, bg, s_prev, p["g_dn_out"])
    t = b * l
    flat = lambda a: a.reshape(t, a.shape[-1])
    x2, xn = _merge(flat(x), flat(ya), flat(yb), flat(graw), p["w_up_pool"], p["w_up_dn"], p["w_out"], p["g_ffn"], tm_merge)
    experts, gates = _route(xn, p["w_peer_q"], p["peer_keys"], tm_route)
    y = _experts(experts, gates, xn, x2, p["g_final"], p["peer_table"])
    return (y.reshape(b, l, D_MODEL), pool_tail[:, POOL_PREFIX_ROWS - POOL_STATE:],
            conv_tail[:, CONV_PREFIX_ROWS - (CONV_W - 1):], s_new)


def kernel(x_prompt, x_sample, cache_pool, state_dn_conv, state_dn, g_mix, w_in, w_pool_grp, pool_scale, w_conv, a_log,
           dt_bias, g_dn_out, w_up_pool, w_up_dn, w_out, g_ffn, w_peer_q, peer_sub_keys, peer_u, peer_v, g_final):
    depth = w_in.shape[0]
    assert depth == 1
    bp = x_prompt.shape[0]
    lane_pad = lambda a, off: jnp.pad(a.astype(F32)[None, :], ((0, 0), (off, LANES - off - a.shape[0])))
    w = w_in[0]
    params = {
        "g_mix": g_mix[0][None, :],
        "w_u": w[:, OFF_U:OFF_QKV].astype(BF16),
        "w_qkv": w[:, OFF_QKV:OFF_Z].astype(BF16),
        "w_z": w[:, OFF_Z:OFF_B].astype(BF16),
        "w_ba": jnp.pad(w[:, OFF_B:OFF_G], ((0, 0), (0, LANES - 2 * DN_HEADS))).astype(BF16),
        "w_g": w[:, OFF_G:].astype(BF16),
        "w_grp": w_pool_grp[0].astype(BF16),
        "pool_scale": pool_scale[0][None, :],
        "w_conv8": jnp.pad(w_conv[0], ((0, SUBLANES - CONV_W), (0, 0))),
        "a_log_pad": lane_pad(a_log[0], DN_HEADS),
        "dt_pad": lane_pad(dt_bias[0], DN_HEADS),
        "g_dn_out": g_dn_out[0][None, :],
        "w_up_pool": w_up_pool[0].astype(BF16),
        "w_up_dn": w_up_dn[0].astype(BF16),
        "w_out": w_out[0].astype(BF16),
        "g_ffn": g_ffn[0][None, :],
        "w_peer_q": w_peer_q[0].reshape(D_MODEL, PEER_HEADS * PEER_DQ).astype(BF16),
        "peer_keys": peer_sub_keys[0].astype(BF16),
        "peer_table": jnp.concatenate([peer_u[0].reshape(-1, SUBLANES, LANES), peer_v[0].reshape(-1, SUBLANES, LANES)], axis=1),
        "g_final": g_final[None, :],
    }
    zeros = lambda *shape: jnp.zeros(shape, F32)
    yp, pool_p, conv_p, dn_p = _group(
        x_prompt, zeros(bp, POOL_STATE, D_POOL), zeros(bp, CONV_W - 1, CONV_CH), zeros(bp, DN_HEADS, DN_DK, DN_DV),
        0, params)
    ys, pool_s, conv_s, dn_s = _group(
        x_sample, cache_pool[0], state_dn_conv[0], state_dn[0].astype(F32),
        PAST_LEN, params)
    return (yp, ys, pool_p[None], conv_p[None], dn_p[None].astype(state_dn.dtype),
            pool_s[None], conv_s[None], dn_s[None].astype(state_dn.dtype))
```

```python
import functools

import jax
import jax.numpy as jnp
from jax import lax
from jax.experimental import pallas as pl
from jax.experimental.pallas import tpu as pltpu

D_MODEL = 1024
CHUNK = 64
D_POOL = 512
POOL_WINDOWS = (2, 4, 8, 16)
POOL_GROUP = 128
POOL_STATE = 15
DN_HEADS = 4
DN_DK = 128
DN_DV = 128
DN_QK = DN_HEADS * DN_DK
DN_VW = DN_HEADS * DN_DV
CONV_W = 4
CONV_CH = 2 * DN_QK + DN_VW
OFF_U = 0
OFF_QKV = OFF_U + D_POOL
OFF_Z = OFF_QKV + CONV_CH
OFF_B = OFF_Z + DN_VW
OFF_A = OFF_B + DN_HEADS
OFF_G = OFF_A + DN_HEADS
PEER_HEADS = 8
PEER_KEYS = 128
PEER_DQ = 256
PEER_TOPK = 16
PEER_PICKS = PEER_HEADS * PEER_TOPK
EPS = 1e-6
PAST_LEN = 4096

LANES = 128
SUBLANES = 8
POOL_PREFIX_ROWS = 16
CONV_PREFIX_ROWS = 8
VMEM_LIMIT = 56 * 1024 * 1024

F32 = jnp.float32
BF16 = jnp.bfloat16
HIGHEST = lax.Precision.HIGHEST


def _mm(a, b):
    return jnp.dot(a.astype(BF16), b.astype(BF16), preferred_element_type=F32)


def _mm_f32(a, b):
    return jnp.dot(a, b, precision=HIGHEST, preferred_element_type=F32)


def _mm_nt_f32(a, b):
    return lax.dot_general(a, b, (((1,), (1,)), ((), ())), precision=HIGHEST, preferred_element_type=F32)


def _mm_tn_f32(a, b):
    return lax.dot_general(a, b, (((0,), (0,)), ((), ())), precision=HIGHEST, preferred_element_type=F32)


def _sigmoid(x):
    return 1.0 / (1.0 + jnp.exp(-x))


def _silu(x):
    return x * _sigmoid(x)


def _softplus(x):
    return jnp.maximum(x, 0.0) + jnp.log1p(jnp.exp(-jnp.abs(x)))


def _rms(x, g):
    return x * lax.rsqrt(jnp.mean(x * x, axis=-1, keepdims=True) + EPS) * g


def _front_kernel(pos0, x_ref, pool_pre_ref, conv_pre_ref, g_mix_ref, w_u_ref, w_qkv_ref, w_z_ref, w_ba_ref, w_g_ref,
                  w_grp_ref, pool_scale_ref, w_conv_ref, a_log_ref, dt_ref,
                  ya_ref, q_ref, k_ref, v_ref, z_ref, graw_ref, bg_ref, pool_tail_ref, conv_tail_ref,
                  carry_u, carry_c):
    sb, tl, _ = x_ref.shape
    li = pl.program_id(1)

    @pl.when(li == 0)
    def _():
        carry_u[...] = pool_pre_ref[...]
        carry_c[...] = conv_pre_ref[...]

    x = x_ref[...].reshape(sb * tl, D_MODEL)
    h = _rms(x, g_mix_ref[...]).astype(BF16)
    u = jnp.dot(h, w_u_ref[...], preferred_element_type=F32)
    qkv = jnp.dot(h, w_qkv_ref[...], preferred_element_type=F32)
    z_ref[...] = jnp.dot(h, w_z_ref[...], preferred_element_type=F32).reshape(sb, tl, DN_VW)
    graw_ref[...] = jnp.dot(h, w_g_ref[...], preferred_element_type=F32).reshape(sb, tl, 2 * D_MODEL)
    ba = jnp.dot(h, w_ba_ref[...], preferred_element_type=F32)
    lane = lax.broadcasted_iota(jnp.int32, ba.shape, 1)
    beta = _sigmoid(ba)
    g = -jnp.exp(a_log_ref[...]) * _softplus(ba + dt_ref[...])
    bg_ref[...] = jnp.where(lane < DN_HEADS, beta, g).reshape(sb, tl, LANES)

    row = lax.broadcasted_iota(jnp.int32, (tl, POOL_GROUP), 0)
    pos1 = pos0 + li * tl + row + 1
    w_conv = w_conv_ref[...]

    for s in range(sb):
        u_s = u[s * tl:(s + 1) * tl]
        ext = jnp.concatenate([carry_u[s], u_s], axis=0)
        mixed = []
        for gi, w in enumerate(POOL_WINDOWS):
            acc = ext[:, gi * POOL_GROUP:(gi + 1) * POOL_GROUP]
            span = 1
            while span < w:
                acc = acc + pltpu.roll(acc, span, axis=0)
                span *= 2
            win = acc[POOL_PREFIX_ROWS:]
            cnt = jnp.minimum(pos1, w).astype(F32)
            pooled = win / cnt - u_s[:, gi * POOL_GROUP:(gi + 1) * POOL_GROUP]
            mixed.append(_mm(pooled, w_grp_ref[gi]))
        ya_ref[s] = jnp.concatenate(mixed, axis=1) * pool_scale_ref[...]
        pool_tail_ref[s] = ext[tl:]
        carry_u[s] = ext[tl:]

        c_s = qkv[s * tl:(s + 1) * tl]
        cext = jnp.concatenate([carry_c[s], c_s], axis=0)
        y = c_s * w_conv[CONV_W - 1:CONV_W]
        for j in range(1, CONV_W):
            y = y + pltpu.roll(cext, j, axis=0)[CONV_PREFIX_ROWS:] * w_conv[CONV_W - 1 - j:CONV_W - j]
        y = _silu(y)
        conv_tail_ref[s] = cext[tl:]
        carry_c[s] = cext[tl:]
        for hh in range(DN_HEADS):
            sl = slice(hh * DN_DK, (hh + 1) * DN_DK)
            qh = y[:, sl]
            q_ref[s, :, sl] = qh * lax.rsqrt(jnp.sum(qh * qh, axis=-1, keepdims=True) + EPS)
            kh = y[:, DN_QK + hh * DN_DK:DN_QK + (hh + 1) * DN_DK]
            k_ref[s, :, sl] = kh * lax.rsqrt(jnp.sum(kh * kh, axis=-1, keepdims=True) + EPS)
        v_ref[s] = y[:, 2 * DN_QK:]


def _front(x, pool_pre, conv_pre, pos0, sb, tl, g_mix, w_u, w_qkv, w_z, w_ba, w_g, w_grp, pool_scale, w_conv8,
           a_log_pad, dt_pad):
    b, l, _ = x.shape
    grid = (b // sb, l // tl)
    tok = lambda width: pl.BlockSpec((sb, tl, width), lambda i, j: (i, j, 0))
    seq = lambda rows, width: pl.BlockSpec((sb, rows, width), lambda i, j: (i, 0, 0))
    full = lambda a: pl.BlockSpec(a.shape, lambda i, j: (0,) * a.ndim)
    weights = (g_mix, w_u, w_qkv, w_z, w_ba, w_g, w_grp, pool_scale, w_conv8, a_log_pad, dt_pad)
    out_shape = (
        jax.ShapeDtypeStruct((b, l, D_POOL), F32),
        jax.ShapeDtypeStruct((b, l, DN_QK), F32),
        jax.ShapeDtypeStruct((b, l, DN_QK), F32),
        jax.ShapeDtypeStruct((b, l, DN_VW), F32),
        jax.ShapeDtypeStruct((b, l, DN_VW), F32),
        jax.ShapeDtypeStruct((b, l, 2 * D_MODEL), F32),
        jax.ShapeDtypeStruct((b, l, LANES), F32),
        jax.ShapeDtypeStruct((b, POOL_PREFIX_ROWS, D_POOL), F32),
        jax.ShapeDtypeStruct((b, CONV_PREFIX_ROWS, CONV_CH), F32),
    )
    return pl.pallas_call(
        functools.partial(_front_kernel, pos0),
        out_shape=out_shape,
        grid=grid,
        in_specs=[tok(D_MODEL), seq(POOL_PREFIX_ROWS, D_POOL), seq(CONV_PREFIX_ROWS, CONV_CH)] + [full(a) for a in weights],
        out_specs=(tok(D_POOL), tok(DN_QK), tok(DN_QK), tok(DN_VW), tok(DN_VW), tok(2 * D_MODEL), tok(LANES),
                   seq(POOL_PREFIX_ROWS, D_POOL), seq(CONV_PREFIX_ROWS, CONV_CH)),
        scratch_shapes=[pltpu.VMEM((sb, POOL_PREFIX_ROWS, D_POOL), F32), pltpu.VMEM((sb, CONV_PREFIX_ROWS, CONV_CH), F32)],
        compiler_params=pltpu.CompilerParams(dimension_semantics=("arbitrary", "arbitrary"), vmem_limit_bytes=VMEM_LIMIT),
        name="mixer_front",
    )(x, pool_pre, conv_pre, *weights)


_NN = (((1,), (0,)), ((), ()))
_NT = (((1,), (1,)), ((), ()))
_TN = (((0,), (0,)), ((), ()))


def _dot_bf16(a, b, dims):
    return lax.dot_general(a.astype(BF16), b.astype(BF16), dims, preferred_element_type=F32)


def _mm_nt(a, b):
    return _dot_bf16(a, b, _NT)


def _mm_chain(a, b):
    return _dot_bf16(a, b, _NN)


def _mm_state(a, b):
    return _dot_bf16(a, b, _NN)


def _mm_tn_state(a, b):
    return _dot_bf16(a, b, _TN)


def _delta_prep_kernel(q_ref, k_ref, v_ref, bg_ref, uv_ref, wk_ref, qd_ref, kd_ref, qk_ref, gl_ref):
    sb = q_ref.shape[0]
    ncb = q_ref.shape[1] // CHUNK
    row = lax.broadcasted_iota(jnp.int32, (CHUNK, CHUNK), 0)
    col = lax.broadcasted_iota(jnp.int32, (CHUNK, CHUNK), 1)
    incl = row >= col
    strict = row > col
    tri = incl.astype(F32)
    xs, pw = [], []
    for s in range(sb):
        for c in range(ncb):
            rows = slice(c * CHUNK, (c + 1) * CHUNK)
            bg = bg_ref[s, rows, :]
            gc_all = _mm_f32(tri, bg)
            gc_rows = gc_all.T
            gl_ref[s, c * SUBLANES:(c + 1) * SUBLANES, :] = gc_all[CHUNK - SUBLANES:, :]
            for hh in range(DN_HEADS):
                sl = slice(hh * DN_DK, (hh + 1) * DN_DK)
                q = q_ref[s, rows, sl] * (DN_DK ** -0.5)
                k = k_ref[s, rows, sl]
                beta = bg[:, hh:hh + 1]
                gcol = gc_all[:, DN_HEADS + hh:DN_HEADS + hh + 1]
                grow = gc_rows[DN_HEADS + hh:DN_HEADS + hh + 1, :]
                decay = jnp.exp(jnp.where(incl, gcol - grow, -jnp.inf))
                egc = jnp.exp(gcol)
                kb = k * beta
                pw.append(-jnp.where(strict, _mm_nt(kb, k) * decay, 0.0))
                xs.append(jnp.concatenate([v_ref[s, rows, sl] * beta, kb * egc], axis=1))
                qk_ref[s, rows, hh * CHUNK:(hh + 1) * CHUNK] = _mm_nt(q, k) * decay
                gl = gcol[CHUNK - 1:CHUNK, :]
                qd_ref[s, rows, sl] = q * egc
                kd_ref[s, rows, sl] = k * jnp.exp(gl - gcol)
    span = 1
    while True:
        xs = [x + _mm_chain(p, x) for p, x in zip(pw, xs)]
        span *= 2
        if span >= CHUNK:
            break
        pw = [_mm_chain(p, p) for p in pw]
    i = 0
    for s in range(sb):
        for c in range(ncb):
            rows = slice(c * CHUNK, (c + 1) * CHUNK)
            for hh in range(DN_HEADS):
                sl = slice(hh * DN_DK, (hh + 1) * DN_DK)
                uv_ref[s, rows, sl] = xs[i][:, :DN_DV]
                wk_ref[s, rows, sl] = xs[i][:, DN_DV:]
                i += 1


def _delta_scan_kernel(uv_ref, wk_ref, qd_ref, kd_ref, qk_ref, gl_ref, z_ref, s0_ref, g_out_ref, yb_ref, s_out_ref, s_scr):
    ci = pl.program_id(1)
    bb = uv_ref.shape[0]

    @pl.when(ci == 0)
    def _():
        s_scr[...] = s0_ref[...]

    g_out = g_out_ref[...]
    probs = [(b, hh) for b in range(bb) for hh in range(DN_HEADS)]
    lanes = lambda hh: slice(hh * DN_DK, (hh + 1) * DN_DK)
    s_old = [s_scr[b, hh] for b, hh in probs]
    v_new = [uv_ref[b, :, lanes(hh)] - _mm_state(wk_ref[b, :, lanes(hh)], s) for (b, hh), s in zip(probs, s_old)]
    o_state = [_mm_state(qd_ref[b, :, lanes(hh)], s) for (b, hh), s in zip(probs, s_old)]
    for (b, hh), s, vn, os_ in zip(probs, s_old, v_new, o_state):
        o = os_ + _mm_state(qk_ref[b, :, hh * CHUNK:(hh + 1) * CHUNK], vn)
        egl = jnp.exp(gl_ref[b, SUBLANES - 1:SUBLANES, DN_HEADS + hh:DN_HEADS + hh + 1])
        s_new = s * egl + _mm_tn_state(kd_ref[b, :, lanes(hh)], vn)
        s_scr[b, hh] = s_new
        s_out_ref[b, hh] = s_new
        yb_ref[b, :, lanes(hh)] = _rms(o, g_out) * _silu(z_ref[b, :, lanes(hh)])


DELTA_PREP_CHUNKS = 4
DELTA_SCAN_SEQS = 8


def _delta(q, k, v, z, bg, s0, g_out):
    b, l, _ = q.shape
    nc = l // CHUNK
    ncb = min(DELTA_PREP_CHUNKS, nc)
    sb = min(b, DELTA_PREP_CHUNKS // ncb)
    rows = ncb * CHUNK
    tok = lambda width: pl.BlockSpec((sb, rows, width), lambda i, j: (i, j, 0))
    glspec = pl.BlockSpec((sb, ncb * SUBLANES, LANES), lambda i, j: (i, j, 0))
    wide = lambda width: jax.ShapeDtypeStruct((b, l, width), F32)
    uv, wk, qd, kd, qk, gl = pl.pallas_call(
        _delta_prep_kernel,
        out_shape=(wide(DN_VW), wide(DN_QK), wide(DN_QK), wide(DN_QK), wide(DN_HEADS * CHUNK),
                   jax.ShapeDtypeStruct((b, nc * SUBLANES, LANES), F32)),
        grid=(b // sb, nc // ncb),
        in_specs=[tok(DN_QK), tok(DN_QK), tok(DN_VW), tok(LANES)],
        out_specs=(tok(DN_VW), tok(DN_QK), tok(DN_QK), tok(DN_QK), tok(DN_HEADS * CHUNK), glspec),
        compiler_params=pltpu.CompilerParams(dimension_semantics=("arbitrary", "arbitrary"), vmem_limit_bytes=VMEM_LIMIT),
        name="delta_prep",
    )(q, k, v, bg)

    bb = min(DELTA_SCAN_SEQS, b)
    ctok = lambda width: pl.BlockSpec((bb, CHUNK, width), lambda i, j: (i, j, 0))
    st = pl.BlockSpec((bb, DN_HEADS, DN_DK, DN_DV), lambda i, j: (i, 0, 0, 0))
    return pl.pallas_call(
        _delta_scan_kernel,
        out_shape=(wide(DN_VW), jax.ShapeDtypeStruct((b, DN_HEADS, DN_DK, DN_DV), F32)),
        grid=(b // bb, nc),
        in_specs=[ctok(DN_VW), ctok(DN_QK), ctok(DN_QK), ctok(DN_QK), ctok(DN_HEADS * CHUNK),
                  pl.BlockSpec((bb, SUBLANES, LANES), lambda i, j: (i, j, 0)), ctok(DN_VW), st,
                  pl.BlockSpec((1, DN_DV), lambda i, j: (0, 0))],
        out_specs=(ctok(DN_VW), st),
        scratch_shapes=[pltpu.VMEM((bb, DN_HEADS, DN_DK, DN_DV), F32)],
        compiler_params=pltpu.CompilerParams(dimension_semantics=("arbitrary", "arbitrary"), vmem_limit_bytes=VMEM_LIMIT),
        name="delta_scan",
    )(uv, wk, qd, kd, qk, gl, z, s0, g_out)


def _merge_kernel(x_ref, ya_ref, yb_ref, graw_ref, w_up_pool_ref, w_up_dn_ref, w_out_ref, g_ffn_ref, x2_ref, xn_ref):
    graw = graw_ref[...]
    ga = _sigmoid(graw[:, :D_MODEL])
    gb = _sigmoid(graw[:, D_MODEL:])
    merged = ga * _mm(ya_ref[...], w_up_pool_ref[...]) + gb * _mm(yb_ref[...], w_up_dn_ref[...])
    x2 = x_ref[...] + _mm(merged, w_out_ref[...])
    x2_ref[...] = x2
    xn_ref[...] = _rms(x2, g_ffn_ref[...])


def _merge(x, ya, yb, graw, w_up_pool, w_up_dn, w_out, g_ffn, tm):
    t = x.shape[0]
    tok = lambda width: pl.BlockSpec((tm, width), lambda i: (i, 0))
    full = lambda a: pl.BlockSpec(a.shape, lambda i: (0,) * a.ndim)
    weights = (w_up_pool, w_up_dn, w_out, g_ffn)
    return pl.pallas_call(
        _merge_kernel,
        out_shape=(jax.ShapeDtypeStruct((t, D_MODEL), F32), jax.ShapeDtypeStruct((t, D_MODEL), F32)),
        grid=(t // tm,),
        in_specs=[tok(D_MODEL), tok(D_POOL), tok(DN_VW), tok(2 * D_MODEL)] + [full(a) for a in weights],
        out_specs=(tok(D_MODEL), tok(D_MODEL)),
        compiler_params=pltpu.CompilerParams(dimension_semantics=("arbitrary",), vmem_limit_bytes=VMEM_LIMIT),
        name="branch_merge",
    )(x, ya, yb, graw, *weights)


def _top16_rows(s, ids=None):
    if ids is None:
        ids = lax.broadcasted_iota(jnp.int32, s.shape, 0)
    ids = ids.astype(F32)
    vals, idxs = [], []
    for _ in range(PEER_TOPK):
        m = jnp.max(s, axis=0, keepdims=True)
        idx = jnp.min(jnp.where(s == m, ids, jnp.inf), axis=0, keepdims=True)
        vals.append(m)
        idxs.append(idx)
        s = jnp.where(ids == idx, -jnp.inf, s)
    return jnp.concatenate(vals, axis=0), jnp.concatenate(idxs, axis=0).astype(jnp.int32)


def _pair_candidates(v1, v2):
    tokens = v1.shape[1]
    sub = lax.broadcasted_iota(jnp.int32, (SUBLANES, tokens), 0)
    vals = [v1[0:1] + v2[0:SUBLANES], v1[0:1] + v2[SUBLANES:]]
    ids = [sub, sub + SUBLANES]
    for a in range(1, SUBLANES):
        vals.append(v1[a:a + 1] + v2[0:SUBLANES])
        ids.append(sub + a * PEER_TOPK)
    vals.append(v1[SUBLANES:] + v2[0:1])
    ids.append((sub + SUBLANES) * PEER_TOPK)
    return jnp.concatenate(vals, axis=0), jnp.concatenate(ids, axis=0)


def _take_rows(table, idx):
    out = jnp.zeros_like(table)
    for a in range(PEER_TOPK):
        out = jnp.where(idx == a, table[a:a + 1, :], out)
    return out


def _route_kernel(xn_ref, wq_ref, keys_ref, experts_ref, gates_ref):
    half = PEER_DQ // 2
    q = jnp.dot(xn_ref[...].astype(BF16), wq_ref[...], preferred_element_type=F32).astype(BF16)
    experts, gates = [], []
    for hh in range(PEER_HEADS):
        q1 = q[:, hh * PEER_DQ:hh * PEER_DQ + half]
        q2 = q[:, hh * PEER_DQ + half:(hh + 1) * PEER_DQ]
        nt = (((1,), (1,)), ((), ()))
        s1 = lax.dot_general(keys_ref[0, hh], q1, nt, preferred_element_type=F32)
        s2 = lax.dot_general(keys_ref[1, hh], q2, nt, preferred_element_type=F32)
        v1, i1 = _top16_rows(s1)
        v2, i2 = _top16_rows(s2)
        cv, ci = _top16_rows(*_pair_candidates(v1, v2))
        e1 = _take_rows(i1, ci // PEER_TOPK)
        e2 = _take_rows(i2, ci % PEER_TOPK)
        experts.append(e1 * PEER_KEYS + e2)
        ex = jnp.exp(cv - cv[0:1, :])
        gates.append(ex / jnp.sum(ex, axis=0, keepdims=True))
    experts_ref[...] = jnp.concatenate(experts, axis=0).T
    gates_ref[...] = jnp.concatenate(gates, axis=0).T


def _route(xn, wq, keys, tm):
    t = xn.shape[0]
    return pl.pallas_call(
        _route_kernel,
        out_shape=(jax.ShapeDtypeStruct((t, PEER_PICKS), jnp.int32), jax.ShapeDtypeStruct((t, PEER_PICKS), F32)),
        grid=(t // tm,),
        in_specs=[pl.BlockSpec((tm, D_MODEL), lambda i: (i, 0)),
                  pl.BlockSpec(wq.shape, lambda i: (0, 0)),
                  pl.BlockSpec(keys.shape, lambda i: (0, 0, 0, 0))],
        out_specs=(pl.BlockSpec((tm, PEER_PICKS), lambda i: (i, 0)), pl.BlockSpec((tm, PEER_PICKS), lambda i: (i, 0))),
        compiler_params=pltpu.CompilerParams(dimension_semantics=("arbitrary",), vmem_limit_bytes=VMEM_LIMIT),
        name="peer_route",
    )(xn, wq, keys)


EXPERT_TOKENS = 16
EXPERT_ISSUE_TOKENS = 12


def _experts_kernel(idx_ref, idx_next_ref, gates_ref, xn_ref, x2_ref, g_final_ref, tab_ref, out_ref, buf_a, buf_b, cb, sem):
    bufs = (buf_a, buf_b)
    i = pl.program_id(0)
    n = pl.num_programs(0)
    tb = EXPERT_TOKENS
    lane = lax.broadcasted_iota(jnp.int32, (PEER_HEADS, PEER_TOPK), 1)
    g_final = g_final_ref[...]

    def issue_picks(ref, sl, lo, hi):
        for f in range(lo, min(hi, tb * PEER_PICKS)):
            t, h, k = f // PEER_PICKS, (f // PEER_TOPK) % PEER_HEADS, f % PEER_TOPK
            e = ref[0, t, h * PEER_TOPK + k]
            pltpu.async_copy(tab_ref.at[e], bufs[sl].at[t, k, h], sem.at[sl], priority=k % 2)

    def wait_slot(sl):
        for t in range(tb):
            pltpu.make_async_copy(bufs[sl].at[t], bufs[sl].at[t], sem.at[sl]).wait()

    def gate_coefficients(t, sl):
        x = xn_ref[t]
        sub = lax.broadcasted_iota(jnp.int32, (PEER_HEADS, PEER_TOPK), 0)
        act = jnp.zeros((PEER_HEADS, PEER_TOPK), F32)
        for h in range(PEER_HEADS):
            part = jnp.zeros((SUBLANES, PEER_TOPK), F32)
            for k in range(PEER_TOPK):
                part = jnp.where(lane == k, jnp.sum(bufs[sl][t, k, h, :SUBLANES, :] * x, axis=-1, keepdims=True), part)
            act = jnp.where(sub == h, jnp.sum(part, axis=0, keepdims=True), act)
        coef = gates_ref[t] * (0.5 * act * (1.0 + lax.erf(act * (0.5 ** 0.5))))
        for k in range(PEER_TOPK):
            cb[t, k] = jnp.broadcast_to(coef[:, k:k + 1], (PEER_HEADS, LANES))

    def mix_token(t, sl):
        accs = [jnp.zeros((SUBLANES, LANES), F32) for _ in range(4)]
        for k in range(PEER_TOPK):
            for h in range(PEER_HEADS):
                c = jnp.broadcast_to(cb[t, k, h:h + 1, :], (SUBLANES, LANES))
                accs[h % 4] = accs[h % 4] + c * bufs[sl][t, k, h, SUBLANES:, :]
        z = x2_ref[t] + ((accs[0] + accs[1]) + (accs[2] + accs[3]))
        ms = jnp.sum(jnp.sum(z * z, axis=-1, keepdims=True), axis=0, keepdims=True) * (1.0 / D_MODEL)
        out_ref[t] = z * lax.rsqrt(ms + EPS) * g_final

    @pl.when(i == 0)
    def _():
        issue_picks(idx_ref, 0, 0, tb * PEER_PICKS)

    per_token = -(-tb * PEER_PICKS // EXPERT_ISSUE_TOKENS)

    def step(sl):
        wait_slot(sl)
        gate_coefficients(0, sl)
        for t in range(tb):
            @pl.when(i + t >= 0)
            def _():
                issue_picks(idx_next_ref, 1 - sl, t * per_token, (t + 1) * per_token)
                if t + 1 < tb:
                    gate_coefficients(t + 1, sl)
                mix_token(t, sl)

        @pl.when(i == n - 1)
        def _():
            wait_slot(1 - sl)

    @pl.when(i % 2 == 0)
    def _():
        step(0)

    @pl.when(i % 2 == 1)
    def _():
        step(1)


def _experts(experts, gates, xn, x2, g_final, table):
    t = xn.shape[0]
    tb = EXPERT_TOKENS
    nb = t // tb
    idx = experts.reshape(nb, tb, PEER_PICKS)
    gates = gates.reshape(t, PEER_HEADS, PEER_TOPK)
    tiles = lambda a: a.reshape(a.shape[0], SUBLANES, LANES)
    tok = pl.BlockSpec((tb, SUBLANES, LANES), lambda i: (i, 0, 0))
    out = pl.pallas_call(
        _experts_kernel,
        out_shape=jax.ShapeDtypeStruct((t, SUBLANES, LANES), F32),
        grid=(nb,),
        in_specs=[pl.BlockSpec((1, tb, PEER_PICKS), lambda i: (i, 0, 0), memory_space=pltpu.SMEM),
                  pl.BlockSpec((1, tb, PEER_PICKS), lambda i: (jnp.minimum(i + 1, nb - 1), 0, 0), memory_space=pltpu.SMEM),
                  pl.BlockSpec((tb, PEER_HEADS, PEER_TOPK), lambda i: (i, 0, 0)), tok, tok,
                  pl.BlockSpec((SUBLANES, LANES), lambda i: (0, 0)),
                  pl.BlockSpec(memory_space=pl.ANY)],
        out_specs=tok,
        scratch_shapes=[pltpu.VMEM((tb, PEER_TOPK, PEER_HEADS, 2 * SUBLANES, LANES), F32),
                        pltpu.VMEM((tb, PEER_TOPK, PEER_HEADS, 2 * SUBLANES, LANES), F32),
                        pltpu.VMEM((tb, PEER_TOPK, PEER_HEADS, LANES), F32),
                        pltpu.SemaphoreType.DMA((2,))],
        compiler_params=pltpu.CompilerParams(dimension_semantics=("arbitrary",), vmem_limit_bytes=VMEM_LIMIT),
        name="peer_experts",
    )(idx, idx, gates, tiles(xn), tiles(x2), g_final.reshape(SUBLANES, LANES), table)
    return out.reshape(t, D_MODEL)


def _pad_rows_front(a, rows):
    return jnp.pad(a, ((0, 0), (rows - a.shape[1], 0), (0, 0)))


MERGE_ROWS = 512
ROUTE_ROWS = 256
FRONT_ROWS = 512


def _group(x, pool_prev, conv_prev, s_prev, pos0, p):
    b, l, _ = x.shape
    tl = min(FRONT_ROWS, l)
    sb = min(b, FRONT_ROWS // tl)
    tm_merge = min(MERGE_ROWS, b * l)
    tm_route = min(ROUTE_ROWS, b * l)
    ya, q, k, v, z, graw, bg, pool_tail, conv_tail = _front(
        x, _pad_rows_front(pool_prev, POOL_PREFIX_ROWS), _pad_rows_front(conv_prev, CONV_PREFIX_ROWS), pos0, sb, tl,
        p["g_mix"], p["w_u"], p["w_qkv"], p["w_z"], p["w_ba"], p["w_g"], p["w_grp"], p["pool_scale"], p["w_conv8"],
        p["a_log_pad"], p["dt_pad"])
    yb, s_new = _delta(q, k, v, z, bg, s_prev, p["g_dn_out"])
    t = b * l
    flat = lambda a: a.reshape(t, a.shape[-1])
    x2, xn = _merge(flat(x), flat(ya), flat(yb), flat(graw), p["w_up_pool"], p["w_up_dn"], p["w_out"], p["g_ffn"], tm_merge)
    experts, gates = _route(xn, p["w_peer_q"], p["peer_keys"], tm_route)
    y = _experts(experts, gates, xn, x2, p["g_final"], p["peer_table"])
    return (y.reshape(b, l, D_MODEL), pool_tail[:, POOL_PREFIX_ROWS - POOL_STATE:],
            conv_tail[:, CONV_PREFIX_ROWS - (CONV_W - 1):], s_new)


def kernel(x_prompt, x_sample, cache_pool, state_dn_conv, state_dn, g_mix, w_in, w_pool_grp, pool_scale, w_conv, a_log,
           dt_bias, g_dn_out, w_up_pool, w_up_dn, w_out, g_ffn, w_peer_q, peer_sub_keys, peer_u, peer_v, g_final):
    depth = w_in.shape[0]
    assert depth == 1
    bp = x_prompt.shape[0]
    lane_pad = lambda a, off: jnp.pad(a.astype(F32)[None, :], ((0, 0), (off, LANES - off - a.shape[0])))
    w = w_in[0]
    params = {
        "g_mix": g_mix[0][None, :],
        "w_u": w[:, OFF_U:OFF_QKV].astype(BF16),
        "w_qkv": w[:, OFF_QKV:OFF_Z].astype(BF16),
        "w_z": w[:, OFF_Z:OFF_B].astype(BF16),
        "w_ba": jnp.pad(w[:, OFF_B:OFF_G], ((0, 0), (0, LANES - 2 * DN_HEADS))).astype(BF16),
        "w_g": w[:, OFF_G:].astype(BF16),
        "w_grp": w_pool_grp[0].astype(BF16),
        "pool_scale": pool_scale[0][None, :],
        "w_conv8": jnp.pad(w_conv[0], ((0, SUBLANES - CONV_W), (0, 0))),
        "a_log_pad": lane_pad(a_log[0], DN_HEADS),
        "dt_pad": lane_pad(dt_bias[0], DN_HEADS),
        "g_dn_out": g_dn_out[0][None, :],
        "w_up_pool": w_up_pool[0].astype(BF16),
        "w_up_dn": w_up_dn[0].astype(BF16),
        "w_out": w_out[0].astype(BF16),
        "g_ffn": g_ffn[0][None, :],
        "w_peer_q": w_peer_q[0].reshape(D_MODEL, PEER_HEADS * PEER_DQ).astype(BF16),
        "peer_keys": peer_sub_keys[0].astype(BF16),
        "peer_table": jnp.concatenate([peer_u[0].reshape(-1, SUBLANES, LANES), peer_v[0].reshape(-1, SUBLANES, LANES)], axis=1),
        "g_final": g_final[None, :],
    }
    zeros = lambda *shape: jnp.zeros(shape, F32)
    yp, pool_p, conv_p, dn_p = _group(
        x_prompt, zeros(bp, POOL_STATE, D_POOL), zeros(bp, CONV_W - 1, CONV_CH), zeros(bp, DN_HEADS, DN_DK, DN_DV),
        0, params)
    ys, pool_s, conv_s, dn_s = _group(
        x_sample, cache_pool[0], state_dn_conv[0], state_dn[0].astype(F32),
        PAST_LEN, params)
    return (yp, ys, pool_p[None], conv_p[None], dn_p[None].astype(state_dn.dtype),
            pool_s[None], conv_s[None], dn_s[None].astype(state_dn.dtype))
```

```python
import functools

import jax
import jax.numpy as jnp
from jax import lax
from jax.experimental import pallas as pl
from jax.experimental.pallas import tpu as pltpu

D_MODEL = 1024
CHUNK = 64
D_POOL = 512
POOL_WINDOWS = (2, 4, 8, 16)
POOL_GROUP = 128
POOL_STATE = 15
DN_HEADS = 4
DN_DK = 128
DN_DV = 128
DN_QK = DN_HEADS * DN_DK
DN_VW = DN_HEADS * DN_DV
CONV_W = 4
CONV_CH = 2 * DN_QK + DN_VW
OFF_U = 0
OFF_QKV = OFF_U + D_POOL
OFF_Z = OFF_QKV + CONV_CH
OFF_B = OFF_Z + DN_VW
OFF_A = OFF_B + DN_HEADS
OFF_G = OFF_A + DN_HEADS
PEER_HEADS = 8
PEER_KEYS = 128
PEER_DQ = 256
PEER_TOPK = 16
PEER_PICKS = PEER_HEADS * PEER_TOPK
EPS = 1e-6
PAST_LEN = 4096

LANES = 128
SUBLANES = 8
POOL_PREFIX_ROWS = 16
CONV_PREFIX_ROWS = 8
VMEM_LIMIT = 56 * 1024 * 1024

F32 = jnp.float32
BF16 = jnp.bfloat16
HIGHEST = lax.Precision.HIGHEST


def _mm(a, b):
    return jnp.dot(a.astype(BF16), b.astype(BF16), preferred_element_type=F32)


def _mm_f32(a, b):
    return jnp.dot(a, b, precision=HIGHEST, preferred_element_type=F32)


def _mm_nt_f32(a, b):
    return lax.dot_general(a, b, (((1,), (1,)), ((), ())), precision=HIGHEST, preferred_element_type=F32)


def _mm_tn_f32(a, b):
    return lax.dot_general(a, b, (((0,), (0,)), ((), ())), precision=HIGHEST, preferred_element_type=F32)


def _sigmoid(x):
    return 1.0 / (1.0 + jnp.exp(-x))


def _silu(x):
    return x * _sigmoid(x)


def _softplus(x):
    return jnp.maximum(x, 0.0) + jnp.log1p(jnp.exp(-jnp.abs(x)))


def _rms(x, g):
    return x * lax.rsqrt(jnp.mean(x * x, axis=-1, keepdims=True) + EPS) * g


def _front_kernel(pos0, x_ref, pool_pre_ref, conv_pre_ref, g_mix_ref, w_u_ref, w_qkv_ref, w_z_ref, w_ba_ref, w_g_ref,
                  w_grp_ref, pool_scale_ref, w_conv_ref, a_log_ref, dt_ref,
                  ya_ref, q_ref, k_ref, v_ref, z_ref, graw_ref, bg_ref, pool_tail_ref, conv_tail_ref,
                  carry_u, carry_c):
    sb, tl, _ = x_ref.shape
    li = pl.program_id(1)

    @pl.when(li == 0)
    def _():
        carry_u[...] = pool_pre_ref[...]
        carry_c[...] = conv_pre_ref[...]

    x = x_ref[...].reshape(sb * tl, D_MODEL)
    h = _rms(x, g_mix_ref[...]).astype(BF16)
    u = jnp.dot(h, w_u_ref[...], preferred_element_type=F32)
    qkv = jnp.dot(h, w_qkv_ref[...], preferred_element_type=F32)
    z_ref[...] = jnp.dot(h, w_z_ref[...], preferred_element_type=F32).reshape(sb, tl, DN_VW)
    graw_ref[...] = jnp.dot(h, w_g_ref[...], preferred_element_type=F32).reshape(sb, tl, 2 * D_MODEL)
    ba = jnp.dot(h, w_ba_ref[...], preferred_element_type=F32)
    lane = lax.broadcasted_iota(jnp.int32, ba.shape, 1)
    beta = _sigmoid(ba)
    g = -jnp.exp(a_log_ref[...]) * _softplus(ba + dt_ref[...])
    bg_ref[...] = jnp.where(lane < DN_HEADS, beta, g).reshape(sb, tl, LANES)

    row = lax.broadcasted_iota(jnp.int32, (tl, POOL_GROUP), 0)
    pos1 = pos0 + li * tl + row + 1
    w_conv = w_conv_ref[...]

    for s in range(sb):
        u_s = u[s * tl:(s + 1) * tl]
        ext = jnp.concatenate([carry_u[s], u_s], axis=0)
        mixed = []
        for gi, w in enumerate(POOL_WINDOWS):
            acc = ext[:, gi * POOL_GROUP:(gi + 1) * POOL_GROUP]
            span = 1
            while span < w:
                acc = acc + pltpu.roll(acc, span, axis=0)
                span *= 2
            win = acc[POOL_PREFIX_ROWS:]
            cnt = jnp.minimum(pos1, w).astype(F32)
            pooled = win / cnt - u_s[:, gi * POOL_GROUP:(gi + 1) * POOL_GROUP]
            mixed.append(_mm(pooled, w_grp_ref[gi]))
        ya_ref[s] = jnp.concatenate(mixed, axis=1) * pool_scale_ref[...]
        pool_tail_ref[s] = ext[tl:]
        carry_u[s] = ext[tl:]

        c_s = qkv[s * tl:(s + 1) * tl]
        cext = jnp.concatenate([carry_c[s], c_s], axis=0)
        y = c_s * w_conv[CONV_W - 1:CONV_W]
        for j in range(1, CONV_W):
            y = y + pltpu.roll(cext, j, axis=0)[CONV_PREFIX_ROWS:] * w_conv[CONV_W - 1 - j:CONV_W - j]
        y = _silu(y)
        conv_tail_ref[s] = cext[tl:]
        carry_c[s] = cext[tl:]
        for hh in range(DN_HEADS):
            sl = slice(hh * DN_DK, (hh + 1) * DN_DK)
            qh = y[:, sl]
            q_ref[s, :, sl] = qh * lax.rsqrt(jnp.sum(qh * qh, axis=-1, keepdims=True) + EPS)
            kh = y[:, DN_QK + hh * DN_DK:DN_QK + (hh + 1) * DN_DK]
            k_ref[s, :, sl] = kh * lax.rsqrt(jnp.sum(kh * kh, axis=-1, keepdims=True) + EPS)
        v_ref[s] = y[:, 2 * DN_QK:]


def _front(x, pool_pre, conv_pre, pos0, sb, tl, g_mix, w_u, w_qkv, w_z, w_ba, w_g, w_grp, pool_scale, w_conv8,
           a_log_pad, dt_pad):
    b, l, _ = x.shape
    grid = (b // sb, l // tl)
    tok = lambda width: pl.BlockSpec((sb, tl, width), lambda i, j: (i, j, 0))
    seq = lambda rows, width: pl.BlockSpec((sb, rows, width), lambda i, j: (i, 0, 0))
    full = lambda a: pl.BlockSpec(a.shape, lambda i, j: (0,) * a.ndim)
    weights = (g_mix, w_u, w_qkv, w_z, w_ba, w_g, w_grp, pool_scale, w_conv8, a_log_pad, dt_pad)
    out_shape = (
        jax.ShapeDtypeStruct((b, l, D_POOL), F32),
        jax.ShapeDtypeStruct((b, l, DN_QK), F32),
        jax.ShapeDtypeStruct((b, l, DN_QK), F32),
        jax.ShapeDtypeStruct((b, l, DN_VW), F32),
        jax.ShapeDtypeStruct((b, l, DN_VW), F32),
        jax.ShapeDtypeStruct((b, l, 2 * D_MODEL), F32),
        jax.ShapeDtypeStruct((b, l, LANES), F32),
        jax.ShapeDtypeStruct((b, POOL_PREFIX_ROWS, D_POOL), F32),
        jax.ShapeDtypeStruct((b, CONV_PREFIX_ROWS, CONV_CH), F32),
    )
    return pl.pallas_call(
        functools.partial(_front_kernel, pos0),
        out_shape=out_shape,
        grid=grid,
        in_specs=[tok(D_MODEL), seq(POOL_PREFIX_ROWS, D_POOL), seq(CONV_PREFIX_ROWS, CONV_CH)] + [full(a) for a in weights],
        out_specs=(tok(D_POOL), tok(DN_QK), tok(DN_QK), tok(DN_VW), tok(DN_VW), tok(2 * D_MODEL), tok(LANES),
                   seq(POOL_PREFIX_ROWS, D_POOL), seq(CONV_PREFIX_ROWS, CONV_CH)),
        scratch_shapes=[pltpu.VMEM((sb, POOL_PREFIX_ROWS, D_POOL), F32), pltpu.VMEM((sb, CONV_PREFIX_ROWS, CONV_CH), F32)],
        compiler_params=pltpu.CompilerParams(dimension_semantics=("arbitrary", "arbitrary"), vmem_limit_bytes=VMEM_LIMIT),
        name="mixer_front",
    )(x, pool_pre, conv_pre, *weights)


_NN = (((1,), (0,)), ((), ()))
_NT = (((1,), (1,)), ((), ()))
_TN = (((0,), (0,)), ((), ()))


def _dot_bf16(a, b, dims):
    return lax.dot_general(a.astype(BF16), b.astype(BF16), dims, preferred_element_type=F32)


def _mm_nt(a, b):
    return _dot_bf16(a, b, _NT)


def _mm_chain(a, b):
    return _dot_bf16(a, b, _NN)


def _mm_state(a, b):
    return _dot_bf16(a, b, _NN)


def _mm_tn_state(a, b):
    return _dot_bf16(a, b, _TN)


def _delta_prep_kernel(q_ref, k_ref, v_ref, bg_ref, uv_ref, wk_ref, qd_ref, kd_ref, qk_ref, gl_ref):
    sb = q_ref.shape[0]
    ncb = q_ref.shape[1] // CHUNK
    row = lax.broadcasted_iota(jnp.int32, (CHUNK, CHUNK), 0)
    col = lax.broadcasted_iota(jnp.int32, (CHUNK, CHUNK), 1)
    incl = row >= col
    strict = row > col
    tri = incl.astype(F32)
    xs, pw = [], []
    for s in range(sb):
        for c in range(ncb):
            rows = slice(c * CHUNK, (c + 1) * CHUNK)
            bg = bg_ref[s, rows, :]
            gc_all = _mm_f32(tri, bg)
            gc_rows = gc_all.T
            gl_ref[s, c * SUBLANES:(c + 1) * SUBLANES, :] = gc_all[CHUNK - SUBLANES:, :]
            for hh in range(DN_HEADS):
                sl = slice(hh * DN_DK, (hh + 1) * DN_DK)
                q = q_ref[s, rows, sl] * (DN_DK ** -0.5)
                k = k_ref[s, rows, sl]
                beta = bg[:, hh:hh + 1]
                gcol = gc_all[:, DN_HEADS + hh:DN_HEADS + hh + 1]
                grow = gc_rows[DN_HEADS + hh:DN_HEADS + hh + 1, :]
                decay = jnp.exp(jnp.where(incl, gcol - grow, -jnp.inf))
                egc = jnp.exp(gcol)
                kb = k * beta
                pw.append(-jnp.where(strict, _mm_nt(kb, k) * decay, 0.0))
                xs.append(jnp.concatenate([v_ref[s, rows, sl] * beta, kb * egc], axis=1))
                qk_ref[s, rows, hh * CHUNK:(hh + 1) * CHUNK] = _mm_nt(q, k) * decay
                gl = gcol[CHUNK - 1:CHUNK, :]
                qd_ref[s, rows, sl] = q * egc
                kd_ref[s, rows, sl] = k * jnp.exp(gl - gcol)
    span = 1
    while True:
        xs = [x + _mm_chain(p, x) for p, x in zip(pw, xs)]
        span *= 2
        if span >= CHUNK:
            break
        pw = [_mm_chain(p, p) for p in pw]
    i = 0
    for s in range(sb):
        for c in range(ncb):
            rows = slice(c * CHUNK, (c + 1) * CHUNK)
            for hh in range(DN_HEADS):
                sl = slice(hh * DN_DK, (hh + 1) * DN_DK)
                uv_ref[s, rows, sl] = xs[i][:, :DN_DV]
                wk_ref[s, rows, sl] = xs[i][:, DN_DV:]
                i += 1


def _delta_scan_kernel(uv_ref, wk_ref, qd_ref, kd_ref, qk_ref, gl_ref, z_ref, s0_ref, g_out_ref, yb_ref, s_out_ref, s_scr):
    ci = pl.program_id(1)
    bb = uv_ref.shape[0]

    @pl.when(ci == 0)
    def _():
        s_scr[...] = s0_ref[...]

    g_out = g_out_ref[...]
    probs = [(b, hh) for b in range(bb) for hh in range(DN_HEADS)]
    lanes = lambda hh: slice(hh * DN_DK, (hh + 1) * DN_DK)
    s_old = [s_scr[b, hh] for b, hh in probs]
    v_new = [uv_ref[b, :, lanes(hh)] - _mm_state(wk_ref[b, :, lanes(hh)], s) for (b, hh), s in zip(probs, s_old)]
    o_state = [_mm_state(qd_ref[b, :, lanes(hh)], s) for (b, hh), s in zip(probs, s_old)]
    for (b, hh), s, vn, os_ in zip(probs, s_old, v_new, o_state):
        o = os_ + _mm_state(qk_ref[b, :, hh * CHUNK:(hh + 1) * CHUNK], vn)
        egl = jnp.exp(gl_ref[b, SUBLANES - 1:SUBLANES, DN_HEADS + hh:DN_HEADS + hh + 1])
        s_new = s * egl + _mm_tn_state(kd_ref[b, :, lanes(hh)], vn)
        s_scr[b, hh] = s_new
        s_out_ref[b, hh] = s_new
        yb_ref[b, :, lanes(hh)] = _rms(o, g_out) * _silu(z_ref[b, :, lanes(hh)])


DELTA_PREP_CHUNKS = 4
DELTA_SCAN_SEQS = 8


def _delta(q, k, v, z, bg, s0, g_out):
    b, l, _ = q.shape
    nc = l // CHUNK
    ncb = min(DELTA_PREP_CHUNKS, nc)
    sb = min(b, DELTA_PREP_CHUNKS // ncb)
    rows = ncb * CHUNK
    tok = lambda width: pl.BlockSpec((sb, rows, width), lambda i, j: (i, j, 0))
    glspec = pl.BlockSpec((sb, ncb * SUBLANES, LANES), lambda i, j: (i, j, 0))
    wide = lambda width: jax.ShapeDtypeStruct((b, l, width), F32)
    uv, wk, qd, kd, qk, gl = pl.pallas_call(
        _delta_prep_kernel,
        out_shape=(wide(DN_VW), wide(DN_QK), wide(DN_QK), wide(DN_QK), wide(DN_HEADS * CHUNK),
                   jax.ShapeDtypeStruct((b, nc * SUBLANES, LANES), F32)),
        grid=(b // sb, nc // ncb),
        in_specs=[tok(DN_QK), tok(DN_QK), tok(DN_VW), tok(LANES)],
        out_specs=(tok(DN_VW), tok(DN_QK), tok(DN_QK), tok(DN_QK), tok(DN_HEADS * CHUNK), glspec),
        compiler_params=pltpu.CompilerParams(dimension_semantics=("arbitrary", "arbitrary"), vmem_limit_bytes=VMEM_LIMIT),
        name="delta_prep",
    )(q, k, v, bg)

    bb = min(DELTA_SCAN_SEQS, b)
    ctok = lambda width: pl.BlockSpec((bb, CHUNK, width), lambda i, j: (i, j, 0))
    st = pl.BlockSpec((bb, DN_HEADS, DN_DK, DN_DV), lambda i, j: (i, 0, 0, 0))
    return pl.pallas_call(
        _delta_scan_kernel,
        out_shape=(wide(DN_VW), jax.ShapeDtypeStruct((b, DN_HEADS, DN_DK, DN_DV), F32)),
        grid=(b // bb, nc),
        in_specs=[ctok(DN_VW), ctok(DN_QK), ctok(DN_QK), ctok(DN_QK), ctok(DN_HEADS * CHUNK),
                  pl.BlockSpec((bb, SUBLANES, LANES), lambda i, j: (i, j, 0)), ctok(DN_VW), st,
                  pl.BlockSpec((1, DN_DV), lambda i, j: (0, 0))],
        out_specs=(ctok(DN_VW), st),
        scratch_shapes=[pltpu.VMEM((bb, DN_HEADS, DN_DK, DN_DV), F32)],
        compiler_params=pltpu.CompilerParams(dimension_semantics=("arbitrary", "arbitrary"), vmem_limit_bytes=VMEM_LIMIT),
        name="delta_scan",
    )(uv, wk, qd, kd, qk, gl, z, s0, g_out)


def _merge_kernel(x_ref, ya_ref, yb_ref, graw_ref, w_up_pool_ref, w_up_dn_ref, w_out_ref, g_ffn_ref, x2_ref, xn_ref):
    graw = graw_ref[...]
    ga = _sigmoid(graw[:, :D_MODEL])
    gb = _sigmoid(graw[:, D_MODEL:])
    merged = ga * _mm(ya_ref[...], w_up_pool_ref[...]) + gb * _mm(yb_ref[...], w_up_dn_ref[...])
    x2 = x_ref[...] + _mm(merged, w_out_ref[...])
    x2_ref[...] = x2
    xn_ref[...] = _rms(x2, g_ffn_ref[...])


def _merge(x, ya, yb, graw, w_up_pool, w_up_dn, w_out, g_ffn, tm):
    t = x.shape[0]
    tok = lambda width: pl.BlockSpec((tm, width), lambda i: (i, 0))
    full = lambda a: pl.BlockSpec(a.shape, lambda i: (0,) * a.ndim)
    weights = (w_up_pool, w_up_dn, w_out, g_ffn)
    return pl.pallas_call(
        _merge_kernel,
        out_shape=(jax.ShapeDtypeStruct((t, D_MODEL), F32), jax.ShapeDtypeStruct((t, D_MODEL), F32)),
        grid=(t // tm,),
        in_specs=[tok(D_MODEL), tok(D_POOL), tok(DN_VW), tok(2 * D_MODEL)] + [full(a) for a in weights],
        out_specs=(tok(D_MODEL), tok(D_MODEL)),
        compiler_params=pltpu.CompilerParams(dimension_semantics=("arbitrary",), vmem_limit_bytes=VMEM_LIMIT),
        name="branch_merge",
    )(x, ya, yb, graw, *weights)


def _top16_rows(s, ids=None):
    if ids is None:
        ids = lax.broadcasted_iota(jnp.int32, s.shape, 0)
    ids = ids.astype(F32)
    vals, idxs = [], []
    for _ in range(PEER_TOPK):
        m = jnp.max(s, axis=0, keepdims=True)
        idx = jnp.min(jnp.where(s == m, ids, jnp.inf), axis=0, keepdims=True)
        vals.append(m)
        idxs.append(idx)
        s = jnp.where(ids == idx, -jnp.inf, s)
    return jnp.concatenate(vals, axis=0), jnp.concatenate(idxs, axis=0).astype(jnp.int32)


def _pair_candidates(v1, v2):
    tokens = v1.shape[1]
    sub = lax.broadcasted_iota(jnp.int32, (SUBLANES, tokens), 0)
    vals = [v1[0:1] + v2[0:SUBLANES], v1[0:1] + v2[SUBLANES:]]
    ids = [sub, sub + SUBLANES]
    for a in range(1, SUBLANES):
        vals.append(v1[a:a + 1] + v2[0:SUBLANES])
        ids.append(sub + a * PEER_TOPK)
    vals.append(v1[SUBLANES:] + v2[0:1])
    ids.append((sub + SUBLANES) * PEER_TOPK)
    return jnp.concatenate(vals, axis=0), jnp.concatenate(ids, axis=0)


def _take_rows(table, idx):
    out = jnp.zeros_like(table)
    for a in range(PEER_TOPK):
        out = jnp.where(idx == a, table[a:a + 1, :], out)
    return out


def _route_kernel(xn_ref, wq_ref, keys_ref, experts_ref, gates_ref):
    half = PEER_DQ // 2
    q = jnp.dot(xn_ref[...].astype(BF16), wq_ref[...], preferred_element_type=F32).astype(BF16)
    experts, gates = [], []
    for hh in range(PEER_HEADS):
        q1 = q[:, hh * PEER_DQ:hh * PEER_DQ + half]
        q2 = q[:, hh * PEER_DQ + half:(hh + 1) * PEER_DQ]
        nt = (((1,), (1,)), ((), ()))
        s1 = lax.dot_general(keys_ref[0, hh], q1, nt, preferred_element_type=F32)
        s2 = lax.dot_general(keys_ref[1, hh], q2, nt, preferred_element_type=F32)
        v1, i1 = _top16_rows(s1)
        v2, i2 = _top16_rows(s2)
        cv, ci = _top16_rows(*_pair_candidates(v1, v2))
        e1 = _take_rows(i1, ci // PEER_TOPK)
        e2 = _take_rows(i2, ci % PEER_TOPK)
        experts.append(e1 * PEER_KEYS + e2)
        ex = jnp.exp(cv - cv[0:1, :])
        gates.append(ex / jnp.sum(ex, axis=0, keepdims=True))
    experts_ref[...] = jnp.concatenate(experts, axis=0).T
    gates_ref[...] = jnp.concatenate(gates, axis=0).T


def _route(xn, wq, keys, tm):
    t = xn.shape[0]
    return pl.pallas_call(
        _route_kernel,
        out_shape=(jax.ShapeDtypeStruct((t, PEER_PICKS), jnp.int32), jax.ShapeDtypeStruct((t, PEER_PICKS), F32)),
        grid=(t // tm,),
        in_specs=[pl.BlockSpec((tm, D_MODEL), lambda i: (i, 0)),
                  pl.BlockSpec(wq.shape, lambda i: (0, 0)),
                  pl.BlockSpec(keys.shape, lambda i: (0, 0, 0, 0))],
        out_specs=(pl.BlockSpec((tm, PEER_PICKS), lambda i: (i, 0)), pl.BlockSpec((tm, PEER_PICKS), lambda i: (i, 0))),
        compiler_params=pltpu.CompilerParams(dimension_semantics=("arbitrary",), vmem_limit_bytes=VMEM_LIMIT),
        name="peer_route",
    )(xn, wq, keys)


EXPERT_TOKENS = 16
EXPERT_ISSUE_TOKENS = 12


def _experts_kernel(idx_ref, idx_next_ref, gates_ref, xn_ref, x2_ref, g_final_ref, tab_ref, out_ref, buf_a, buf_b, cb, sem):
    bufs = (buf_a, buf_b)
    i = pl.program_id(0)
    n = pl.num_programs(0)
    tb = EXPERT_TOKENS
    lane = lax.broadcasted_iota(jnp.int32, (PEER_HEADS, PEER_TOPK), 1)
    g_final = g_final_ref[...]

    def issue_picks(ref, sl, lo, hi):
        for f in range(lo, min(hi, tb * PEER_PICKS)):
            t, h, k = f // PEER_PICKS, (f // PEER_TOPK) % PEER_HEADS, f % PEER_TOPK
            e = ref[0, t, h * PEER_TOPK + k]
            pltpu.async_copy(tab_ref.at[e], bufs[sl].at[t, k, h], sem.at[sl], priority=k % 2)

    def wait_slot(sl):
        for t in range(tb):
            pltpu.make_async_copy(bufs[sl].at[t], bufs[sl].at[t], sem.at[sl]).wait()

    def gate_coefficients(t, sl):
        x = xn_ref[t]
        sub = lax.broadcasted_iota(jnp.int32, (PEER_HEADS, PEER_TOPK), 0)
        act = jnp.zeros((PEER_HEADS, PEER_TOPK), F32)
        for h in range(PEER_HEADS):
            part = jnp.zeros((SUBLANES, PEER_TOPK), F32)
            for k in range(PEER_TOPK):
                u = lax.bitcast_convert_type(bufs[sl][t, k, h] << 16, F32)
                part = jnp.where(lane == k, jnp.sum(u * x, axis=-1, keepdims=True), part)
            act = jnp.where(sub == h, jnp.sum(part, axis=0, keepdims=True), act)
        coef = gates_ref[t] * (0.5 * act * (1.0 + lax.erf(act * (0.5 ** 0.5))))
        for k in range(PEER_TOPK):
            cb[t, k] = jnp.broadcast_to(coef[:, k:k + 1], (PEER_HEADS, LANES))

    def mix_token(t, sl):
        accs = [jnp.zeros((SUBLANES, LANES), F32) for _ in range(4)]
        for k in range(PEER_TOPK):
            for h in range(PEER_HEADS):
                c = jnp.broadcast_to(cb[t, k, h:h + 1, :], (SUBLANES, LANES))
                v = lax.bitcast_convert_type(bufs[sl][t, k, h] & jnp.uint32(0xFFFF0000), F32)
                accs[h % 4] = accs[h % 4] + c * v
        z = x2_ref[t] + ((accs[0] + accs[1]) + (accs[2] + accs[3]))
        ms = jnp.sum(jnp.sum(z * z, axis=-1, keepdims=True), axis=0, keepdims=True) * (1.0 / D_MODEL)
        out_ref[t] = z * lax.rsqrt(ms + EPS) * g_final

    @pl.when(i == 0)
    def _():
        issue_picks(idx_ref, 0, 0, tb * PEER_PICKS)

    per_token = -(-tb * PEER_PICKS // EXPERT_ISSUE_TOKENS)

    def step(sl):
        wait_slot(sl)
        gate_coefficients(0, sl)
        for t in range(tb):
            @pl.when(i + t >= 0)
            def _():
                issue_picks(idx_next_ref, 1 - sl, t * per_token, (t + 1) * per_token)
                if t + 1 < tb:
                    gate_coefficients(t + 1, sl)
                mix_token(t, sl)

        @pl.when(i == n - 1)
        def _():
            wait_slot(1 - sl)

    @pl.when(i % 2 == 0)
    def _():
        step(0)

    @pl.when(i % 2 == 1)
    def _():
        step(1)


def _experts(experts, gates, xn, x2, g_final, table):
    t = xn.shape[0]
    tb = EXPERT_TOKENS
    nb = t // tb
    idx = experts.reshape(nb, tb, PEER_PICKS)
    gates = gates.reshape(t, PEER_HEADS, PEER_TOPK)
    tiles = lambda a: a.reshape(a.shape[0], SUBLANES, LANES)
    tok = pl.BlockSpec((tb, SUBLANES, LANES), lambda i: (i, 0, 0))
    out = pl.pallas_call(
        _experts_kernel,
        out_shape=jax.ShapeDtypeStruct((t, SUBLANES, LANES), F32),
        grid=(nb,),
        in_specs=[pl.BlockSpec((1, tb, PEER_PICKS), lambda i: (i, 0, 0), memory_space=pltpu.SMEM),
                  pl.BlockSpec((1, tb, PEER_PICKS), lambda i: (jnp.minimum(i + 1, nb - 1), 0, 0), memory_space=pltpu.SMEM),
                  pl.BlockSpec((tb, PEER_HEADS, PEER_TOPK), lambda i: (i, 0, 0)), tok, tok,
                  pl.BlockSpec((SUBLANES, LANES), lambda i: (0, 0)),
                  pl.BlockSpec(memory_space=pl.ANY)],
        out_specs=tok,
        scratch_shapes=[pltpu.VMEM((tb, PEER_TOPK, PEER_HEADS, SUBLANES, LANES), jnp.uint32),
                        pltpu.VMEM((tb, PEER_TOPK, PEER_HEADS, SUBLANES, LANES), jnp.uint32),
                        pltpu.VMEM((tb, PEER_TOPK, PEER_HEADS, LANES), F32),
                        pltpu.SemaphoreType.DMA((2,))],
        compiler_params=pltpu.CompilerParams(dimension_semantics=("arbitrary",), vmem_limit_bytes=VMEM_LIMIT),
        name="peer_experts",
    )(idx, idx, gates, tiles(xn), tiles(x2), g_final.reshape(SUBLANES, LANES), table)
    return out.reshape(t, D_MODEL)


def _pack_rows(u, v):
    half = lambda a: lax.bitcast_convert_type(a.astype(BF16), jnp.uint16).astype(jnp.uint32)
    return (half(u) | (half(v) << 16)).reshape(-1, SUBLANES, LANES)


def _pad_rows_front(a, rows):
    return jnp.pad(a, ((0, 0), (rows - a.shape[1], 0), (0, 0)))


MERGE_ROWS = 512
ROUTE_ROWS = 256
FRONT_ROWS = 512


def _group(x, pool_prev, conv_prev, s_prev, pos0, p):
    b, l, _ = x.shape
    tl = min(FRONT_ROWS, l)
    sb = min(b, FRONT_ROWS // tl)
    tm_merge = min(MERGE_ROWS, b * l)
    tm_route = min(ROUTE_ROWS, b * l)
    ya, q, k, v, z, graw, bg, pool_tail, conv_tail = _front(
        x, _pad_rows_front(pool_prev, POOL_PREFIX_ROWS), _pad_rows_front(conv_prev, CONV_PREFIX_ROWS), pos0, sb, tl,
        p["g_mix"], p["w_u"], p["w_qkv"], p["w_z"], p["w_ba"], p["w_g"], p["w_grp"], p["pool_scale"], p["w_conv8"],
        p["a_log_pad"], p["dt_pad"])
    yb, s_new = _delta(q, k, v, z, bg, s_prev, p["g_dn_out"])
    t = b * l
    flat = lambda a: a.reshape(t, a.shape[-1])
    x2, xn = _merge(flat(x), flat(ya), flat(yb), flat(graw), p["w_up_pool"], p["w_up_dn"], p["w_out"], p["g_ffn"], tm_merge)
    experts, gates = _route(xn, p["w_peer_q"], p["peer_keys"], tm_route)
    y = _experts(experts, gates, xn, x2, p["g_final"], p["peer_table"])
    return (y.reshape(b, l, D_MODEL), pool_tail[:, POOL_PREFIX_ROWS - POOL_STATE:],
            conv_tail[:, CONV_PREFIX_ROWS - (CONV_W - 1):], s_new)


def kernel(x_prompt, x_sample, cache_pool, state_dn_conv, state_dn, g_mix, w_in, w_pool_grp, pool_scale, w_conv, a_log,
           dt_bias, g_dn_out, w_up_pool, w_up_dn, w_out, g_ffn, w_peer_q, peer_sub_keys, peer_u, peer_v, g_final):
    depth = w_in.shape[0]
    assert depth == 1
    bp = x_prompt.shape[0]
    lane_pad = lambda a, off: jnp.pad(a.astype(F32)[None, :], ((0, 0), (off, LANES - off - a.shape[0])))
    w = w_in[0]
    params = {
        "g_mix": g_mix[0][None, :],
        "w_u": w[:, OFF_U:OFF_QKV].astype(BF16),
        "w_qkv": w[:, OFF_QKV:OFF_Z].astype(BF16),
        "w_z": w[:, OFF_Z:OFF_B].astype(BF16),
        "w_ba": jnp.pad(w[:, OFF_B:OFF_G], ((0, 0), (0, LANES - 2 * DN_HEADS))).astype(BF16),
        "w_g": w[:, OFF_G:].astype(BF16),
        "w_grp": w_pool_grp[0].astype(BF16),
        "pool_scale": pool_scale[0][None, :],
        "w_conv8": jnp.pad(w_conv[0], ((0, SUBLANES - CONV_W), (0, 0))),
        "a_log_pad": lane_pad(a_log[0], DN_HEADS),
        "dt_pad": lane_pad(dt_bias[0], DN_HEADS),
        "g_dn_out": g_dn_out[0][None, :],
        "w_up_pool": w_up_pool[0].astype(BF16),
        "w_up_dn": w_up_dn[0].astype(BF16),
        "w_out": w_out[0].astype(BF16),
        "g_ffn": g_ffn[0][None, :],
        "w_peer_q": w_peer_q[0].reshape(D_MODEL, PEER_HEADS * PEER_DQ).astype(BF16),
        "peer_keys": peer_sub_keys[0].astype(BF16),
        "peer_table": _pack_rows(peer_u[0], peer_v[0]),
        "g_final": g_final[None, :],
    }
    zeros = lambda *shape: jnp.zeros(shape, F32)
    yp, pool_p, conv_p, dn_p = _group(
        x_prompt, zeros(bp, POOL_STATE, D_POOL), zeros(bp, CONV_W - 1, CONV_CH), zeros(bp, DN_HEADS, DN_DK, DN_DV),
        0, params)
    ys, pool_s, conv_s, dn_s = _group(
        x_sample, cache_pool[0], state_dn_conv[0], state_dn[0].astype(F32),
        PAST_LEN, params)
    return (yp, ys, pool_p[None], conv_p[None], dn_p[None].astype(state_dn.dtype),
            pool_s[None], conv_s[None], dn_s[None].astype(state_dn.dtype))
```

```python
import functools

import jax
import jax.numpy as jnp
from jax import lax
from jax.experimental import pallas as pl
from jax.experimental.pallas import tpu as pltpu

D_MODEL = 1024
CHUNK = 64
D_POOL = 512
POOL_WINDOWS = (2, 4, 8, 16)
POOL_GROUP = 128
POOL_STATE = 15
DN_HEADS = 4
DN_DK = 128
DN_DV = 128
DN_QK = DN_HEADS * DN_DK
DN_VW = DN_HEADS * DN_DV
CONV_W = 4
CONV_CH = 2 * DN_QK + DN_VW
OFF_U = 0
OFF_QKV = OFF_U + D_POOL
OFF_Z = OFF_QKV + CONV_CH
OFF_B = OFF_Z + DN_VW
OFF_A = OFF_B + DN_HEADS
OFF_G = OFF_A + DN_HEADS
PEER_HEADS = 8
PEER_KEYS = 128
PEER_DQ = 256
PEER_TOPK = 16
PEER_PICKS = PEER_HEADS * PEER_TOPK
EPS = 1e-6
PAST_LEN = 4096

LANES = 128
SUBLANES = 8
POOL_PREFIX_ROWS = 16
CONV_PREFIX_ROWS = 8
VMEM_LIMIT = 56 * 1024 * 1024

F32 = jnp.float32
BF16 = jnp.bfloat16
HIGHEST = lax.Precision.HIGHEST


def _mm(a, b):
    return jnp.dot(a.astype(BF16), b.astype(BF16), preferred_element_type=F32)


def _mm_f32(a, b):
    return jnp.dot(a, b, precision=HIGHEST, preferred_element_type=F32)


def _mm_nt_f32(a, b):
    return lax.dot_general(a, b, (((1,), (1,)), ((), ())), precision=HIGHEST, preferred_element_type=F32)


def _mm_tn_f32(a, b):
    return lax.dot_general(a, b, (((0,), (0,)), ((), ())), precision=HIGHEST, preferred_element_type=F32)


def _sigmoid(x):
    return 1.0 / (1.0 + jnp.exp(-x))


def _silu(x):
    return x * _sigmoid(x)


def _softplus(x):
    return jnp.maximum(x, 0.0) + jnp.log1p(jnp.exp(-jnp.abs(x)))


def _rms(x, g):
    return x * lax.rsqrt(jnp.mean(x * x, axis=-1, keepdims=True) + EPS) * g


def _front_kernel(pos0, x_ref, pool_pre_ref, conv_pre_ref, g_mix_ref, w_u_ref, w_qkv_ref, w_z_ref, w_ba_ref, w_g_ref,
                  w_grp_ref, pool_scale_ref, w_conv_ref, a_log_ref, dt_ref,
                  ya_ref, q_ref, k_ref, v_ref, z_ref, graw_ref, bg_ref, pool_tail_ref, conv_tail_ref,
                  carry_u, carry_c):
    sb, tl, _ = x_ref.shape
    li = pl.program_id(1)

    @pl.when(li == 0)
    def _():
        carry_u[...] = pool_pre_ref[...]
        carry_c[...] = conv_pre_ref[...]

    x = x_ref[...].reshape(sb * tl, D_MODEL)
    h = _rms(x, g_mix_ref[...]).astype(BF16)
    u = jnp.dot(h, w_u_ref[...], preferred_element_type=F32)
    qkv = jnp.dot(h, w_qkv_ref[...], preferred_element_type=F32)
    z_ref[...] = jnp.dot(h, w_z_ref[...], preferred_element_type=F32).reshape(sb, tl, DN_VW)
    graw_ref[...] = jnp.dot(h, w_g_ref[...], preferred_element_type=F32).reshape(sb, tl, 2 * D_MODEL)
    ba = jnp.dot(h, w_ba_ref[...], preferred_element_type=F32)
    lane = lax.broadcasted_iota(jnp.int32, ba.shape, 1)
    beta = _sigmoid(ba)
    g = -jnp.exp(a_log_ref[...]) * _softplus(ba + dt_ref[...])
    bg_ref[...] = jnp.where(lane < DN_HEADS, beta, g).reshape(sb, tl, LANES)

    row = lax.broadcasted_iota(jnp.int32, (tl, POOL_GROUP), 0)
    pos1 = pos0 + li * tl + row + 1
    w_conv = w_conv_ref[...]

    for s in range(sb):
        u_s = u[s * tl:(s + 1) * tl]
        ext = jnp.concatenate([carry_u[s], u_s], axis=0)
        mixed = []
        for gi, w in enumerate(POOL_WINDOWS):
            acc = ext[:, gi * POOL_GROUP:(gi + 1) * POOL_GROUP]
            span = 1
            while span < w:
                acc = acc + pltpu.roll(acc, span, axis=0)
                span *= 2
            win = acc[POOL_PREFIX_ROWS:]
            cnt = jnp.minimum(pos1, w).astype(F32)
            pooled = win / cnt - u_s[:, gi * POOL_GROUP:(gi + 1) * POOL_GROUP]
            mixed.append(_mm(pooled, w_grp_ref[gi]))
        ya_ref[s] = jnp.concatenate(mixed, axis=1) * pool_scale_ref[...]
        pool_tail_ref[s] = ext[tl:]
        carry_u[s] = ext[tl:]

        c_s = qkv[s * tl:(s + 1) * tl]
        cext = jnp.concatenate([carry_c[s], c_s], axis=0)
        y = c_s * w_conv[CONV_W - 1:CONV_W]
        for j in range(1, CONV_W):
            y = y + pltpu.roll(cext, j, axis=0)[CONV_PREFIX_ROWS:] * w_conv[CONV_W - 1 - j:CONV_W - j]
        y = _silu(y)
        conv_tail_ref[s] = cext[tl:]
        carry_c[s] = cext[tl:]
        for hh in range(DN_HEADS):
            sl = slice(hh * DN_DK, (hh + 1) * DN_DK)
            qh = y[:, sl]
            q_ref[s, :, sl] = qh * lax.rsqrt(jnp.sum(qh * qh, axis=-1, keepdims=True) + EPS)
            kh = y[:, DN_QK + hh * DN_DK:DN_QK + (hh + 1) * DN_DK]
            k_ref[s, :, sl] = kh * lax.rsqrt(jnp.sum(kh * kh, axis=-1, keepdims=True) + EPS)
        v_ref[s] = y[:, 2 * DN_QK:]


def _front(x, pool_pre, conv_pre, pos0, sb, tl, g_mix, w_u, w_qkv, w_z, w_ba, w_g, w_grp, pool_scale, w_conv8,
           a_log_pad, dt_pad):
    b, l, _ = x.shape
    grid = (b // sb, l // tl)
    tok = lambda width: pl.BlockSpec((sb, tl, width), lambda i, j: (i, j, 0))
    seq = lambda rows, width: pl.BlockSpec((sb, rows, width), lambda i, j: (i, 0, 0))
    full = lambda a: pl.BlockSpec(a.shape, lambda i, j: (0,) * a.ndim)
    weights = (g_mix, w_u, w_qkv, w_z, w_ba, w_g, w_grp, pool_scale, w_conv8, a_log_pad, dt_pad)
    out_shape = (
        jax.ShapeDtypeStruct((b, l, D_POOL), F32),
        jax.ShapeDtypeStruct((b, l, DN_QK), F32),
        jax.ShapeDtypeStruct((b, l, DN_QK), F32),
        jax.ShapeDtypeStruct((b, l, DN_VW), F32),
        jax.ShapeDtypeStruct((b, l, DN_VW), F32),
        jax.ShapeDtypeStruct((b, l, 2 * D_MODEL), F32),
        jax.ShapeDtypeStruct((b, l, LANES), F32),
        jax.ShapeDtypeStruct((b, POOL_PREFIX_ROWS, D_POOL), F32),
        jax.ShapeDtypeStruct((b, CONV_PREFIX_ROWS, CONV_CH), F32),
    )
    return pl.pallas_call(
        functools.partial(_front_kernel, pos0),
        out_shape=out_shape,
        grid=grid,
        in_specs=[tok(D_MODEL), seq(POOL_PREFIX_ROWS, D_POOL), seq(CONV_PREFIX_ROWS, CONV_CH)] + [full(a) for a in weights],
        out_specs=(tok(D_POOL), tok(DN_QK), tok(DN_QK), tok(DN_VW), tok(DN_VW), tok(2 * D_MODEL), tok(LANES),
                   seq(POOL_PREFIX_ROWS, D_POOL), seq(CONV_PREFIX_ROWS, CONV_CH)),
        scratch_shapes=[pltpu.VMEM((sb, POOL_PREFIX_ROWS, D_POOL), F32), pltpu.VMEM((sb, CONV_PREFIX_ROWS, CONV_CH), F32)],
        compiler_params=pltpu.CompilerParams(dimension_semantics=("arbitrary", "arbitrary"), vmem_limit_bytes=VMEM_LIMIT),
        name="mixer_front",
    )(x, pool_pre, conv_pre, *weights)


_NN = (((1,), (0,)), ((), ()))
_NT = (((1,), (1,)), ((), ()))
_TN = (((0,), (0,)), ((), ()))


def _dot_bf16(a, b, dims):
    return lax.dot_general(a.astype(BF16), b.astype(BF16), dims, preferred_element_type=F32)


def _mm_nt(a, b):
    return _dot_bf16(a, b, _NT)


def _mm_chain(a, b):
    return _dot_bf16(a, b, _NN)


def _mm_state(a, b):
    return _dot_bf16(a, b, _NN)


def _mm_tn_state(a, b):
    return _dot_bf16(a, b, _TN)


def _delta_prep_kernel(q_ref, k_ref, v_ref, bg_ref, uv_ref, wk_ref, qd_ref, kd_ref, qk_ref, gl_ref):
    sb = q_ref.shape[0]
    ncb = q_ref.shape[1] // CHUNK
    row = lax.broadcasted_iota(jnp.int32, (CHUNK, CHUNK), 0)
    col = lax.broadcasted_iota(jnp.int32, (CHUNK, CHUNK), 1)
    incl = row >= col
    strict = row > col
    tri = incl.astype(F32)
    xs, pw = [], []
    for s in range(sb):
        for c in range(ncb):
            rows = slice(c * CHUNK, (c + 1) * CHUNK)
            bg = bg_ref[s, rows, :]
            gc_all = _mm_f32(tri, bg)
            gc_rows = gc_all.T
            gl_ref[s, c * SUBLANES:(c + 1) * SUBLANES, :] = gc_all[CHUNK - SUBLANES:, :]
            for hh in range(DN_HEADS):
                sl = slice(hh * DN_DK, (hh + 1) * DN_DK)
                q = q_ref[s, rows, sl] * (DN_DK ** -0.5)
                k = k_ref[s, rows, sl]
                beta = bg[:, hh:hh + 1]
                gcol = gc_all[:, DN_HEADS + hh:DN_HEADS + hh + 1]
                grow = gc_rows[DN_HEADS + hh:DN_HEADS + hh + 1, :]
                decay = jnp.exp(jnp.where(incl, gcol - grow, -jnp.inf))
                egc = jnp.exp(gcol)
                kb = k * beta
                pw.append(-jnp.where(strict, _mm_nt(kb, k) * decay, 0.0))
                xs.append(jnp.concatenate([v_ref[s, rows, sl] * beta, kb * egc], axis=1))
                qk_ref[s, rows, hh * CHUNK:(hh + 1) * CHUNK] = _mm_nt(q, k) * decay
                gl = gcol[CHUNK - 1:CHUNK, :]
                qd_ref[s, rows, sl] = q * egc
                kd_ref[s, rows, sl] = k * jnp.exp(gl - gcol)
    span = 1
    while True:
        xs = [x + _mm_chain(p, x) for p, x in zip(pw, xs)]
        span *= 2
        if span >= CHUNK:
            break
        pw = [_mm_chain(p, p) for p in pw]
    i = 0
    for s in range(sb):
        for c in range(ncb):
            rows = slice(c * CHUNK, (c + 1) * CHUNK)
            for hh in range(DN_HEADS):
                sl = slice(hh * DN_DK, (hh + 1) * DN_DK)
                uv_ref[s, rows, sl] = xs[i][:, :DN_DV]
                wk_ref[s, rows, sl] = xs[i][:, DN_DV:]
                i += 1


def _delta_scan_kernel(uv_ref, wk_ref, qd_ref, kd_ref, qk_ref, gl_ref, z_ref, s0_ref, g_out_ref, yb_ref, s_out_ref, s_scr):
    ci = pl.program_id(1)
    bb = uv_ref.shape[0]

    @pl.when(ci == 0)
    def _():
        s_scr[...] = s0_ref[...]

    g_out = g_out_ref[...]
    probs = [(b, hh) for b in range(bb) for hh in range(DN_HEADS)]
    lanes = lambda hh: slice(hh * DN_DK, (hh + 1) * DN_DK)
    s_old = [s_scr[b, hh] for b, hh in probs]
    v_new = [uv_ref[b, :, lanes(hh)] - _mm_state(wk_ref[b, :, lanes(hh)], s) for (b, hh), s in zip(probs, s_old)]
    o_state = [_mm_state(qd_ref[b, :, lanes(hh)], s) for (b, hh), s in zip(probs, s_old)]
    for (b, hh), s, vn, os_ in zip(probs, s_old, v_new, o_state):
        o = os_ + _mm_state(qk_ref[b, :, hh * CHUNK:(hh + 1) * CHUNK], vn)
        egl = jnp.exp(gl_ref[b, SUBLANES - 1:SUBLANES, DN_HEADS + hh:DN_HEADS + hh + 1])
        s_new = s * egl + _mm_tn_state(kd_ref[b, :, lanes(hh)], vn)
        s_scr[b, hh] = s_new
        s_out_ref[b, hh] = s_new
        yb_ref[b, :, lanes(hh)] = _rms(o, g_out) * _silu(z_ref[b, :, lanes(hh)])


DELTA_PREP_CHUNKS = 4
DELTA_SCAN_SEQS = 8


def _delta(q, k, v, z, bg, s0, g_out):
    b, l, _ = q.shape
    nc = l // CHUNK
    ncb = min(DELTA_PREP_CHUNKS, nc)
    sb = min(b, DELTA_PREP_CHUNKS // ncb)
    rows = ncb * CHUNK
    tok = lambda width: pl.BlockSpec((sb, rows, width), lambda i, j: (i, j, 0))
    glspec = pl.BlockSpec((sb, ncb * SUBLANES, LANES), lambda i, j: (i, j, 0))
    wide = lambda width: jax.ShapeDtypeStruct((b, l, width), F32)
    uv, wk, qd, kd, qk, gl = pl.pallas_call(
        _delta_prep_kernel,
        out_shape=(wide(DN_VW), wide(DN_QK), wide(DN_QK), wide(DN_QK), wide(DN_HEADS * CHUNK),
                   jax.ShapeDtypeStruct((b, nc * SUBLANES, LANES), F32)),
        grid=(b // sb, nc // ncb),
        in_specs=[tok(DN_QK), tok(DN_QK), tok(DN_VW), tok(LANES)],
        out_specs=(tok(DN_VW), tok(DN_QK), tok(DN_QK), tok(DN_QK), tok(DN_HEADS * CHUNK), glspec),
        compiler_params=pltpu.CompilerParams(dimension_semantics=("arbitrary", "arbitrary"), vmem_limit_bytes=VMEM_LIMIT),
        name="delta_prep",
    )(q, k, v, bg)

    bb = min(DELTA_SCAN_SEQS, b)
    ctok = lambda width: pl.BlockSpec((bb, CHUNK, width), lambda i, j: (i, j, 0))
    st = pl.BlockSpec((bb, DN_HEADS, DN_DK, DN_DV), lambda i, j: (i, 0, 0, 0))
    return pl.pallas_call(
        _delta_scan_kernel,
        out_shape=(wide(DN_VW), jax.ShapeDtypeStruct((b, DN_HEADS, DN_DK, DN_DV), F32)),
        grid=(b // bb, nc),
        in_specs=[ctok(DN_VW), ctok(DN_QK), ctok(DN_QK), ctok(DN_QK), ctok(DN_HEADS * CHUNK),
                  pl.BlockSpec((bb, SUBLANES, LANES), lambda i, j: (i, j, 0)), ctok(DN_VW), st,
                  pl.BlockSpec((1, DN_DV), lambda i, j: (0, 0))],
        out_specs=(ctok(DN_VW), st),
        scratch_shapes=[pltpu.VMEM((bb, DN_HEADS, DN_DK, DN_DV), F32)],
        compiler_params=pltpu.CompilerParams(dimension_semantics=("arbitrary", "arbitrary"), vmem_limit_bytes=VMEM_LIMIT),
        name="delta_scan",
    )(uv, wk, qd, kd, qk, gl, z, s0, g_out)


def _merge_kernel(x_ref, ya_ref, yb_ref, graw_ref, w_up_pool_ref, w_up_dn_ref, w_out_ref, g_ffn_ref, x2_ref, xn_ref):
    graw = graw_ref[...]
    ga = _sigmoid(graw[:, :D_MODEL])
    gb = _sigmoid(graw[:, D_MODEL:])
    merged = ga * _mm(ya_ref[...], w_up_pool_ref[...]) + gb * _mm(yb_ref[...], w_up_dn_ref[...])
    x2 = x_ref[...] + _mm(merged, w_out_ref[...])
    x2_ref[...] = x2
    xn_ref[...] = _rms(x2, g_ffn_ref[...])


def _merge(x, ya, yb, graw, w_up_pool, w_up_dn, w_out, g_ffn, tm):
    t = x.shape[0]
    tok = lambda width: pl.BlockSpec((tm, width), lambda i: (i, 0))
    full = lambda a: pl.BlockSpec(a.shape, lambda i: (0,) * a.ndim)
    weights = (w_up_pool, w_up_dn, w_out, g_ffn)
    return pl.pallas_call(
        _merge_kernel,
        out_shape=(jax.ShapeDtypeStruct((t, D_MODEL), F32), jax.ShapeDtypeStruct((t, D_MODEL), F32)),
        grid=(t // tm,),
        in_specs=[tok(D_MODEL), tok(D_POOL), tok(DN_VW), tok(2 * D_MODEL)] + [full(a) for a in weights],
        out_specs=(tok(D_MODEL), tok(D_MODEL)),
        compiler_params=pltpu.CompilerParams(dimension_semantics=("arbitrary",), vmem_limit_bytes=VMEM_LIMIT),
        name="branch_merge",
    )(x, ya, yb, graw, *weights)


def _top16_rows(s, ids=None):
    if ids is None:
        ids = lax.broadcasted_iota(jnp.int32, s.shape, 0)
    ids = ids.astype(F32)
    vals, idxs = [], []
    for _ in range(PEER_TOPK):
        m = jnp.max(s, axis=0, keepdims=True)
        idx = jnp.min(jnp.where(s == m, ids, jnp.inf), axis=0, keepdims=True)
        vals.append(m)
        idxs.append(idx)
        s = jnp.where(ids == idx, -jnp.inf, s)
    return jnp.concatenate(vals, axis=0), jnp.concatenate(idxs, axis=0).astype(jnp.int32)


def _pair_candidates(v1, v2):
    tokens = v1.shape[1]
    sub = lax.broadcasted_iota(jnp.int32, (SUBLANES, tokens), 0)
    vals = [v1[0:1] + v2[0:SUBLANES], v1[0:1] + v2[SUBLANES:]]
    ids = [sub, sub + SUBLANES]
    for a in range(1, SUBLANES):
        vals.append(v1[a:a + 1] + v2[0:SUBLANES])
        ids.append(sub + a * PEER_TOPK)
    vals.append(v1[SUBLANES:] + v2[0:1])
    ids.append((sub + SUBLANES) * PEER_TOPK)
    return jnp.concatenate(vals, axis=0), jnp.concatenate(ids, axis=0)


def _take_rows(table, idx):
    out = jnp.zeros_like(table)
    for a in range(PEER_TOPK):
        out = jnp.where(idx == a, table[a:a + 1, :], out)
    return out


def _route_kernel(xn_ref, wq_ref, keys_ref, experts_ref, gates_ref):
    half = PEER_DQ // 2
    q = jnp.dot(xn_ref[...].astype(BF16), wq_ref[...], preferred_element_type=F32).astype(BF16)
    experts, gates = [], []
    for hh in range(PEER_HEADS):
        q1 = q[:, hh * PEER_DQ:hh * PEER_DQ + half]
        q2 = q[:, hh * PEER_DQ + half:(hh + 1) * PEER_DQ]
        nt = (((1,), (1,)), ((), ()))
        s1 = lax.dot_general(keys_ref[0, hh], q1, nt, preferred_element_type=F32)
        s2 = lax.dot_general(keys_ref[1, hh], q2, nt, preferred_element_type=F32)
        v1, i1 = _top16_rows(s1)
        v2, i2 = _top16_rows(s2)
        cv, ci = _top16_rows(*_pair_candidates(v1, v2))
        e1 = _take_rows(i1, ci // PEER_TOPK)
        e2 = _take_rows(i2, ci % PEER_TOPK)
        experts.append(e1 * PEER_KEYS + e2)
        ex = jnp.exp(cv - cv[0:1, :])
        gates.append(ex / jnp.sum(ex, axis=0, keepdims=True))
    experts_ref[...] = jnp.concatenate(experts, axis=0).T
    gates_ref[...] = jnp.concatenate(gates, axis=0).T


def _route(xn, wq, keys, tm):
    t = xn.shape[0]
    return pl.pallas_call(
        _route_kernel,
        out_shape=(jax.ShapeDtypeStruct((t, PEER_PICKS), jnp.int32), jax.ShapeDtypeStruct((t, PEER_PICKS), F32)),
        grid=(t // tm,),
        in_specs=[pl.BlockSpec((tm, D_MODEL), lambda i: (i, 0)),
                  pl.BlockSpec(wq.shape, lambda i: (0, 0)),
                  pl.BlockSpec(keys.shape, lambda i: (0, 0, 0, 0))],
        out_specs=(pl.BlockSpec((tm, PEER_PICKS), lambda i: (i, 0)), pl.BlockSpec((tm, PEER_PICKS), lambda i: (i, 0))),
        compiler_params=pltpu.CompilerParams(dimension_semantics=("arbitrary",), vmem_limit_bytes=VMEM_LIMIT),
        name="peer_route",
    )(xn, wq, keys)


EXPERT_TOKENS = 32
EXPERT_ISSUE_TOKENS = 26


def _experts_kernel(idx_ref, idx_next_ref, gates_ref, xn_ref, x2_ref, g_final_ref, tab_ref, out_ref, buf_a, buf_b, cb, sem):
    bufs = (buf_a, buf_b)
    i = pl.program_id(0)
    n = pl.num_programs(0)
    tb = EXPERT_TOKENS
    lane = lax.broadcasted_iota(jnp.int32, (PEER_HEADS, PEER_TOPK), 1)
    g_final = g_final_ref[...]

    def issue_picks(ref, sl, lo, hi):
        for f in range(lo, min(hi, tb * PEER_PICKS)):
            t, h, k = f // PEER_PICKS, (f // PEER_TOPK) % PEER_HEADS, f % PEER_TOPK
            e = ref[0, t, h * PEER_TOPK + k]
            pltpu.async_copy(tab_ref.at[e], bufs[sl].at[t, k, h], sem.at[sl], priority=k % 2)

    def wait_slot(sl):
        for t in range(tb):
            pltpu.make_async_copy(bufs[sl].at[t], bufs[sl].at[t], sem.at[sl]).wait()

    def gate_coefficients(t, sl):
        x = xn_ref[t]
        sub = lax.broadcasted_iota(jnp.int32, (PEER_HEADS, PEER_TOPK), 0)
        act = jnp.zeros((PEER_HEADS, PEER_TOPK), F32)
        for h in range(PEER_HEADS):
            part = jnp.zeros((SUBLANES, PEER_TOPK), F32)
            for k in range(PEER_TOPK):
                u = lax.bitcast_convert_type(bufs[sl][t, k, h] << 16, F32)
                part = jnp.where(lane == k, jnp.sum(u * x, axis=-1, keepdims=True), part)
            act = jnp.where(sub == h, jnp.sum(part, axis=0, keepdims=True), act)
        coef = gates_ref[t] * (0.5 * act * (1.0 + lax.erf(act * (0.5 ** 0.5))))
        for k in range(PEER_TOPK):
            cb[t, k] = jnp.broadcast_to(coef[:, k:k + 1], (PEER_HEADS, LANES))

    def mix_token(t, sl):
        accs = [jnp.zeros((SUBLANES, LANES), F32) for _ in range(4)]
        for k in range(PEER_TOPK):
            for h in range(PEER_HEADS):
                c = jnp.broadcast_to(cb[t, k, h:h + 1, :], (SUBLANES, LANES))
                v = lax.bitcast_convert_type(bufs[sl][t, k, h] & jnp.uint32(0xFFFF0000), F32)
                accs[h % 4] = accs[h % 4] + c * v
        z = x2_ref[t] + ((accs[0] + accs[1]) + (accs[2] + accs[3]))
        ms = jnp.sum(jnp.sum(z * z, axis=-1, keepdims=True), axis=0, keepdims=True) * (1.0 / D_MODEL)
        out_ref[t] = z * lax.rsqrt(ms + EPS) * g_final

    @pl.when(i == 0)
    def _():
        issue_picks(idx_ref, 0, 0, tb * PEER_PICKS)

    per_token = -(-tb * PEER_PICKS // EXPERT_ISSUE_TOKENS)

    def step(sl):
        wait_slot(sl)
        gate_coefficients(0, sl)
        for t in range(tb):
            @pl.when(i + t >= 0)
            def _():
                issue_picks(idx_next_ref, 1 - sl, t * per_token, (t + 1) * per_token)
                if t + 1 < tb:
                    gate_coefficients(t + 1, sl)
                mix_token(t, sl)

        @pl.when(i == n - 1)
        def _():
            wait_slot(1 - sl)

    @pl.when(i % 2 == 0)
    def _():
        step(0)

    @pl.when(i % 2 == 1)
    def _():
        step(1)


def _experts(experts, gates, xn, x2, g_final, table):
    t = xn.shape[0]
    tb = EXPERT_TOKENS
    nb = t // tb
    idx = experts.reshape(nb, tb, PEER_PICKS)
    gates = gates.reshape(t, PEER_HEADS, PEER_TOPK)
    tiles = lambda a: a.reshape(a.shape[0], SUBLANES, LANES)
    tok = pl.BlockSpec((tb, SUBLANES, LANES), lambda i: (i, 0, 0))
    out = pl.pallas_call(
        _experts_kernel,
        out_shape=jax.ShapeDtypeStruct((t, SUBLANES, LANES), F32),
        grid=(nb,),
        in_specs=[pl.BlockSpec((1, tb, PEER_PICKS), lambda i: (i, 0, 0), memory_space=pltpu.SMEM),
                  pl.BlockSpec((1, tb, PEER_PICKS), lambda i: (jnp.minimum(i + 1, nb - 1), 0, 0), memory_space=pltpu.SMEM),
                  pl.BlockSpec((tb, PEER_HEADS, PEER_TOPK), lambda i: (i, 0, 0)), tok, tok,
                  pl.BlockSpec((SUBLANES, LANES), lambda i: (0, 0)),
                  pl.BlockSpec(memory_space=pl.ANY)],
        out_specs=tok,
        scratch_shapes=[pltpu.VMEM((tb, PEER_TOPK, PEER_HEADS, SUBLANES, LANES), jnp.uint32),
                        pltpu.VMEM((tb, PEER_TOPK, PEER_HEADS, SUBLANES, LANES), jnp.uint32),
                        pltpu.VMEM((tb, PEER_TOPK, PEER_HEADS, LANES), F32),
                        pltpu.SemaphoreType.DMA((2,))],
        compiler_params=pltpu.CompilerParams(dimension_semantics=("arbitrary",), vmem_limit_bytes=VMEM_LIMIT),
        name="peer_experts",
    )(idx, idx, gates, tiles(xn), tiles(x2), g_final.reshape(SUBLANES, LANES), table)
    return out.reshape(t, D_MODEL)


def _pack_rows(u, v):
    half = lambda a: lax.bitcast_convert_type(a.astype(BF16), jnp.uint16).astype(jnp.uint32)
    return (half(u) | (half(v) << 16)).reshape(-1, SUBLANES, LANES)


def _pad_rows_front(a, rows):
    return jnp.pad(a, ((0, 0), (rows - a.shape[1], 0), (0, 0)))


MERGE_ROWS = 512
ROUTE_ROWS = 256
FRONT_ROWS = 512


def _group(x, pool_prev, conv_prev, s_prev, pos0, p):
    b, l, _ = x.shape
    tl = min(FRONT_ROWS, l)
    sb = min(b, FRONT_ROWS // tl)
    tm_merge = min(MERGE_ROWS, b * l)
    tm_route = min(ROUTE_ROWS, b * l)
    ya, q, k, v, z, graw, bg, pool_tail, conv_tail = _front(
        x, _pad_rows_front(pool_prev, POOL_PREFIX_ROWS), _pad_rows_front(conv_prev, CONV_PREFIX_ROWS), pos0, sb, tl,
        p["g_mix"], p["w_u"], p["w_qkv"], p["w_z"], p["w_ba"], p["w_g"], p["w_grp"], p["pool_scale"], p["w_conv8"],
        p["a_log_pad"], p["dt_pad"])
    yb, s_new = _delta(q, k, v, z, bg, s_prev, p["g_dn_out"])
    t = b * l
    flat = lambda a: a.reshape(t, a.shape[-1])
    x2, xn = _merge(flat(x), flat(ya), flat(yb), flat(graw), p["w_up_pool"], p["w_up_dn"], p["w_out"], p["g_ffn"], tm_merge)
    experts, gates = _route(xn, p["w_peer_q"], p["peer_keys"], tm_route)
    y = _experts(experts, gates, xn, x2, p["g_final"], p["peer_table"])
    return (y.reshape(b, l, D_MODEL), pool_tail[:, POOL_PREFIX_ROWS - POOL_STATE:],
            conv_tail[:, CONV_PREFIX_ROWS - (CONV_W - 1):], s_new)


def kernel(x_prompt, x_sample, cache_pool, state_dn_conv, state_dn, g_mix, w_in, w_pool_grp, pool_scale, w_conv, a_log,
           dt_bias, g_dn_out, w_up_pool, w_up_dn, w_out, g_ffn, w_peer_q, peer_sub_keys, peer_u, peer_v, g_final):
    depth = w_in.shape[0]
    assert depth == 1
    bp = x_prompt.shape[0]
    lane_pad = lambda a, off: jnp.pad(a.astype(F32)[None, :], ((0, 0), (off, LANES - off - a.shape[0])))
    w = w_in[0]
    params = {
        "g_mix": g_mix[0][None, :],
        "w_u": w[:, OFF_U:OFF_QKV].astype(BF16),
        "w_qkv": w[:, OFF_QKV:OFF_Z].astype(BF16),
        "w_z": w[:, OFF_Z:OFF_B].astype(BF16),
        "w_ba": jnp.pad(w[:, OFF_B:OFF_G], ((0, 0), (0, LANES - 2 * DN_HEADS))).astype(BF16),
        "w_g": w[:, OFF_G:].astype(BF16),
        "w_grp": w_pool_grp[0].astype(BF16),
        "pool_scale": pool_scale[0][None, :],
        "w_conv8": jnp.pad(w_conv[0], ((0, SUBLANES - CONV_W), (0, 0))),
        "a_log_pad": lane_pad(a_log[0], DN_HEADS),
        "dt_pad": lane_pad(dt_bias[0], DN_HEADS),
        "g_dn_out": g_dn_out[0][None, :],
        "w_up_pool": w_up_pool[0].astype(BF16),
        "w_up_dn": w_up_dn[0].astype(BF16),
        "w_out": w_out[0].astype(BF16),
        "g_ffn": g_ffn[0][None, :],
        "w_peer_q": w_peer_q[0].reshape(D_MODEL, PEER_HEADS * PEER_DQ).astype(BF16),
        "peer_keys": peer_sub_keys[0].astype(BF16),
        "peer_table": _pack_rows(peer_u[0], peer_v[0]),
        "g_final": g_final[None, :],
    }
    zeros = lambda *shape: jnp.zeros(shape, F32)
    yp, pool_p, conv_p, dn_p = _group(
        x_prompt, zeros(bp, POOL_STATE, D_POOL), zeros(bp, CONV_W - 1, CONV_CH), zeros(bp, DN_HEADS, DN_DK, DN_DV),
        0, params)
    ys, pool_s, conv_s, dn_s = _group(
        x_sample, cache_pool[0], state_dn_conv[0], state_dn[0].astype(F32),
        PAST_LEN, params)
    return (yp, ys, pool_p[None], conv_p[None], dn_p[None].astype(state_dn.dtype),
            pool_s[None], conv_s[None], dn_s[None].astype(state_dn.dtype))
```

```python
import functools

import jax
import jax.numpy as jnp
from jax import lax
from jax.experimental import pallas as pl
from jax.experimental.pallas import tpu as pltpu

D_MODEL = 1024
CHUNK = 64
D_POOL = 512
POOL_WINDOWS = (2, 4, 8, 16)
POOL_GROUP = 128
POOL_STATE = 15
DN_HEADS = 4
DN_DK = 128
DN_DV = 128
DN_QK = DN_HEADS * DN_DK
DN_VW = DN_HEADS * DN_DV
CONV_W = 4
CONV_CH = 2 * DN_QK + DN_VW
OFF_U = 0
OFF_QKV = OFF_U + D_POOL
OFF_Z = OFF_QKV + CONV_CH
OFF_B = OFF_Z + DN_VW
OFF_A = OFF_B + DN_HEADS
OFF_G = OFF_A + DN_HEADS
PEER_HEADS = 8
PEER_KEYS = 128
PEER_DQ = 256
PEER_TOPK = 16
PEER_PICKS = PEER_HEADS * PEER_TOPK
EPS = 1e-6
PAST_LEN = 4096

LANES = 128
SUBLANES = 8
POOL_PREFIX_ROWS = 16
CONV_PREFIX_ROWS = 8
VMEM_LIMIT = 56 * 1024 * 1024

F32 = jnp.float32
BF16 = jnp.bfloat16
HIGHEST = lax.Precision.HIGHEST


def _mm(a, b):
    return jnp.dot(a.astype(BF16), b.astype(BF16), preferred_element_type=F32)


def _mm_f32(a, b):
    return jnp.dot(a, b, precision=HIGHEST, preferred_element_type=F32)


def _mm_nt_f32(a, b):
    return lax.dot_general(a, b, (((1,), (1,)), ((), ())), precision=HIGHEST, preferred_element_type=F32)


def _mm_tn_f32(a, b):
    return lax.dot_general(a, b, (((0,), (0,)), ((), ())), precision=HIGHEST, preferred_element_type=F32)


def _sigmoid(x):
    return 1.0 / (1.0 + jnp.exp(-x))


def _silu(x):
    return x * _sigmoid(x)


def _softplus(x):
    return jnp.maximum(x, 0.0) + jnp.log1p(jnp.exp(-jnp.abs(x)))


def _rms(x, g):
    return x * lax.rsqrt(jnp.mean(x * x, axis=-1, keepdims=True) + EPS) * g


def _front_kernel(pos0, x_ref, pool_pre_ref, conv_pre_ref, g_mix_ref, w_u_ref, w_qkv_ref, w_z_ref, w_ba_ref, w_g_ref,
                  w_grp_ref, pool_scale_ref, w_conv_ref, a_log_ref, dt_ref,
                  ya_ref, q_ref, k_ref, v_ref, z_ref, graw_ref, bg_ref, pool_tail_ref, conv_tail_ref,
                  carry_u, carry_c):
    sb, tl, _ = x_ref.shape
    li = pl.program_id(1)

    @pl.when(li == 0)
    def _():
        carry_u[...] = pool_pre_ref[...]
        carry_c[...] = conv_pre_ref[...]

    x = x_ref[...].reshape(sb * tl, D_MODEL)
    h = _rms(x, g_mix_ref[...]).astype(BF16)
    u = jnp.dot(h, w_u_ref[...], preferred_element_type=F32)
    qkv = jnp.dot(h, w_qkv_ref[...], preferred_element_type=F32)
    z_ref[...] = jnp.dot(h, w_z_ref[...], preferred_element_type=F32).reshape(sb, tl, DN_VW)
    graw_ref[...] = jnp.dot(h, w_g_ref[...], preferred_element_type=F32).reshape(sb, tl, 2 * D_MODEL)
    ba = jnp.dot(h, w_ba_ref[...], preferred_element_type=F32)
    lane = lax.broadcasted_iota(jnp.int32, ba.shape, 1)
    beta = _sigmoid(ba)
    g = -jnp.exp(a_log_ref[...]) * _softplus(ba + dt_ref[...])
    bg_ref[...] = jnp.where(lane < DN_HEADS, beta, g).reshape(sb, tl, LANES)

    row = lax.broadcasted_iota(jnp.int32, (tl, POOL_GROUP), 0)
    pos1 = pos0 + li * tl + row + 1
    w_conv = w_conv_ref[...]

    for s in range(sb):
        u_s = u[s * tl:(s + 1) * tl]
        ext = jnp.concatenate([carry_u[s], u_s], axis=0)
        mixed = []
        for gi, w in enumerate(POOL_WINDOWS):
            acc = ext[:, gi * POOL_GROUP:(gi + 1) * POOL_GROUP]
            span = 1
            while span < w:
                acc = acc + pltpu.roll(acc, span, axis=0)
                span *= 2
            win = acc[POOL_PREFIX_ROWS:]
            cnt = jnp.minimum(pos1, w).astype(F32)
            pooled = win / cnt - u_s[:, gi * POOL_GROUP:(gi + 1) * POOL_GROUP]
            mixed.append(_mm(pooled, w_grp_ref[gi]))
        ya_ref[s] = jnp.concatenate(mixed, axis=1) * pool_scale_ref[...]
        pool_tail_ref[s] = ext[tl:]
        carry_u[s] = ext[tl:]

        c_s = qkv[s * tl:(s + 1) * tl]
        cext = jnp.concatenate([carry_c[s], c_s], axis=0)
        y = c_s * w_conv[CONV_W - 1:CONV_W]
        for j in range(1, CONV_W):
            y = y + pltpu.roll(cext, j, axis=0)[CONV_PREFIX_ROWS:] * w_conv[CONV_W - 1 - j:CONV_W - j]
        y = _silu(y)
        conv_tail_ref[s] = cext[tl:]
        carry_c[s] = cext[tl:]
        for hh in range(DN_HEADS):
            sl = slice(hh * DN_DK, (hh + 1) * DN_DK)
            qh = y[:, sl]
            q_ref[s, :, sl] = qh * lax.rsqrt(jnp.sum(qh * qh, axis=-1, keepdims=True) + EPS)
            kh = y[:, DN_QK + hh * DN_DK:DN_QK + (hh + 1) * DN_DK]
            k_ref[s, :, sl] = kh * lax.rsqrt(jnp.sum(kh * kh, axis=-1, keepdims=True) + EPS)
        v_ref[s] = y[:, 2 * DN_QK:]


def _front(x, pool_pre, conv_pre, pos0, sb, tl, g_mix, w_u, w_qkv, w_z, w_ba, w_g, w_grp, pool_scale, w_conv8,
           a_log_pad, dt_pad):
    b, l, _ = x.shape
    grid = (b // sb, l // tl)
    tok = lambda width: pl.BlockSpec((sb, tl, width), lambda i, j: (i, j, 0))
    seq = lambda rows, width: pl.BlockSpec((sb, rows, width), lambda i, j: (i, 0, 0))
    full = lambda a: pl.BlockSpec(a.shape, lambda i, j: (0,) * a.ndim)
    weights = (g_mix, w_u, w_qkv, w_z, w_ba, w_g, w_grp, pool_scale, w_conv8, a_log_pad, dt_pad)
    out_shape = (
        jax.ShapeDtypeStruct((b, l, D_POOL), F32),
        jax.ShapeDtypeStruct((b, l, DN_QK), F32),
        jax.ShapeDtypeStruct((b, l, DN_QK), F32),
        jax.ShapeDtypeStruct((b, l, DN_VW), F32),
        jax.ShapeDtypeStruct((b, l, DN_VW), F32),
        jax.ShapeDtypeStruct((b, l, 2 * D_MODEL), F32),
        jax.ShapeDtypeStruct((b, l, LANES), F32),
        jax.ShapeDtypeStruct((b, POOL_PREFIX_ROWS, D_POOL), F32),
        jax.ShapeDtypeStruct((b, CONV_PREFIX_ROWS, CONV_CH), F32),
    )
    return pl.pallas_call(
        functools.partial(_front_kernel, pos0),
        out_shape=out_shape,
        grid=grid,
        in_specs=[tok(D_MODEL), seq(POOL_PREFIX_ROWS, D_POOL), seq(CONV_PREFIX_ROWS, CONV_CH)] + [full(a) for a in weights],
        out_specs=(tok(D_POOL), tok(DN_QK), tok(DN_QK), tok(DN_VW), tok(DN_VW), tok(2 * D_MODEL), tok(LANES),
                   seq(POOL_PREFIX_ROWS, D_POOL), seq(CONV_PREFIX_ROWS, CONV_CH)),
        scratch_shapes=[pltpu.VMEM((sb, POOL_PREFIX_ROWS, D_POOL), F32), pltpu.VMEM((sb, CONV_PREFIX_ROWS, CONV_CH), F32)],
        compiler_params=pltpu.CompilerParams(dimension_semantics=("arbitrary", "arbitrary"), vmem_limit_bytes=VMEM_LIMIT),
        name="mixer_front",
    )(x, pool_pre, conv_pre, *weights)


_NN = (((1,), (0,)), ((), ()))
_NT = (((1,), (1,)), ((), ()))
_TN = (((0,), (0,)), ((), ()))


def _dot_bf16(a, b, dims):
    return lax.dot_general(a.astype(BF16), b.astype(BF16), dims, preferred_element_type=F32)


def _mm_nt(a, b):
    return _dot_bf16(a, b, _NT)


def _mm_chain(a, b):
    return _dot_bf16(a, b, _NN)


def _mm_state(a, b):
    return _dot_bf16(a, b, _NN)


def _mm_tn_state(a, b):
    return _dot_bf16(a, b, _TN)


def _delta_prep_kernel(q_ref, k_ref, v_ref, bg_ref, uv_ref, wk_ref, qd_ref, kd_ref, qk_ref, gl_ref):
    sb = q_ref.shape[0]
    ncb = q_ref.shape[1] // CHUNK
    row = lax.broadcasted_iota(jnp.int32, (CHUNK, CHUNK), 0)
    col = lax.broadcasted_iota(jnp.int32, (CHUNK, CHUNK), 1)
    incl = row >= col
    strict = row > col
    tri = incl.astype(F32)
    xs, pw = [], []
    for s in range(sb):
        for c in range(ncb):
            rows = slice(c * CHUNK, (c + 1) * CHUNK)
            bg = bg_ref[s, rows, :]
            gc_all = _mm_f32(tri, bg)
            gc_rows = gc_all.T
            gl_ref[s, c * SUBLANES:(c + 1) * SUBLANES, :] = gc_all[CHUNK - SUBLANES:, :]
            for hh in range(DN_HEADS):
                sl = slice(hh * DN_DK, (hh + 1) * DN_DK)
                q = q_ref[s, rows, sl] * (DN_DK ** -0.5)
                k = k_ref[s, rows, sl]
                beta = bg[:, hh:hh + 1]
                gcol = gc_all[:, DN_HEADS + hh:DN_HEADS + hh + 1]
                grow = gc_rows[DN_HEADS + hh:DN_HEADS + hh + 1, :]
                decay = jnp.exp(jnp.where(incl, gcol - grow, -jnp.inf))
                egc = jnp.exp(gcol)
                kb = k * beta
                pw.append(-jnp.where(strict, _mm_nt(kb, k) * decay, 0.0))
                xs.append(jnp.concatenate([v_ref[s, rows, sl] * beta, kb * egc], axis=1))
                qk_ref[s, rows, hh * CHUNK:(hh + 1) * CHUNK] = _mm_nt(q, k) * decay
                gl = gcol[CHUNK - 1:CHUNK, :]
                qd_ref[s, rows, sl] = q * egc
                kd_ref[s, rows, sl] = k * jnp.exp(gl - gcol)
    span = 1
    while True:
        xs = [x + _mm_chain(p, x) for p, x in zip(pw, xs)]
        span *= 2
        if span >= CHUNK:
            break
        pw = [_mm_chain(p, p) for p in pw]
    i = 0
    for s in range(sb):
        for c in range(ncb):
            rows = slice(c * CHUNK, (c + 1) * CHUNK)
            for hh in range(DN_HEADS):
                sl = slice(hh * DN_DK, (hh + 1) * DN_DK)
                uv_ref[s, rows, sl] = xs[i][:, :DN_DV]
                wk_ref[s, rows, sl] = xs[i][:, DN_DV:]
                i += 1


def _delta_scan_kernel(uv_ref, wk_ref, qd_ref, kd_ref, qk_ref, gl_ref, z_ref, s0_ref, g_out_ref, yb_ref, s_out_ref, s_scr):
    ci = pl.program_id(1)
    bb = uv_ref.shape[0]

    @pl.when(ci == 0)
    def _():
        s_scr[...] = s0_ref[...]

    g_out = g_out_ref[...]
    probs = [(b, hh) for b in range(bb) for hh in range(DN_HEADS)]
    lanes = lambda hh: slice(hh * DN_DK, (hh + 1) * DN_DK)
    s_old = [s_scr[b, hh] for b, hh in probs]
    v_new = [uv_ref[b, :, lanes(hh)] - _mm_state(wk_ref[b, :, lanes(hh)], s) for (b, hh), s in zip(probs, s_old)]
    o_state = [_mm_state(qd_ref[b, :, lanes(hh)], s) for (b, hh), s in zip(probs, s_old)]
    for (b, hh), s, vn, os_ in zip(probs, s_old, v_new, o_state):
        o = os_ + _mm_state(qk_ref[b, :, hh * CHUNK:(hh + 1) * CHUNK], vn)
        egl = jnp.exp(gl_ref[b, SUBLANES - 1:SUBLANES, DN_HEADS + hh:DN_HEADS + hh + 1])
        s_new = s * egl + _mm_tn_state(kd_ref[b, :, lanes(hh)], vn)
        s_scr[b, hh] = s_new
        s_out_ref[b, hh] = s_new
        yb_ref[b, :, lanes(hh)] = _rms(o, g_out) * _silu(z_ref[b, :, lanes(hh)])


DELTA_PREP_CHUNKS = 4
DELTA_SCAN_SEQS = 8


def _delta(q, k, v, z, bg, s0, g_out):
    b, l, _ = q.shape
    nc = l // CHUNK
    ncb = min(DELTA_PREP_CHUNKS, nc)
    sb = min(b, DELTA_PREP_CHUNKS // ncb)
    rows = ncb * CHUNK
    tok = lambda width: pl.BlockSpec((sb, rows, width), lambda i, j: (i, j, 0))
    glspec = pl.BlockSpec((sb, ncb * SUBLANES, LANES), lambda i, j: (i, j, 0))
    wide = lambda width: jax.ShapeDtypeStruct((b, l, width), F32)
    uv, wk, qd, kd, qk, gl = pl.pallas_call(
        _delta_prep_kernel,
        out_shape=(wide(DN_VW), wide(DN_QK), wide(DN_QK), wide(DN_QK), wide(DN_HEADS * CHUNK),
                   jax.ShapeDtypeStruct((b, nc * SUBLANES, LANES), F32)),
        grid=(b // sb, nc // ncb),
        in_specs=[tok(DN_QK), tok(DN_QK), tok(DN_VW), tok(LANES)],
        out_specs=(tok(DN_VW), tok(DN_QK), tok(DN_QK), tok(DN_QK), tok(DN_HEADS * CHUNK), glspec),
        compiler_params=pltpu.CompilerParams(dimension_semantics=("arbitrary", "arbitrary"), vmem_limit_bytes=VMEM_LIMIT),
        name="delta_prep",
    )(q, k, v, bg)

    bb = min(DELTA_SCAN_SEQS, b)
    ctok = lambda width: pl.BlockSpec((bb, CHUNK, width), lambda i, j: (i, j, 0))
    st = pl.BlockSpec((bb, DN_HEADS, DN_DK, DN_DV), lambda i, j: (i, 0, 0, 0))
    return pl.pallas_call(
        _delta_scan_kernel,
        out_shape=(wide(DN_VW), jax.ShapeDtypeStruct((b, DN_HEADS, DN_DK, DN_DV), F32)),
        grid=(b // bb, nc),
        in_specs=[ctok(DN_VW), ctok(DN_QK), ctok(DN_QK), ctok(DN_QK), ctok(DN_HEADS * CHUNK),
                  pl.BlockSpec((bb, SUBLANES, LANES), lambda i, j: (i, j, 0)), ctok(DN_VW), st,
                  pl.BlockSpec((1, DN_DV), lambda i, j: (0, 0))],
        out_specs=(ctok(DN_VW), st),
        scratch_shapes=[pltpu.VMEM((bb, DN_HEADS, DN_DK, DN_DV), F32)],
        compiler_params=pltpu.CompilerParams(dimension_semantics=("arbitrary", "arbitrary"), vmem_limit_bytes=VMEM_LIMIT),
        name="delta_scan",
    )(uv, wk, qd, kd, qk, gl, z, s0, g_out)


def _merge_kernel(x_ref, ya_ref, yb_ref, graw_ref, w_up_pool_ref, w_up_dn_ref, w_out_ref, g_ffn_ref, x2_ref, xn_ref):
    graw = graw_ref[...]
    ga = _sigmoid(graw[:, :D_MODEL])
    gb = _sigmoid(graw[:, D_MODEL:])
    merged = ga * _mm(ya_ref[...], w_up_pool_ref[...]) + gb * _mm(yb_ref[...], w_up_dn_ref[...])
    x2 = x_ref[...] + _mm(merged, w_out_ref[...])
    x2_ref[...] = x2
    xn_ref[...] = _rms(x2, g_ffn_ref[...])


def _merge(x, ya, yb, graw, w_up_pool, w_up_dn, w_out, g_ffn, tm):
    t = x.shape[0]
    tok = lambda width: pl.BlockSpec((tm, width), lambda i: (i, 0))
    full = lambda a: pl.BlockSpec(a.shape, lambda i: (0,) * a.ndim)
    weights = (w_up_pool, w_up_dn, w_out, g_ffn)
    return pl.pallas_call(
        _merge_kernel,
        out_shape=(jax.ShapeDtypeStruct((t, D_MODEL), F32), jax.ShapeDtypeStruct((t, D_MODEL), F32)),
        grid=(t // tm,),
        in_specs=[tok(D_MODEL), tok(D_POOL), tok(DN_VW), tok(2 * D_MODEL)] + [full(a) for a in weights],
        out_specs=(tok(D_MODEL), tok(D_MODEL)),
        compiler_params=pltpu.CompilerParams(dimension_semantics=("arbitrary",), vmem_limit_bytes=VMEM_LIMIT),
        name="branch_merge",
    )(x, ya, yb, graw, *weights)


def _top16_rows(s, ids=None):
    if ids is None:
        ids = lax.broadcasted_iota(jnp.int32, s.shape, 0)
    ids = ids.astype(F32)
    vals, idxs = [], []
    for _ in range(PEER_TOPK):
        m = jnp.max(s, axis=0, keepdims=True)
        idx = jnp.min(jnp.where(s == m, ids, jnp.inf), axis=0, keepdims=True)
        vals.append(m)
        idxs.append(idx)
        s = jnp.where(ids == idx, -jnp.inf, s)
    return jnp.concatenate(vals, axis=0), jnp.concatenate(idxs, axis=0).astype(jnp.int32)


def _pair_candidates(v1, v2):
    tokens = v1.shape[1]
    sub = lax.broadcasted_iota(jnp.int32, (SUBLANES, tokens), 0)
    vals = [v1[0:1] + v2[0:SUBLANES], v1[0:1] + v2[SUBLANES:]]
    ids = [sub, sub + SUBLANES]
    for a in range(1, SUBLANES):
        vals.append(v1[a:a + 1] + v2[0:SUBLANES])
        ids.append(sub + a * PEER_TOPK)
    vals.append(v1[SUBLANES:] + v2[0:1])
    ids.append((sub + SUBLANES) * PEER_TOPK)
    return jnp.concatenate(vals, axis=0), jnp.concatenate(ids, axis=0)


def _take_rows(table, idx):
    out = jnp.zeros_like(table)
    for a in range(PEER_TOPK):
        out = jnp.where(idx == a, table[a:a + 1, :], out)
    return out


def _route_kernel(xn_ref, wq_ref, keys_ref, experts_ref, gates_ref):
    half = PEER_DQ // 2
    q = jnp.dot(xn_ref[...].astype(BF16), wq_ref[...], preferred_element_type=F32).astype(BF16)
    experts, gates = [], []
    for hh in range(PEER_HEADS):
        q1 = q[:, hh * PEER_DQ:hh * PEER_DQ + half]
        q2 = q[:, hh * PEER_DQ + half:(hh + 1) * PEER_DQ]
        nt = (((1,), (1,)), ((), ()))
        s1 = lax.dot_general(keys_ref[0, hh], q1, nt, preferred_element_type=F32)
        s2 = lax.dot_general(keys_ref[1, hh], q2, nt, preferred_element_type=F32)
        v1, i1 = _top16_rows(s1)
        v2, i2 = _top16_rows(s2)
        cv, ci = _top16_rows(*_pair_candidates(v1, v2))
        e1 = _take_rows(i1, ci // PEER_TOPK)
        e2 = _take_rows(i2, ci % PEER_TOPK)
        experts.append(e1 * PEER_KEYS + e2)
        ex = jnp.exp(cv - cv[0:1, :])
        gates.append(ex / jnp.sum(ex, axis=0, keepdims=True))
    experts_ref[...] = jnp.concatenate(experts, axis=0).T
    gates_ref[...] = jnp.concatenate(gates, axis=0).T


def _route(xn, wq, keys, tm):
    t = xn.shape[0]
    return pl.pallas_call(
        _route_kernel,
        out_shape=(jax.ShapeDtypeStruct((t, PEER_PICKS), jnp.int32), jax.ShapeDtypeStruct((t, PEER_PICKS), F32)),
        grid=(t // tm,),
        in_specs=[pl.BlockSpec((tm, D_MODEL), lambda i: (i, 0)),
                  pl.BlockSpec(wq.shape, lambda i: (0, 0)),
                  pl.BlockSpec(keys.shape, lambda i: (0, 0, 0, 0))],
        out_specs=(pl.BlockSpec((tm, PEER_PICKS), lambda i: (i, 0)), pl.BlockSpec((tm, PEER_PICKS), lambda i: (i, 0))),
        compiler_params=pltpu.CompilerParams(dimension_semantics=("arbitrary",), vmem_limit_bytes=VMEM_LIMIT),
        name="peer_route",
    )(xn, wq, keys)


EXPERT_TOKENS = 32
EXPERT_ISSUE_TOKENS = 28


def _experts_kernel(idx_ref, idx_next_ref, gates_ref, xn_ref, x2_ref, g_final_ref, tab_ref, out_ref, buf_a, buf_b, cb, sem):
    bufs = (buf_a, buf_b)
    i = pl.program_id(0)
    n = pl.num_programs(0)
    tb = EXPERT_TOKENS
    lane = lax.broadcasted_iota(jnp.int32, (PEER_HEADS, PEER_TOPK), 1)
    g_final = g_final_ref[...]

    def issue_picks(ref, sl, lo, hi):
        for f in range(lo, min(hi, tb * PEER_PICKS)):
            t, h, k = f // PEER_PICKS, (f // PEER_TOPK) % PEER_HEADS, f % PEER_TOPK
            e = ref[0, t, h * PEER_TOPK + k]
            pltpu.async_copy(tab_ref.at[e], bufs[sl].at[t, k, h], sem.at[sl], priority=k % 2)

    def wait_slot(sl):
        for t in range(tb):
            pltpu.make_async_copy(bufs[sl].at[t], bufs[sl].at[t], sem.at[sl]).wait()

    def gate_coefficients(t, sl):
        x = xn_ref[t]
        sub = lax.broadcasted_iota(jnp.int32, (PEER_HEADS, PEER_TOPK), 0)
        act = jnp.zeros((PEER_HEADS, PEER_TOPK), F32)
        for h in range(PEER_HEADS):
            part = jnp.zeros((SUBLANES, PEER_TOPK), F32)
            for k in range(PEER_TOPK):
                u = lax.bitcast_convert_type(bufs[sl][t, k, h] << 16, F32)
                part = jnp.where(lane == k, jnp.sum(u * x, axis=-1, keepdims=True), part)
            act = jnp.where(sub == h, jnp.sum(part, axis=0, keepdims=True), act)
        coef = gates_ref[t] * (0.5 * act * (1.0 + lax.erf(act * (0.5 ** 0.5))))
        for k in range(PEER_TOPK):
            cb[t, k] = jnp.broadcast_to(coef[:, k:k + 1], (PEER_HEADS, LANES))

    def mix_token(t, sl):
        accs = [jnp.zeros((SUBLANES, LANES), F32) for _ in range(4)]
        for k in range(PEER_TOPK):
            for h in range(PEER_HEADS):
                c = jnp.broadcast_to(cb[t, k, h:h + 1, :], (SUBLANES, LANES))
                v = lax.bitcast_convert_type(bufs[sl][t, k, h] & jnp.uint32(0xFFFF0000), F32)
                accs[h % 4] = accs[h % 4] + c * v
        z = x2_ref[t] + ((accs[0] + accs[1]) + (accs[2] + accs[3]))
        ms = jnp.sum(jnp.sum(z * z, axis=-1, keepdims=True), axis=0, keepdims=True) * (1.0 / D_MODEL)
        out_ref[t] = z * lax.rsqrt(ms + EPS) * g_final

    @pl.when(i == 0)
    def _():
        issue_picks(idx_ref, 0, 0, tb * PEER_PICKS)

    per_token = -(-tb * PEER_PICKS // EXPERT_ISSUE_TOKENS)

    def step(sl):
        wait_slot(sl)
        gate_coefficients(0, sl)
        for t in range(tb):
            @pl.when(i + t >= 0)
            def _():
                issue_picks(idx_next_ref, 1 - sl, t * per_token, (t + 1) * per_token)
                if t + 1 < tb:
                    gate_coefficients(t + 1, sl)
                mix_token(t, sl)

        @pl.when(i == n - 1)
        def _():
            wait_slot(1 - sl)

    @pl.when(i % 2 == 0)
    def _():
        step(0)

    @pl.when(i % 2 == 1)
    def _():
        step(1)


def _experts(experts, gates, xn, x2, g_final, table):
    t = xn.shape[0]
    tb = EXPERT_TOKENS
    nb = t // tb
    idx = experts.reshape(nb, tb, PEER_PICKS)
    gates = gates.reshape(t, PEER_HEADS, PEER_TOPK)
    tiles = lambda a: a.reshape(a.shape[0], SUBLANES, LANES)
    tok = pl.BlockSpec((tb, SUBLANES, LANES), lambda i: (i, 0, 0))
    out = pl.pallas_call(
        _experts_kernel,
        out_shape=jax.ShapeDtypeStruct((t, SUBLANES, LANES), F32),
        grid=(nb,),
        in_specs=[pl.BlockSpec((1, tb, PEER_PICKS), lambda i: (i, 0, 0), memory_space=pltpu.SMEM),
                  pl.BlockSpec((1, tb, PEER_PICKS), lambda i: (jnp.minimum(i + 1, nb - 1), 0, 0), memory_space=pltpu.SMEM),
                  pl.BlockSpec((tb, PEER_HEADS, PEER_TOPK), lambda i: (i, 0, 0)), tok, tok,
                  pl.BlockSpec((SUBLANES, LANES), lambda i: (0, 0)),
                  pl.BlockSpec(memory_space=pl.ANY)],
        out_specs=tok,
        scratch_shapes=[pltpu.VMEM((tb, PEER_TOPK, PEER_HEADS, SUBLANES, LANES), jnp.uint32),
                        pltpu.VMEM((tb, PEER_TOPK, PEER_HEADS, SUBLANES, LANES), jnp.uint32),
                        pltpu.VMEM((tb, PEER_TOPK, PEER_HEADS, LANES), F32),
                        pltpu.SemaphoreType.DMA((2,))],
        compiler_params=pltpu.CompilerParams(dimension_semantics=("arbitrary",), vmem_limit_bytes=VMEM_LIMIT),
        name="peer_experts",
    )(idx, idx, gates, tiles(xn), tiles(x2), g_final.reshape(SUBLANES, LANES), table)
    return out.reshape(t, D_MODEL)


def _pack_rows(u, v):
    half = lambda a: lax.bitcast_convert_type(a.astype(BF16), jnp.uint16).astype(jnp.uint32)
    return (half(u) | (half(v) << 16)).reshape(-1, SUBLANES, LANES)


def _pad_rows_front(a, rows):
    return jnp.pad(a, ((0, 0), (rows - a.shape[1], 0), (0, 0)))


MERGE_ROWS = 512
ROUTE_ROWS = 256
FRONT_ROWS = 512


def _group(x, pool_prev, conv_prev, s_prev, pos0, p):
    b, l, _ = x.shape
    tl = min(FRONT_ROWS, l)
    sb = min(b, FRONT_ROWS // tl)
    tm_merge = min(MERGE_ROWS, b * l)
    tm_route = min(ROUTE_ROWS, b * l)
    ya, q, k, v, z, graw, bg, pool_tail, conv_tail = _front(
        x, _pad_rows_front(pool_prev, POOL_PREFIX_ROWS), _pad_rows_front(conv_prev, CONV_PREFIX_ROWS), pos0, sb, tl,
        p["g_mix"], p["w_u"], p["w_qkv"], p["w_z"], p["w_ba"], p["w_g"], p["w_grp"], p["pool_scale"], p["w_conv8"],
        p["a_log_pad"], p["dt_pad"])
    yb, s_new = _delta(q, k, v, z, bg, s_prev, p["g_dn_out"])
    t = b * l
    flat = lambda a: a.reshape(t, a.shape[-1])
    x2, xn = _merge(flat(x), flat(ya), flat(yb), flat(graw), p["w_up_pool"], p["w_up_dn"], p["w_out"], p["g_ffn"], tm_merge)
    experts, gates = _route(xn, p["w_peer_q"], p["peer_keys"], tm_route)
    y = _experts(experts, gates, xn, x2, p["g_final"], p["peer_table"])
    return (y.reshape(b, l, D_MODEL), pool_tail[:, POOL_PREFIX_ROWS - POOL_STATE:],
            conv_tail[:, CONV_PREFIX_ROWS - (CONV_W - 1):], s_new)


def kernel(x_prompt, x_sample, cache_pool, state_dn_conv, state_dn, g_mix, w_in, w_pool_grp, pool_scale, w_conv, a_log,
           dt_bias, g_dn_out, w_up_pool, w_up_dn, w_out, g_ffn, w_peer_q, peer_sub_keys, peer_u, peer_v, g_final):
    depth = w_in.shape[0]
    assert depth == 1
    bp = x_prompt.shape[0]
    lane_pad = lambda a, off: jnp.pad(a.astype(F32)[None, :], ((0, 0), (off, LANES - off - a.shape[0])))
    w = w_in[0]
    params = {
        "g_mix": g_mix[0][None, :],
        "w_u": w[:, OFF_U:OFF_QKV].astype(BF16),
        "w_qkv": w[:, OFF_QKV:OFF_Z].astype(BF16),
        "w_z": w[:, OFF_Z:OFF_B].astype(BF16),
        "w_ba": jnp.pad(w[:, OFF_B:OFF_G], ((0, 0), (0, LANES - 2 * DN_HEADS))).astype(BF16),
        "w_g": w[:, OFF_G:].astype(BF16),
        "w_grp": w_pool_grp[0].astype(BF16),
        "pool_scale": pool_scale[0][None, :],
        "w_conv8": jnp.pad(w_conv[0], ((0, SUBLANES - CONV_W), (0, 0))),
        "a_log_pad": lane_pad(a_log[0], DN_HEADS),
        "dt_pad": lane_pad(dt_bias[0], DN_HEADS),
        "g_dn_out": g_dn_out[0][None, :],
        "w_up_pool": w_up_pool[0].astype(BF16),
        "w_up_dn": w_up_dn[0].astype(BF16),
        "w_out": w_out[0].astype(BF16),
        "g_ffn": g_ffn[0][None, :],
        "w_peer_q": w_peer_q[0].reshape(D_MODEL, PEER_HEADS * PEER_DQ).astype(BF16),
        "peer_keys": peer_sub_keys[0].astype(BF16),
        "peer_table": _pack_rows(peer_u[0], peer_v[0]),
        "g_final": g_final[None, :],
    }
    zeros = lambda *shape: jnp.zeros(shape, F32)
    yp, pool_p, conv_p, dn_p = _group(
        x_prompt, zeros(bp, POOL_STATE, D_POOL), zeros(bp, CONV_W - 1, CONV_CH), zeros(bp, DN_HEADS, DN_DK, DN_DV),
        0, params)
    ys, pool_s, conv_s, dn_s = _group(
        x_sample, cache_pool[0], state_dn_conv[0], state_dn[0].astype(F32),
        PAST_LEN, params)
    return (yp, ys, pool_p[None], conv_p[None], dn_p[None].astype(state_dn.dtype),
            pool_s[None], conv_s[None], dn_s[None].astype(state_dn.dtype))
```

```python
import functools

import jax
import jax.numpy as jnp
from jax import lax
from jax.experimental import pallas as pl
from jax.experimental.pallas import tpu as pltpu

D_MODEL = 1024
CHUNK = 64
D_POOL = 512
POOL_WINDOWS = (2, 4, 8, 16)
POOL_GROUP = 128
POOL_STATE = 15
DN_HEADS = 4
DN_DK = 128
DN_DV = 128
DN_QK = DN_HEADS * DN_DK
DN_VW = DN_HEADS * DN_DV
CONV_W = 4
CONV_CH = 2 * DN_QK + DN_VW
OFF_U = 0
OFF_QKV = OFF_U + D_POOL
OFF_Z = OFF_QKV + CONV_CH
OFF_B = OFF_Z + DN_VW
OFF_A = OFF_B + DN_HEADS
OFF_G = OFF_A + DN_HEADS
PEER_HEADS = 8
PEER_KEYS = 128
PEER_DQ = 256
PEER_TOPK = 16
PEER_PICKS = PEER_HEADS * PEER_TOPK
EPS = 1e-6
PAST_LEN = 4096

LANES = 128
SUBLANES = 8
POOL_PREFIX_ROWS = 16
CONV_PREFIX_ROWS = 8
VMEM_LIMIT = 56 * 1024 * 1024

F32 = jnp.float32
BF16 = jnp.bfloat16
HIGHEST = lax.Precision.HIGHEST


def _mm(a, b):
    return jnp.dot(a.astype(BF16), b.astype(BF16), preferred_element_type=F32)


def _mm_f32(a, b):
    return jnp.dot(a, b, precision=HIGHEST, preferred_element_type=F32)


def _sigmoid(x):
    return 1.0 / (1.0 + jnp.exp(-x))


def _silu(x):
    return x * _sigmoid(x)


def _softplus(x):
    return jnp.maximum(x, 0.0) + jnp.log1p(jnp.exp(-jnp.abs(x)))


def _rms(x, g):
    return x * lax.rsqrt(jnp.mean(x * x, axis=-1, keepdims=True) + EPS) * g


def _front_kernel(pos0, x_ref, pool_pre_ref, conv_pre_ref, g_mix_ref, w_u_ref, w_qkv_ref, w_z_ref, w_ba_ref, w_g_ref,
                  w_grp_ref, pool_scale_ref, w_conv_ref, a_log_ref, dt_ref,
                  ya_ref, q_ref, k_ref, v_ref, z_ref, graw_ref, bg_ref, pool_tail_ref, conv_tail_ref,
                  carry_u, carry_c):
    sb, tl, _ = x_ref.shape
    li = pl.program_id(1)

    @pl.when(li == 0)
    def _():
        carry_u[...] = pool_pre_ref[...]
        carry_c[...] = conv_pre_ref[...]

    x = x_ref[...].reshape(sb * tl, D_MODEL)
    h = _rms(x, g_mix_ref[...]).astype(BF16)
    u = jnp.dot(h, w_u_ref[...], preferred_element_type=F32)
    qkv = jnp.dot(h, w_qkv_ref[...], preferred_element_type=F32)
    z_ref[...] = jnp.dot(h, w_z_ref[...], preferred_element_type=F32).reshape(sb, tl, DN_VW)
    graw_ref[...] = jnp.dot(h, w_g_ref[...], preferred_element_type=F32).reshape(sb, tl, 2 * D_MODEL)
    ba = jnp.dot(h, w_ba_ref[...], preferred_element_type=F32)
    lane = lax.broadcasted_iota(jnp.int32, ba.shape, 1)
    beta = _sigmoid(ba)
    g = -jnp.exp(a_log_ref[...]) * _softplus(ba + dt_ref[...])
    bg_ref[...] = jnp.where(lane < DN_HEADS, beta, g).reshape(sb, tl, LANES)

    row = lax.broadcasted_iota(jnp.int32, (tl, POOL_GROUP), 0)
    pos1 = pos0 + li * tl + row + 1
    w_conv = w_conv_ref[...]

    for s in range(sb):
        u_s = u[s * tl:(s + 1) * tl]
        ext = jnp.concatenate([carry_u[s], u_s], axis=0)
        mixed = []
        for gi, w in enumerate(POOL_WINDOWS):
            acc = ext[:, gi * POOL_GROUP:(gi + 1) * POOL_GROUP]
            span = 1
            while span < w:
                acc = acc + pltpu.roll(acc, span, axis=0)
                span *= 2
            win = acc[POOL_PREFIX_ROWS:]
            cnt = jnp.minimum(pos1, w).astype(F32)
            pooled = win / cnt - u_s[:, gi * POOL_GROUP:(gi + 1) * POOL_GROUP]
            mixed.append(_mm(pooled, w_grp_ref[gi]))
        ya_ref[s] = jnp.concatenate(mixed, axis=1) * pool_scale_ref[...]
        pool_tail_ref[s] = ext[tl:]
        carry_u[s] = ext[tl:]

        c_s = qkv[s * tl:(s + 1) * tl]
        cext = jnp.concatenate([carry_c[s], c_s], axis=0)
        y = c_s * w_conv[CONV_W - 1:CONV_W]
        for j in range(1, CONV_W):
            y = y + pltpu.roll(cext, j, axis=0)[CONV_PREFIX_ROWS:] * w_conv[CONV_W - 1 - j:CONV_W - j]
        y = _silu(y)
        conv_tail_ref[s] = cext[tl:]
        carry_c[s] = cext[tl:]
        for hh in range(DN_HEADS):
            sl = slice(hh * DN_DK, (hh + 1) * DN_DK)
            qh = y[:, sl]
            q_ref[s, :, sl] = qh * lax.rsqrt(jnp.sum(qh * qh, axis=-1, keepdims=True) + EPS)
            kh = y[:, DN_QK + hh * DN_DK:DN_QK + (hh + 1) * DN_DK]
            k_ref[s, :, sl] = kh * lax.rsqrt(jnp.sum(kh * kh, axis=-1, keepdims=True) + EPS)
        v_ref[s] = y[:, 2 * DN_QK:]


def _front(x, pool_pre, conv_pre, pos0, sb, tl, g_mix, w_u, w_qkv, w_z, w_ba, w_g, w_grp, pool_scale, w_conv8,
           a_log_pad, dt_pad):
    b, l, _ = x.shape
    grid = (b // sb, l // tl)
    tok = lambda width: pl.BlockSpec((sb, tl, width), lambda i, j: (i, j, 0))
    seq = lambda rows, width: pl.BlockSpec((sb, rows, width), lambda i, j: (i, 0, 0))
    full = lambda a: pl.BlockSpec(a.shape, lambda i, j: (0,) * a.ndim)
    weights = (g_mix, w_u, w_qkv, w_z, w_ba, w_g, w_grp, pool_scale, w_conv8, a_log_pad, dt_pad)
    out_shape = (
        jax.ShapeDtypeStruct((b, l, D_POOL), F32),
        jax.ShapeDtypeStruct((b, l, DN_QK), F32),
        jax.ShapeDtypeStruct((b, l, DN_QK), F32),
        jax.ShapeDtypeStruct((b, l, DN_VW), F32),
        jax.ShapeDtypeStruct((b, l, DN_VW), F32),
        jax.ShapeDtypeStruct((b, l, 2 * D_MODEL), F32),
        jax.ShapeDtypeStruct((b, l, LANES), F32),
        jax.ShapeDtypeStruct((b, POOL_PREFIX_ROWS, D_POOL), F32),
        jax.ShapeDtypeStruct((b, CONV_PREFIX_ROWS, CONV_CH), F32),
    )
    return pl.pallas_call(
        functools.partial(_front_kernel, pos0),
        out_shape=out_shape,
        grid=grid,
        in_specs=[tok(D_MODEL), seq(POOL_PREFIX_ROWS, D_POOL), seq(CONV_PREFIX_ROWS, CONV_CH)] + [full(a) for a in weights],
        out_specs=(tok(D_POOL), tok(DN_QK), tok(DN_QK), tok(DN_VW), tok(DN_VW), tok(2 * D_MODEL), tok(LANES),
                   seq(POOL_PREFIX_ROWS, D_POOL), seq(CONV_PREFIX_ROWS, CONV_CH)),
        scratch_shapes=[pltpu.VMEM((sb, POOL_PREFIX_ROWS, D_POOL), F32), pltpu.VMEM((sb, CONV_PREFIX_ROWS, CONV_CH), F32)],
        compiler_params=pltpu.CompilerParams(dimension_semantics=("arbitrary", "arbitrary"), vmem_limit_bytes=VMEM_LIMIT),
        name="mixer_front",
    )(x, pool_pre, conv_pre, *weights)


_NT = (((1,), (1,)), ((), ()))
_TN = (((0,), (0,)), ((), ()))


def _dot_bf16(a, b, dims):
    return lax.dot_general(a.astype(BF16), b.astype(BF16), dims, preferred_element_type=F32)


def _mm_nt(a, b):
    return _dot_bf16(a, b, _NT)


def _mm_tn(a, b):
    return _dot_bf16(a, b, _TN)


def _delta_prep_kernel(q_ref, k_ref, v_ref, bg_ref, uv_ref, wk_ref, qd_ref, kd_ref, qk_ref, gl_ref):
    sb = q_ref.shape[0]
    ncb = q_ref.shape[1] // CHUNK
    row = lax.broadcasted_iota(jnp.int32, (CHUNK, CHUNK), 0)
    col = lax.broadcasted_iota(jnp.int32, (CHUNK, CHUNK), 1)
    incl = row >= col
    strict = row > col
    tri = incl.astype(F32)
    xs, pw = [], []
    for s in range(sb):
        for c in range(ncb):
            rows = slice(c * CHUNK, (c + 1) * CHUNK)
            bg = bg_ref[s, rows, :]
            gc_all = _mm_f32(tri, bg)
            gc_rows = gc_all.T
            gl_ref[s, c * SUBLANES:(c + 1) * SUBLANES, :] = gc_all[CHUNK - SUBLANES:, :]
            for hh in range(DN_HEADS):
                sl = slice(hh * DN_DK, (hh + 1) * DN_DK)
                q = q_ref[s, rows, sl] * (DN_DK ** -0.5)
                k = k_ref[s, rows, sl]
                beta = bg[:, hh:hh + 1]
                gcol = gc_all[:, DN_HEADS + hh:DN_HEADS + hh + 1]
                grow = gc_rows[DN_HEADS + hh:DN_HEADS + hh + 1, :]
                decay = jnp.exp(jnp.where(incl, gcol - grow, -jnp.inf))
                egc = jnp.exp(gcol)
                kb = k * beta
                pw.append(-jnp.where(strict, _mm_nt(kb, k) * decay, 0.0))
                xs.append(jnp.concatenate([v_ref[s, rows, sl] * beta, kb * egc], axis=1))
                qk_ref[s, rows, hh * CHUNK:(hh + 1) * CHUNK] = _mm_nt(q, k) * decay
                gl = gcol[CHUNK - 1:CHUNK, :]
                qd_ref[s, rows, sl] = q * egc
                kd_ref[s, rows, sl] = k * jnp.exp(gl - gcol)
    span = 1
    while True:
        xs = [x + _mm(p, x) for p, x in zip(pw, xs)]
        span *= 2
        if span >= CHUNK:
            break
        pw = [_mm(p, p) for p in pw]
    i = 0
    for s in range(sb):
        for c in range(ncb):
            rows = slice(c * CHUNK, (c + 1) * CHUNK)
            for hh in range(DN_HEADS):
                sl = slice(hh * DN_DK, (hh + 1) * DN_DK)
                uv_ref[s, rows, sl] = xs[i][:, :DN_DV]
                wk_ref[s, rows, sl] = xs[i][:, DN_DV:]
                i += 1


def _delta_scan_kernel(uv_ref, wk_ref, qd_ref, kd_ref, qk_ref, gl_ref, z_ref, s0_ref, g_out_ref, yb_ref, s_out_ref, s_scr):
    ci = pl.program_id(1)
    bb = uv_ref.shape[0]

    @pl.when(ci == 0)
    def _():
        s_scr[...] = s0_ref[...]

    g_out = g_out_ref[...]
    probs = [(b, hh) for b in range(bb) for hh in range(DN_HEADS)]
    lanes = lambda hh: slice(hh * DN_DK, (hh + 1) * DN_DK)
    s_old = [s_scr[b, hh] for b, hh in probs]
    v_new = [uv_ref[b, :, lanes(hh)] - _mm(wk_ref[b, :, lanes(hh)], s) for (b, hh), s in zip(probs, s_old)]
    o_state = [_mm(qd_ref[b, :, lanes(hh)], s) for (b, hh), s in zip(probs, s_old)]
    for (b, hh), s, vn, os_ in zip(probs, s_old, v_new, o_state):
        o = os_ + _mm(qk_ref[b, :, hh * CHUNK:(hh + 1) * CHUNK], vn)
        egl = jnp.exp(gl_ref[b, SUBLANES - 1:SUBLANES, DN_HEADS + hh:DN_HEADS + hh + 1])
        s_new = s * egl + _mm_tn(kd_ref[b, :, lanes(hh)], vn)
        s_scr[b, hh] = s_new
        s_out_ref[b, hh] = s_new
        yb_ref[b, :, lanes(hh)] = _rms(o, g_out) * _silu(z_ref[b, :, lanes(hh)])


DELTA_PREP_CHUNKS = 4
DELTA_SCAN_SEQS = 8


def _delta(q, k, v, z, bg, s0, g_out):
    b, l, _ = q.shape
    nc = l // CHUNK
    ncb = min(DELTA_PREP_CHUNKS, nc)
    sb = min(b, DELTA_PREP_CHUNKS // ncb)
    rows = ncb * CHUNK
    tok = lambda width: pl.BlockSpec((sb, rows, width), lambda i, j: (i, j, 0))
    glspec = pl.BlockSpec((sb, ncb * SUBLANES, LANES), lambda i, j: (i, j, 0))
    wide = lambda width: jax.ShapeDtypeStruct((b, l, width), F32)
    uv, wk, qd, kd, qk, gl = pl.pallas_call(
        _delta_prep_kernel,
        out_shape=(wide(DN_VW), wide(DN_QK), wide(DN_QK), wide(DN_QK), wide(DN_HEADS * CHUNK),
                   jax.ShapeDtypeStruct((b, nc * SUBLANES, LANES), F32)),
        grid=(b // sb, nc // ncb),
        in_specs=[tok(DN_QK), tok(DN_QK), tok(DN_VW), tok(LANES)],
        out_specs=(tok(DN_VW), tok(DN_QK), tok(DN_QK), tok(DN_QK), tok(DN_HEADS * CHUNK), glspec),
        compiler_params=pltpu.CompilerParams(dimension_semantics=("arbitrary", "arbitrary"), vmem_limit_bytes=VMEM_LIMIT),
        name="delta_prep",
    )(q, k, v, bg)

    bb = min(DELTA_SCAN_SEQS, b)
    ctok = lambda width: pl.BlockSpec((bb, CHUNK, width), lambda i, j: (i, j, 0))
    st = pl.BlockSpec((bb, DN_HEADS, DN_DK, DN_DV), lambda i, j: (i, 0, 0, 0))
    return pl.pallas_call(
        _delta_scan_kernel,
        out_shape=(wide(DN_VW), jax.ShapeDtypeStruct((b, DN_HEADS, DN_DK, DN_DV), F32)),
        grid=(b // bb, nc),
        in_specs=[ctok(DN_VW), ctok(DN_QK), ctok(DN_QK), ctok(DN_QK), ctok(DN_HEADS * CHUNK),
                  pl.BlockSpec((bb, SUBLANES, LANES), lambda i, j: (i, j, 0)), ctok(DN_VW), st,
                  pl.BlockSpec((1, DN_DV), lambda i, j: (0, 0))],
        out_specs=(ctok(DN_VW), st),
        scratch_shapes=[pltpu.VMEM((bb, DN_HEADS, DN_DK, DN_DV), F32)],
        compiler_params=pltpu.CompilerParams(dimension_semantics=("arbitrary", "arbitrary"), vmem_limit_bytes=VMEM_LIMIT),
        name="delta_scan",
    )(uv, wk, qd, kd, qk, gl, z, s0, g_out)


def _merge_kernel(x_ref, ya_ref, yb_ref, graw_ref, w_up_pool_ref, w_up_dn_ref, w_out_ref, g_ffn_ref, x2_ref, xn_ref):
    graw = graw_ref[...]
    ga = _sigmoid(graw[:, :D_MODEL])
    gb = _sigmoid(graw[:, D_MODEL:])
    merged = ga * _mm(ya_ref[...], w_up_pool_ref[...]) + gb * _mm(yb_ref[...], w_up_dn_ref[...])
    x2 = x_ref[...] + _mm(merged, w_out_ref[...])
    x2_ref[...] = x2
    xn_ref[...] = _rms(x2, g_ffn_ref[...])


def _merge(x, ya, yb, graw, w_up_pool, w_up_dn, w_out, g_ffn, tm):
    t = x.shape[0]
    tok = lambda width: pl.BlockSpec((tm, width), lambda i: (i, 0))
    full = lambda a: pl.BlockSpec(a.shape, lambda i: (0,) * a.ndim)
    weights = (w_up_pool, w_up_dn, w_out, g_ffn)
    return pl.pallas_call(
        _merge_kernel,
        out_shape=(jax.ShapeDtypeStruct((t, D_MODEL), F32), jax.ShapeDtypeStruct((t, D_MODEL), F32)),
        grid=(t // tm,),
        in_specs=[tok(D_MODEL), tok(D_POOL), tok(DN_VW), tok(2 * D_MODEL)] + [full(a) for a in weights],
        out_specs=(tok(D_MODEL), tok(D_MODEL)),
        compiler_params=pltpu.CompilerParams(dimension_semantics=("arbitrary",), vmem_limit_bytes=VMEM_LIMIT),
        name="branch_merge",
    )(x, ya, yb, graw, *weights)


def _top16_rows(s, ids=None):
    if ids is None:
        ids = lax.broadcasted_iota(jnp.int32, s.shape, 0)
    ids = ids.astype(F32)
    vals, idxs = [], []
    for _ in range(PEER_TOPK):
        m = jnp.max(s, axis=0, keepdims=True)
        idx = jnp.min(jnp.where(s == m, ids, jnp.inf), axis=0, keepdims=True)
        vals.append(m)
        idxs.append(idx)
        s = jnp.where(ids == idx, -jnp.inf, s)
    return jnp.concatenate(vals, axis=0), jnp.concatenate(idxs, axis=0).astype(jnp.int32)


def _pair_candidates(v1, v2):
    tokens = v1.shape[1]
    sub = lax.broadcasted_iota(jnp.int32, (SUBLANES, tokens), 0)
    vals = [v1[0:1] + v2[0:SUBLANES], v1[0:1] + v2[SUBLANES:]]
    ids = [sub, sub + SUBLANES]
    for a in range(1, SUBLANES):
        vals.append(v1[a:a + 1] + v2[0:SUBLANES])
        ids.append(sub + a * PEER_TOPK)
    vals.append(v1[SUBLANES:] + v2[0:1])
    ids.append((sub + SUBLANES) * PEER_TOPK)
    return jnp.concatenate(vals, axis=0), jnp.concatenate(ids, axis=0)


def _take_rows(table, idx):
    out = jnp.zeros_like(table)
    for a in range(PEER_TOPK):
        out = jnp.where(idx == a, table[a:a + 1, :], out)
    return out


def _route_kernel(xn_ref, wq_ref, keys_ref, experts_ref, gates_ref):
    half = PEER_DQ // 2
    q = jnp.dot(xn_ref[...].astype(BF16), wq_ref[...], preferred_element_type=F32).astype(BF16)
    experts, gates = [], []
    for hh in range(PEER_HEADS):
        q1 = q[:, hh * PEER_DQ:hh * PEER_DQ + half]
        q2 = q[:, hh * PEER_DQ + half:(hh + 1) * PEER_DQ]
        nt = (((1,), (1,)), ((), ()))
        s1 = lax.dot_general(keys_ref[0, hh], q1, nt, preferred_element_type=F32)
        s2 = lax.dot_general(keys_ref[1, hh], q2, nt, preferred_element_type=F32)
        v1, i1 = _top16_rows(s1)
        v2, i2 = _top16_rows(s2)
        cv, ci = _top16_rows(*_pair_candidates(v1, v2))
        e1 = _take_rows(i1, ci // PEER_TOPK)
        e2 = _take_rows(i2, ci % PEER_TOPK)
        experts.append(e1 * PEER_KEYS + e2)
        ex = jnp.exp(cv - cv[0:1, :])
        gates.append(ex / jnp.sum(ex, axis=0, keepdims=True))
    experts_ref[...] = jnp.concatenate(experts, axis=0).T
    gates_ref[...] = jnp.concatenate(gates, axis=0).T


def _route(xn, wq, keys, tm):
    t = xn.shape[0]
    return pl.pallas_call(
        _route_kernel,
        out_shape=(jax.ShapeDtypeStruct((t, PEER_PICKS), jnp.int32), jax.ShapeDtypeStruct((t, PEER_PICKS), F32)),
        grid=(t // tm,),
        in_specs=[pl.BlockSpec((tm, D_MODEL), lambda i: (i, 0)),
                  pl.BlockSpec(wq.shape, lambda i: (0, 0)),
                  pl.BlockSpec(keys.shape, lambda i: (0, 0, 0, 0))],
        out_specs=(pl.BlockSpec((tm, PEER_PICKS), lambda i: (i, 0)), pl.BlockSpec((tm, PEER_PICKS), lambda i: (i, 0))),
        compiler_params=pltpu.CompilerParams(dimension_semantics=("arbitrary",), vmem_limit_bytes=VMEM_LIMIT),
        name="peer_route",
    )(xn, wq, keys)


EXPERT_TOKENS = 32
EXPERT_ISSUE_TOKENS = 28


def _experts_kernel(idx_ref, idx_next_ref, gates_ref, xn_ref, x2_ref, g_final_ref, tab_ref, out_ref, buf_a, buf_b, cb, sem):
    bufs = (buf_a, buf_b)
    i = pl.program_id(0)
    n = pl.num_programs(0)
    tb = EXPERT_TOKENS
    lane = lax.broadcasted_iota(jnp.int32, (PEER_HEADS, PEER_TOPK), 1)
    g_final = g_final_ref[...]

    def issue_picks(ref, sl, lo, hi):
        for f in range(lo, min(hi, tb * PEER_PICKS)):
            t, h, k = f // PEER_PICKS, (f // PEER_TOPK) % PEER_HEADS, f % PEER_TOPK
            e = ref[0, t, h * PEER_TOPK + k]
            pltpu.async_copy(tab_ref.at[e], bufs[sl].at[t, k, h], sem.at[sl], priority=k % 2)

    def wait_slot(sl):
        for t in range(tb):
            pltpu.make_async_copy(bufs[sl].at[t], bufs[sl].at[t], sem.at[sl]).wait()

    def gate_coefficients(t, sl):
        x = xn_ref[t]
        sub = lax.broadcasted_iota(jnp.int32, (PEER_HEADS, PEER_TOPK), 0)
        act = jnp.zeros((PEER_HEADS, PEER_TOPK), F32)
        for h in range(PEER_HEADS):
            part = jnp.zeros((SUBLANES, PEER_TOPK), F32)
            for k in range(PEER_TOPK):
                u = lax.bitcast_convert_type(bufs[sl][t, k, h] << 16, F32)
                part = jnp.where(lane == k, jnp.sum(u * x, axis=-1, keepdims=True), part)
            act = jnp.where(sub == h, jnp.sum(part, axis=0, keepdims=True), act)
        coef = gates_ref[t] * (0.5 * act * (1.0 + lax.erf(act * (0.5 ** 0.5))))
        for k in range(PEER_TOPK):
            cb[t, k] = jnp.broadcast_to(coef[:, k:k + 1], (PEER_HEADS, LANES))

    def mix_token(t, sl):
        accs = [jnp.zeros((SUBLANES, LANES), F32) for _ in range(4)]
        for k in range(PEER_TOPK):
            for h in range(PEER_HEADS):
                c = jnp.broadcast_to(cb[t, k, h:h + 1, :], (SUBLANES, LANES))
                v = lax.bitcast_convert_type(bufs[sl][t, k, h] & jnp.uint32(0xFFFF0000), F32)
                accs[h % 4] = accs[h % 4] + c * v
        z = x2_ref[t] + ((accs[0] + accs[1]) + (accs[2] + accs[3]))
        ms = jnp.sum(jnp.sum(z * z, axis=-1, keepdims=True), axis=0, keepdims=True) * (1.0 / D_MODEL)
        out_ref[t] = z * lax.rsqrt(ms + EPS) * g_final

    @pl.when(i == 0)
    def _():
        issue_picks(idx_ref, 0, 0, tb * PEER_PICKS)

    per_token = -(-tb * PEER_PICKS // EXPERT_ISSUE_TOKENS)

    def step(sl):
        wait_slot(sl)
        gate_coefficients(0, sl)
        for t in range(tb):
            @pl.when(i + t >= 0)
            def _():
                issue_picks(idx_next_ref, 1 - sl, t * per_token, (t + 1) * per_token)
                if t + 1 < tb:
                    gate_coefficients(t + 1, sl)
                mix_token(t, sl)

        @pl.when(i == n - 1)
        def _():
            wait_slot(1 - sl)

    @pl.when(i % 2 == 0)
    def _():
        step(0)

    @pl.when(i % 2 == 1)
    def _():
        step(1)


def _experts(experts, gates, xn, x2, g_final, table):
    t = xn.shape[0]
    tb = EXPERT_TOKENS
    nb = t // tb
    idx = experts.reshape(nb, tb, PEER_PICKS)
    gates = gates.reshape(t, PEER_HEADS, PEER_TOPK)
    tiles = lambda a: a.reshape(a.shape[0], SUBLANES, LANES)
    tok = pl.BlockSpec((tb, SUBLANES, LANES), lambda i: (i, 0, 0))
    out = pl.pallas_call(
        _experts_kernel,
        out_shape=jax.ShapeDtypeStruct((t, SUBLANES, LANES), F32),
        grid=(nb,),
        in_specs=[pl.BlockSpec((1, tb, PEER_PICKS), lambda i: (i, 0, 0), memory_space=pltpu.SMEM),
                  pl.BlockSpec((1, tb, PEER_PICKS), lambda i: (jnp.minimum(i + 1, nb - 1), 0, 0), memory_space=pltpu.SMEM),
                  pl.BlockSpec((tb, PEER_HEADS, PEER_TOPK), lambda i: (i, 0, 0)), tok, tok,
                  pl.BlockSpec((SUBLANES, LANES), lambda i: (0, 0)),
                  pl.BlockSpec(memory_space=pl.ANY)],
        out_specs=tok,
        scratch_shapes=[pltpu.VMEM((tb, PEER_TOPK, PEER_HEADS, SUBLANES, LANES), jnp.uint32),
                        pltpu.VMEM((tb, PEER_TOPK, PEER_HEADS, SUBLANES, LANES), jnp.uint32),
                        pltpu.VMEM((tb, PEER_TOPK, PEER_HEADS, LANES), F32),
                        pltpu.SemaphoreType.DMA((2,))],
        compiler_params=pltpu.CompilerParams(dimension_semantics=("arbitrary",), vmem_limit_bytes=VMEM_LIMIT),
        name="peer_experts",
    )(idx, idx, gates, tiles(xn), tiles(x2), g_final.reshape(SUBLANES, LANES), table)
    return out.reshape(t, D_MODEL)


def _pack_rows(u, v):
    half = lambda a: lax.bitcast_convert_type(a.astype(BF16), jnp.uint16).astype(jnp.uint32)
    return (half(u) | (half(v) << 16)).reshape(-1, SUBLANES, LANES)


def _pad_rows_front(a, rows):
    return jnp.pad(a, ((0, 0), (rows - a.shape[1], 0), (0, 0)))


MERGE_ROWS = 512
ROUTE_ROWS = 256
FRONT_ROWS = 512


def _group(x, pool_prev, conv_prev, s_prev, pos0, p):
    b, l, _ = x.shape
    tl = min(FRONT_ROWS, l)
    sb = min(b, FRONT_ROWS // tl)
    tm_merge = min(MERGE_ROWS, b * l)
    tm_route = min(ROUTE_ROWS, b * l)
    ya, q, k, v, z, graw, bg, pool_tail, conv_tail = _front(
        x, _pad_rows_front(pool_prev, POOL_PREFIX_ROWS), _pad_rows_front(conv_prev, CONV_PREFIX_ROWS), pos0, sb, tl,
        p["g_mix"], p["w_u"], p["w_qkv"], p["w_z"], p["w_ba"], p["w_g"], p["w_grp"], p["pool_scale"], p["w_conv8"],
        p["a_log_pad"], p["dt_pad"])
    yb, s_new = _delta(q, k, v, z, bg, s_prev, p["g_dn_out"])
    t = b * l
    flat = lambda a: a.reshape(t, a.shape[-1])
    x2, xn = _merge(flat(x), flat(ya), flat(yb), flat(graw), p["w_up_pool"], p["w_up_dn"], p["w_out"], p["g_ffn"], tm_merge)
    experts, gates = _route(xn, p["w_peer_q"], p["peer_keys"], tm_route)
    y = _experts(experts, gates, xn, x2, p["g_final"], p["peer_table"])
    return (y.reshape(b, l, D_MODEL), pool_tail[:, POOL_PREFIX_ROWS - POOL_STATE:],
            conv_tail[:, CONV_PREFIX_ROWS - (CONV_W - 1):], s_new)


def kernel(x_prompt, x_sample, cache_pool, state_dn_conv, state_dn, g_mix, w_in, w_pool_grp, pool_scale, w_conv, a_log,
           dt_bias, g_dn_out, w_up_pool, w_up_dn, w_out, g_ffn, w_peer_q, peer_sub_keys, peer_u, peer_v, g_final):
    depth = w_in.shape[0]
    assert depth == 1
    bp = x_prompt.shape[0]
    lane_pad = lambda a, off: jnp.pad(a.astype(F32)[None, :], ((0, 0), (off, LANES - off - a.shape[0])))
    w = w_in[0]
    params = {
        "g_mix": g_mix[0][None, :],
        "w_u": w[:, OFF_U:OFF_QKV].astype(BF16),
        "w_qkv": w[:, OFF_QKV:OFF_Z].astype(BF16),
        "w_z": w[:, OFF_Z:OFF_B].astype(BF16),
        "w_ba": jnp.pad(w[:, OFF_B:OFF_G], ((0, 0), (0, LANES - 2 * DN_HEADS))).astype(BF16),
        "w_g": w[:, OFF_G:].astype(BF16),
        "w_grp": w_pool_grp[0].astype(BF16),
        "pool_scale": pool_scale[0][None, :],
        "w_conv8": jnp.pad(w_conv[0], ((0, SUBLANES - CONV_W), (0, 0))),
        "a_log_pad": lane_pad(a_log[0], DN_HEADS),
        "dt_pad": lane_pad(dt_bias[0], DN_HEADS),
        "g_dn_out": g_dn_out[0][None, :],
        "w_up_pool": w_up_pool[0].astype(BF16),
        "w_up_dn": w_up_dn[0].astype(BF16),
        "w_out": w_out[0].astype(BF16),
        "g_ffn": g_ffn[0][None, :],
        "w_peer_q": w_peer_q[0].reshape(D_MODEL, PEER_HEADS * PEER_DQ).astype(BF16),
        "peer_keys": peer_sub_keys[0].astype(BF16),
        "peer_table": _pack_rows(peer_u[0], peer_v[0]),
        "g_final": g_final[None, :],
    }
    zeros = lambda *shape: jnp.zeros(shape, F32)
    yp, pool_p, conv_p, dn_p = _group(
        x_prompt, zeros(bp, POOL_STATE, D_POOL), zeros(bp, CONV_W - 1, CONV_CH), zeros(bp, DN_HEADS, DN_DK, DN_DV),
        0, params)
    ys, pool_s, conv_s, dn_s = _group(
        x_sample, cache_pool[0], state_dn_conv[0], state_dn[0].astype(F32),
        PAST_LEN, params)
    return (yp, ys, pool_p[None], conv_p[None], dn_p[None].astype(state_dn.dtype),
            pool_s[None], conv_s[None], dn_s[None].astype(state_dn.dtype))
```

```python
import functools

import jax
import jax.numpy as jnp
from jax import lax
from jax.experimental import pallas as pl
from jax.experimental.pallas import tpu as pltpu

D_MODEL = 1024
CHUNK = 64
D_POOL = 512
POOL_WINDOWS = (2, 4, 8, 16)
POOL_GROUP = 128
POOL_STATE = 15
DN_HEADS = 4
DN_DK = 128
DN_DV = 128
DN_QK = DN_HEADS * DN_DK
DN_VW = DN_HEADS * DN_DV
CONV_W = 4
CONV_CH = 2 * DN_QK + DN_VW
OFF_U = 0
OFF_QKV = OFF_U + D_POOL
OFF_Z = OFF_QKV + CONV_CH
OFF_B = OFF_Z + DN_VW
OFF_A = OFF_B + DN_HEADS
OFF_G = OFF_A + DN_HEADS
PEER_HEADS = 8
PEER_KEYS = 128
PEER_DQ = 256
PEER_TOPK = 16
PEER_PICKS = PEER_HEADS * PEER_TOPK
EPS = 1e-6
PAST_LEN = 4096

LANES = 128
SUBLANES = 8
POOL_PREFIX_ROWS = 16
CONV_PREFIX_ROWS = 8
VMEM_LIMIT = 56 * 1024 * 1024

F32 = jnp.float32
BF16 = jnp.bfloat16
HIGHEST = lax.Precision.HIGHEST


def _mm(a, b):
    return jnp.dot(a.astype(BF16), b.astype(BF16), preferred_element_type=F32)


def _mm_f32(a, b):
    return jnp.dot(a, b, precision=HIGHEST, preferred_element_type=F32)


def _sigmoid(x):
    return 1.0 / (1.0 + jnp.exp(-x))


def _silu(x):
    return x * _sigmoid(x)


def _softplus(x):
    return jnp.maximum(x, 0.0) + jnp.log1p(jnp.exp(-jnp.abs(x)))


def _rms(x, g):
    return x * lax.rsqrt(jnp.mean(x * x, axis=-1, keepdims=True) + EPS) * g


def _front_kernel(pos0, x_ref, pool_pre_ref, conv_pre_ref, g_mix_ref, w_u_ref, w_qkv_ref, w_z_ref, w_ba_ref, w_g_ref,
                  w_grp_ref, pool_scale_ref, w_conv_ref, a_log_ref, dt_ref,
                  ya_ref, q_ref, k_ref, v_ref, z_ref, graw_ref, bg_ref, pool_tail_ref, conv_tail_ref,
                  carry_u, carry_c):
    sb, tl, _ = x_ref.shape
    li = pl.program_id(1)

    @pl.when(li == 0)
    def _():
        carry_u[...] = pool_pre_ref[...]
        carry_c[...] = conv_pre_ref[...]

    x = x_ref[...].reshape(sb * tl, D_MODEL)
    h = _rms(x, g_mix_ref[...]).astype(BF16)
    u = jnp.dot(h, w_u_ref[...], preferred_element_type=F32)
    qkv = jnp.dot(h, w_qkv_ref[...], preferred_element_type=F32)
    z_ref[...] = jnp.dot(h, w_z_ref[...], preferred_element_type=F32).reshape(sb, tl, DN_VW)
    graw_ref[...] = jnp.dot(h, w_g_ref[...], preferred_element_type=F32).reshape(sb, tl, 2 * D_MODEL)
    ba = jnp.dot(h, w_ba_ref[...], preferred_element_type=F32)
    lane = lax.broadcasted_iota(jnp.int32, ba.shape, 1)
    beta = _sigmoid(ba)
    g = -jnp.exp(a_log_ref[...]) * _softplus(ba + dt_ref[...])
    bg_ref[...] = jnp.where(lane < DN_HEADS, beta, g).reshape(sb, tl, LANES)

    row = lax.broadcasted_iota(jnp.int32, (tl, POOL_GROUP), 0)
    pos1 = pos0 + li * tl + row + 1
    w_conv = w_conv_ref[...]

    for s in range(sb):
        u_s = u[s * tl:(s + 1) * tl]
        ext = jnp.concatenate([carry_u[s], u_s], axis=0)
        mixed = []
        for gi, w in enumerate(POOL_WINDOWS):
            acc = ext[:, gi * POOL_GROUP:(gi + 1) * POOL_GROUP]
            span = 1
            while span < w:
                acc = acc + pltpu.roll(acc, span, axis=0)
                span *= 2
            win = acc[POOL_PREFIX_ROWS:]
            cnt = jnp.minimum(pos1, w).astype(F32)
            pooled = win / cnt - u_s[:, gi * POOL_GROUP:(gi + 1) * POOL_GROUP]
            mixed.append(_mm(pooled, w_grp_ref[gi]))
        ya_ref[s] = jnp.concatenate(mixed, axis=1) * pool_scale_ref[...]
        pool_tail_ref[s] = ext[tl:]
        carry_u[s] = ext[tl:]

        c_s = qkv[s * tl:(s + 1) * tl]
        cext = jnp.concatenate([carry_c[s], c_s], axis=0)
        y = c_s * w_conv[CONV_W - 1:CONV_W]
        for j in range(1, CONV_W):
            y = y + pltpu.roll(cext, j, axis=0)[CONV_PREFIX_ROWS:] * w_conv[CONV_W - 1 - j:CONV_W - j]
        y = _silu(y)
        conv_tail_ref[s] = cext[tl:]
        carry_c[s] = cext[tl:]
        for hh in range(DN_HEADS):
            sl = slice(hh * DN_DK, (hh + 1) * DN_DK)
            qh = y[:, sl]
            q_ref[s, :, sl] = qh * lax.rsqrt(jnp.sum(qh * qh, axis=-1, keepdims=True) + EPS)
            kh = y[:, DN_QK + hh * DN_DK:DN_QK + (hh + 1) * DN_DK]
            k_ref[s, :, sl] = kh * lax.rsqrt(jnp.sum(kh * kh, axis=-1, keepdims=True) + EPS)
        v_ref[s] = y[:, 2 * DN_QK:]


def _front(x, pool_pre, conv_pre, pos0, sb, tl, g_mix, w_u, w_qkv, w_z, w_ba, w_g, w_grp, pool_scale, w_conv8,
           a_log_pad, dt_pad):
    b, l, _ = x.shape
    grid = (b // sb, l // tl)
    tok = lambda width: pl.BlockSpec((sb, tl, width), lambda i, j: (i, j, 0))
    seq = lambda rows, width: pl.BlockSpec((sb, rows, width), lambda i, j: (i, 0, 0))
    full = lambda a: pl.BlockSpec(a.shape, lambda i, j: (0,) * a.ndim)
    weights = (g_mix, w_u, w_qkv, w_z, w_ba, w_g, w_grp, pool_scale, w_conv8, a_log_pad, dt_pad)
    out_shape = (
        jax.ShapeDtypeStruct((b, l, D_POOL), F32),
        jax.ShapeDtypeStruct((b, l, DN_QK), F32),
        jax.ShapeDtypeStruct((b, l, DN_QK), F32),
        jax.ShapeDtypeStruct((b, l, DN_VW), F32),
        jax.ShapeDtypeStruct((b, l, DN_VW), F32),
        jax.ShapeDtypeStruct((b, l, 2 * D_MODEL), F32),
        jax.ShapeDtypeStruct((b, l, LANES), F32),
        jax.ShapeDtypeStruct((b, POOL_PREFIX_ROWS, D_POOL), F32),
        jax.ShapeDtypeStruct((b, CONV_PREFIX_ROWS, CONV_CH), F32),
    )
    return pl.pallas_call(
        functools.partial(_front_kernel, pos0),
        out_shape=out_shape,
        grid=grid,
        in_specs=[tok(D_MODEL), seq(POOL_PREFIX_ROWS, D_POOL), seq(CONV_PREFIX_ROWS, CONV_CH)] + [full(a) for a in weights],
        out_specs=(tok(D_POOL), tok(DN_QK), tok(DN_QK), tok(DN_VW), tok(DN_VW), tok(2 * D_MODEL), tok(LANES),
                   seq(POOL_PREFIX_ROWS, D_POOL), seq(CONV_PREFIX_ROWS, CONV_CH)),
        scratch_shapes=[pltpu.VMEM((sb, POOL_PREFIX_ROWS, D_POOL), F32), pltpu.VMEM((sb, CONV_PREFIX_ROWS, CONV_CH), F32)],
        compiler_params=pltpu.CompilerParams(dimension_semantics=("arbitrary", "arbitrary"), vmem_limit_bytes=VMEM_LIMIT),
        name="mixer_front",
    )(x, pool_pre, conv_pre, *weights)


_NT = (((1,), (1,)), ((), ()))
_TN = (((0,), (0,)), ((), ()))


def _dot_bf16(a, b, dims):
    return lax.dot_general(a.astype(BF16), b.astype(BF16), dims, preferred_element_type=F32)


def _mm_nt(a, b):
    return _dot_bf16(a, b, _NT)


def _mm_tn(a, b):
    return _dot_bf16(a, b, _TN)


def _delta_prep_kernel(q_ref, k_ref, v_ref, bg_ref, uv_ref, wk_ref, qd_ref, kd_ref, qk_ref, gl_ref):
    sb = q_ref.shape[0]
    ncb = q_ref.shape[1] // CHUNK
    row = lax.broadcasted_iota(jnp.int32, (CHUNK, CHUNK), 0)
    col = lax.broadcasted_iota(jnp.int32, (CHUNK, CHUNK), 1)
    incl = row >= col
    strict = row > col
    tri = incl.astype(F32)
    xs, pw = [], []
    for s in range(sb):
        for c in range(ncb):
            rows = slice(c * CHUNK, (c + 1) * CHUNK)
            bg = bg_ref[s, rows, :]
            gc_all = _mm_f32(tri, bg)
            gc_rows = gc_all.T
            gl_ref[s, c * SUBLANES:(c + 1) * SUBLANES, :] = gc_all[CHUNK - SUBLANES:, :]
            for hh in range(DN_HEADS):
                sl = slice(hh * DN_DK, (hh + 1) * DN_DK)
                q = q_ref[s, rows, sl] * (DN_DK ** -0.5)
                k = k_ref[s, rows, sl]
                beta = bg[:, hh:hh + 1]
                gcol = gc_all[:, DN_HEADS + hh:DN_HEADS + hh + 1]
                grow = gc_rows[DN_HEADS + hh:DN_HEADS + hh + 1, :]
                decay = jnp.exp(jnp.where(incl, gcol - grow, -jnp.inf))
                egc = jnp.exp(gcol)
                kb = k * beta
                pw.append(-jnp.where(strict, _mm_nt(kb, k) * decay, 0.0))
                xs.append(jnp.concatenate([v_ref[s, rows, sl] * beta, kb * egc], axis=1))
                qk_ref[s, rows, hh * CHUNK:(hh + 1) * CHUNK] = _mm_nt(q, k) * decay
                gl = gcol[CHUNK - 1:CHUNK, :]
                qd_ref[s, rows, sl] = q * egc
                kd_ref[s, rows, sl] = k * jnp.exp(gl - gcol)
    span = 1
    while True:
        xs = [x + _mm(p, x) for p, x in zip(pw, xs)]
        span *= 2
        if span >= CHUNK:
            break
        pw = [_mm(p, p) for p in pw]
    i = 0
    for s in range(sb):
        for c in range(ncb):
            rows = slice(c * CHUNK, (c + 1) * CHUNK)
            for hh in range(DN_HEADS):
                sl = slice(hh * DN_DK, (hh + 1) * DN_DK)
                uv_ref[s, rows, sl] = xs[i][:, :DN_DV]
                wk_ref[s, rows, sl] = xs[i][:, DN_DV:]
                i += 1


def _delta_scan_kernel(uv_ref, wk_ref, qd_ref, kd_ref, qk_ref, gl_ref, z_ref, s0_ref, g_out_ref, yb_ref, s_out_ref, s_scr):
    ci = pl.program_id(1)
    bb = uv_ref.shape[0]

    @pl.when(ci == 0)
    def _():
        s_scr[...] = s0_ref[...]

    g_out = g_out_ref[...]
    probs = [(b, hh) for b in range(bb) for hh in range(DN_HEADS)]
    lanes = lambda hh: slice(hh * DN_DK, (hh + 1) * DN_DK)
    s_old = [s_scr[b, hh] for b, hh in probs]
    v_new = [uv_ref[b, :, lanes(hh)] - _mm(wk_ref[b, :, lanes(hh)], s) for (b, hh), s in zip(probs, s_old)]
    o_state = [_mm(qd_ref[b, :, lanes(hh)], s) for (b, hh), s in zip(probs, s_old)]
    for (b, hh), s, vn, os_ in zip(probs, s_old, v_new, o_state):
        o = os_ + _mm(qk_ref[b, :, hh * CHUNK:(hh + 1) * CHUNK], vn)
        egl = jnp.exp(gl_ref[b, SUBLANES - 1:SUBLANES, DN_HEADS + hh:DN_HEADS + hh + 1])
        s_new = s * egl + _mm_tn(kd_ref[b, :, lanes(hh)], vn)
        s_scr[b, hh] = s_new
        s_out_ref[b, hh] = s_new
        yb_ref[b, :, lanes(hh)] = _rms(o, g_out) * _silu(z_ref[b, :, lanes(hh)])


DELTA_PREP_CHUNKS = 4
DELTA_SCAN_SEQS = 8


def _delta(q, k, v, z, bg, s0, g_out):
    b, l, _ = q.shape
    nc = l // CHUNK
    ncb = min(DELTA_PREP_CHUNKS, nc)
    sb = min(b, DELTA_PREP_CHUNKS // ncb)
    rows = ncb * CHUNK
    tok = lambda width: pl.BlockSpec((sb, rows, width), lambda i, j: (i, j, 0))
    glspec = pl.BlockSpec((sb, ncb * SUBLANES, LANES), lambda i, j: (i, j, 0))
    wide = lambda width: jax.ShapeDtypeStruct((b, l, width), F32)
    uv, wk, qd, kd, qk, gl = pl.pallas_call(
        _delta_prep_kernel,
        out_shape=(wide(DN_VW), wide(DN_QK), wide(DN_QK), wide(DN_QK), wide(DN_HEADS * CHUNK),
                   jax.ShapeDtypeStruct((b, nc * SUBLANES, LANES), F32)),
        grid=(b // sb, nc // ncb),
        in_specs=[tok(DN_QK), tok(DN_QK), tok(DN_VW), tok(LANES)],
        out_specs=(tok(DN_VW), tok(DN_QK), tok(DN_QK), tok(DN_QK), tok(DN_HEADS * CHUNK), glspec),
        compiler_params=pltpu.CompilerParams(dimension_semantics=("arbitrary", "arbitrary"), vmem_limit_bytes=VMEM_LIMIT),
        name="delta_prep",
    )(q, k, v, bg)

    bb = min(DELTA_SCAN_SEQS, b)
    ctok = lambda width: pl.BlockSpec((bb, CHUNK, width), lambda i, j: (i, j, 0))
    st = pl.BlockSpec((bb, DN_HEADS, DN_DK, DN_DV), lambda i, j: (i, 0, 0, 0))
    return pl.pallas_call(
        _delta_scan_kernel,
        out_shape=(wide(DN_VW), jax.ShapeDtypeStruct((b, DN_HEADS, DN_DK, DN_DV), F32)),
        grid=(b // bb, nc),
        in_specs=[ctok(DN_VW), ctok(DN_QK), ctok(DN_QK), ctok(DN_QK), ctok(DN_HEADS * CHUNK),
                  pl.BlockSpec((bb, SUBLANES, LANES), lambda i, j: (i, j, 0)), ctok(DN_VW), st,
                  pl.BlockSpec((1, DN_DV), lambda i, j: (0, 0))],
        out_specs=(ctok(DN_VW), st),
        scratch_shapes=[pltpu.VMEM((bb, DN_HEADS, DN_DK, DN_DV), F32)],
        compiler_params=pltpu.CompilerParams(dimension_semantics=("arbitrary", "arbitrary"), vmem_limit_bytes=VMEM_LIMIT),
        name="delta_scan",
    )(uv, wk, qd, kd, qk, gl, z, s0, g_out)


def _merge_kernel(x_ref, ya_ref, yb_ref, graw_ref, w_up_pool_ref, w_up_dn_ref, w_out_ref, g_ffn_ref, x2_ref, xn_ref):
    graw = graw_ref[...]
    ga = _sigmoid(graw[:, :D_MODEL])
    gb = _sigmoid(graw[:, D_MODEL:])
    merged = ga * _mm(ya_ref[...], w_up_pool_ref[...]) + gb * _mm(yb_ref[...], w_up_dn_ref[...])
    x2 = x_ref[...] + _mm(merged, w_out_ref[...])
    x2_ref[...] = x2
    xn_ref[...] = _rms(x2, g_ffn_ref[...])


def _merge(x, ya, yb, graw, w_up_pool, w_up_dn, w_out, g_ffn, tm):
    t = x.shape[0]
    tok = lambda width: pl.BlockSpec((tm, width), lambda i: (i, 0))
    full = lambda a: pl.BlockSpec(a.shape, lambda i: (0,) * a.ndim)
    weights = (w_up_pool, w_up_dn, w_out, g_ffn)
    return pl.pallas_call(
        _merge_kernel,
        out_shape=(jax.ShapeDtypeStruct((t, D_MODEL), F32), jax.ShapeDtypeStruct((t, D_MODEL), F32)),
        grid=(t // tm,),
        in_specs=[tok(D_MODEL), tok(D_POOL), tok(DN_VW), tok(2 * D_MODEL)] + [full(a) for a in weights],
        out_specs=(tok(D_MODEL), tok(D_MODEL)),
        compiler_params=pltpu.CompilerParams(dimension_semantics=("arbitrary",), vmem_limit_bytes=VMEM_LIMIT),
        name="branch_merge",
    )(x, ya, yb, graw, *weights)


def _top16_rows(s, ids=None):
    if ids is None:
        ids = lax.broadcasted_iota(jnp.int32, s.shape, 0)
    ids = ids.astype(F32)
    vals, idxs = [], []
    for _ in range(PEER_TOPK):
        m = jnp.max(s, axis=0, keepdims=True)
        idx = jnp.min(jnp.where(s == m, ids, jnp.inf), axis=0, keepdims=True)
        vals.append(m)
        idxs.append(idx)
        s = jnp.where(ids == idx, -jnp.inf, s)
    return jnp.concatenate(vals, axis=0), jnp.concatenate(idxs, axis=0).astype(jnp.int32)


def _pair_candidates(v1, v2):
    tokens = v1.shape[1]
    sub = lax.broadcasted_iota(jnp.int32, (SUBLANES, tokens), 0)
    vals = [v1[0:1] + v2[0:SUBLANES], v1[0:1] + v2[SUBLANES:]]
    ids = [sub, sub + SUBLANES]
    for a in range(1, SUBLANES):
        vals.append(v1[a:a + 1] + v2[0:SUBLANES])
        ids.append(sub + a * PEER_TOPK)
    vals.append(v1[SUBLANES:] + v2[0:1])
    ids.append((sub + SUBLANES) * PEER_TOPK)
    return jnp.concatenate(vals, axis=0), jnp.concatenate(ids, axis=0)


def _take_rows(table, idx):
    out = jnp.zeros_like(table)
    for a in range(PEER_TOPK):
        out = jnp.where(idx == a, table[a:a + 1, :], out)
    return out


def _route_kernel(xn_ref, wq_ref, keys_ref, experts_ref, gates_ref):
    half = PEER_DQ // 2
    q = jnp.dot(xn_ref[...].astype(BF16), wq_ref[...], preferred_element_type=F32).astype(BF16)
    experts, gates = [], []
    for hh in range(PEER_HEADS):
        q1 = q[:, hh * PEER_DQ:hh * PEER_DQ + half]
        q2 = q[:, hh * PEER_DQ + half:(hh + 1) * PEER_DQ]
        nt = (((1,), (1,)), ((), ()))
        s1 = lax.dot_general(keys_ref[0, hh], q1, nt, preferred_element_type=F32)
        s2 = lax.dot_general(keys_ref[1, hh], q2, nt, preferred_element_type=F32)
        v1, i1 = _top16_rows(s1)
        v2, i2 = _top16_rows(s2)
        cv, ci = _top16_rows(*_pair_candidates(v1, v2))
        e1 = _take_rows(i1, ci // PEER_TOPK)
        e2 = _take_rows(i2, ci % PEER_TOPK)
        experts.append(e1 * PEER_KEYS + e2)
        ex = jnp.exp(cv - cv[0:1, :])
        gates.append(ex / jnp.sum(ex, axis=0, keepdims=True))
    experts_ref[...] = jnp.concatenate(experts, axis=0).T
    gates_ref[...] = jnp.concatenate(gates, axis=0).T


def _route(xn, wq, keys, tm):
    t = xn.shape[0]
    return pl.pallas_call(
        _route_kernel,
        out_shape=(jax.ShapeDtypeStruct((t, PEER_PICKS), jnp.int32), jax.ShapeDtypeStruct((t, PEER_PICKS), F32)),
        grid=(t // tm,),
        in_specs=[pl.BlockSpec((tm, D_MODEL), lambda i: (i, 0)),
                  pl.BlockSpec(wq.shape, lambda i: (0, 0)),
                  pl.BlockSpec(keys.shape, lambda i: (0, 0, 0, 0))],
        out_specs=(pl.BlockSpec((tm, PEER_PICKS), lambda i: (i, 0)), pl.BlockSpec((tm, PEER_PICKS), lambda i: (i, 0))),
        compiler_params=pltpu.CompilerParams(dimension_semantics=("arbitrary",), vmem_limit_bytes=VMEM_LIMIT),
        name="peer_route",
    )(xn, wq, keys)


EXPERT_TOKENS = 32
EXPERT_ISSUE_TOKENS = 29


def _experts_kernel(idx_ref, idx_next_ref, gates_ref, xn_ref, x2_ref, g_final_ref, tab_ref, out_ref, buf_a, buf_b, cb, sem):
    bufs = (buf_a, buf_b)
    i = pl.program_id(0)
    n = pl.num_programs(0)
    tb = EXPERT_TOKENS
    lane = lax.broadcasted_iota(jnp.int32, (PEER_HEADS, PEER_TOPK), 1)
    g_final = g_final_ref[...]

    def issue_picks(ref, sl, lo, hi):
        for f in range(lo, min(hi, tb * PEER_PICKS)):
            t, h, k = f // PEER_PICKS, (f // PEER_TOPK) % PEER_HEADS, f % PEER_TOPK
            e = ref[0, t, h * PEER_TOPK + k]
            pltpu.async_copy(tab_ref.at[e], bufs[sl].at[t, k, h], sem.at[sl], priority=k % 2)

    def wait_slot(sl):
        for t in range(tb):
            pltpu.make_async_copy(bufs[sl].at[t], bufs[sl].at[t], sem.at[sl]).wait()

    def gate_coefficients(t, sl):
        x = xn_ref[t]
        sub = lax.broadcasted_iota(jnp.int32, (PEER_HEADS, PEER_TOPK), 0)
        act = jnp.zeros((PEER_HEADS, PEER_TOPK), F32)
        for h in range(PEER_HEADS):
            part = jnp.zeros((SUBLANES, PEER_TOPK), F32)
            for k in range(PEER_TOPK):
                u = lax.bitcast_convert_type(bufs[sl][t, k, h] << 16, F32)
                part = jnp.where(lane == k, jnp.sum(u * x, axis=-1, keepdims=True), part)
            act = jnp.where(sub == h, jnp.sum(part, axis=0, keepdims=True), act)
        coef = gates_ref[t] * (0.5 * act * (1.0 + lax.erf(act * (0.5 ** 0.5))))
        for k in range(PEER_TOPK):
            cb[t, k] = jnp.broadcast_to(coef[:, k:k + 1], (PEER_HEADS, LANES))

    def mix_token(t, sl):
        accs = [jnp.zeros((SUBLANES, LANES), F32) for _ in range(4)]
        for k in range(PEER_TOPK):
            for h in range(PEER_HEADS):
                c = jnp.broadcast_to(cb[t, k, h:h + 1, :], (SUBLANES, LANES))
                v = lax.bitcast_convert_type(bufs[sl][t, k, h] & jnp.uint32(0xFFFF0000), F32)
                accs[h % 4] = accs[h % 4] + c * v
        z = x2_ref[t] + ((accs[0] + accs[1]) + (accs[2] + accs[3]))
        ms = jnp.sum(jnp.sum(z * z, axis=-1, keepdims=True), axis=0, keepdims=True) * (1.0 / D_MODEL)
        out_ref[t] = z * lax.rsqrt(ms + EPS) * g_final

    @pl.when(i == 0)
    def _():
        issue_picks(idx_ref, 0, 0, tb * PEER_PICKS)

    per_token = -(-tb * PEER_PICKS // EXPERT_ISSUE_TOKENS)

    def step(sl):
        wait_slot(sl)
        gate_coefficients(0, sl)
        for t in range(tb):
            @pl.when(i + t >= 0)
            def _():
                issue_picks(idx_next_ref, 1 - sl, t * per_token, (t + 1) * per_token)
                if t + 1 < tb:
                    gate_coefficients(t + 1, sl)
                mix_token(t, sl)

        @pl.when(i == n - 1)
        def _():
            wait_slot(1 - sl)

    @pl.when(i % 2 == 0)
    def _():
        step(0)

    @pl.when(i % 2 == 1)
    def _():
        step(1)


def _experts(experts, gates, xn, x2, g_final, table):
    t = xn.shape[0]
    tb = EXPERT_TOKENS
    nb = t // tb
    idx = experts.reshape(nb, tb, PEER_PICKS)
    gates = gates.reshape(t, PEER_HEADS, PEER_TOPK)
    tiles = lambda a: a.reshape(a.shape[0], SUBLANES, LANES)
    tok = pl.BlockSpec((tb, SUBLANES, LANES), lambda i: (i, 0, 0))
    out = pl.pallas_call(
        _experts_kernel,
        out_shape=jax.ShapeDtypeStruct((t, SUBLANES, LANES), F32),
        grid=(nb,),
        in_specs=[pl.BlockSpec((1, tb, PEER_PICKS), lambda i: (i, 0, 0), memory_space=pltpu.SMEM),
                  pl.BlockSpec((1, tb, PEER_PICKS), lambda i: (jnp.minimum(i + 1, nb - 1), 0, 0), memory_space=pltpu.SMEM),
                  pl.BlockSpec((tb, PEER_HEADS, PEER_TOPK), lambda i: (i, 0, 0)), tok, tok,
                  pl.BlockSpec((SUBLANES, LANES), lambda i: (0, 0)),
                  pl.BlockSpec(memory_space=pl.ANY)],
        out_specs=tok,
        scratch_shapes=[pltpu.VMEM((tb, PEER_TOPK, PEER_HEADS, SUBLANES, LANES), jnp.uint32),
                        pltpu.VMEM((tb, PEER_TOPK, PEER_HEADS, SUBLANES, LANES), jnp.uint32),
                        pltpu.VMEM((tb, PEER_TOPK, PEER_HEADS, LANES), F32),
                        pltpu.SemaphoreType.DMA((2,))],
        compiler_params=pltpu.CompilerParams(dimension_semantics=("arbitrary",), vmem_limit_bytes=VMEM_LIMIT),
        name="peer_experts",
    )(idx, idx, gates, tiles(xn), tiles(x2), g_final.reshape(SUBLANES, LANES), table)
    return out.reshape(t, D_MODEL)


def _pack_rows(u, v):
    half = lambda a: lax.bitcast_convert_type(a.astype(BF16), jnp.uint16).astype(jnp.uint32)
    return (half(u) | (half(v) << 16)).reshape(-1, SUBLANES, LANES)


def _pad_rows_front(a, rows):
    return jnp.pad(a, ((0, 0), (rows - a.shape[1], 0), (0, 0)))


MERGE_ROWS = 512
ROUTE_ROWS = 256
FRONT_ROWS = 512


def _group(x, pool_prev, conv_prev, s_prev, pos0, p):
    b, l, _ = x.shape
    tl = min(FRONT_ROWS, l)
    sb = min(b, FRONT_ROWS // tl)
    tm_merge = min(MERGE_ROWS, b * l)
    tm_route = min(ROUTE_ROWS, b * l)
    ya, q, k, v, z, graw, bg, pool_tail, conv_tail = _front(
        x, _pad_rows_front(pool_prev, POOL_PREFIX_ROWS), _pad_rows_front(conv_prev, CONV_PREFIX_ROWS), pos0, sb, tl,
        p["g_mix"], p["w_u"], p["w_qkv"], p["w_z"], p["w_ba"], p["w_g"], p["w_grp"], p["pool_scale"], p["w_conv8"],
        p["a_log_pad"], p["dt_pad"])
    yb, s_new = _delta(q, k, v, z, bg, s_prev, p["g_dn_out"])
    t = b * l
    flat = lambda a: a.reshape(t, a.shape[-1])
    x2, xn = _merge(flat(x), flat(ya), flat(yb), flat(graw), p["w_up_pool"], p["w_up_dn"], p["w_out"], p["g_ffn"], tm_merge)
    experts, gates = _route(xn, p["w_peer_q"], p["peer_keys"], tm_route)
    y = _experts(experts, gates, xn, x2, p["g_final"], p["peer_table"])
    return (y.reshape(b, l, D_MODEL), pool_tail[:, POOL_PREFIX_ROWS - POOL_STATE:],
            conv_tail[:, CONV_PREFIX_ROWS - (CONV_W - 1):], s_new)


def kernel(x_prompt, x_sample, cache_pool, state_dn_conv, state_dn, g_mix, w_in, w_pool_grp, pool_scale, w_conv, a_log,
           dt_bias, g_dn_out, w_up_pool, w_up_dn, w_out, g_ffn, w_peer_q, peer_sub_keys, peer_u, peer_v, g_final):
    depth = w_in.shape[0]
    assert depth == 1
    bp = x_prompt.shape[0]
    lane_pad = lambda a, off: jnp.pad(a.astype(F32)[None, :], ((0, 0), (off, LANES - off - a.shape[0])))
    w = w_in[0]
    params = {
        "g_mix": g_mix[0][None, :],
        "w_u": w[:, OFF_U:OFF_QKV].astype(BF16),
        "w_qkv": w[:, OFF_QKV:OFF_Z].astype(BF16),
        "w_z": w[:, OFF_Z:OFF_B].astype(BF16),
        "w_ba": jnp.pad(w[:, OFF_B:OFF_G], ((0, 0), (0, LANES - 2 * DN_HEADS))).astype(BF16),
        "w_g": w[:, OFF_G:].astype(BF16),
        "w_grp": w_pool_grp[0].astype(BF16),
        "pool_scale": pool_scale[0][None, :],
        "w_conv8": jnp.pad(w_conv[0], ((0, SUBLANES - CONV_W), (0, 0))),
        "a_log_pad": lane_pad(a_log[0], DN_HEADS),
        "dt_pad": lane_pad(dt_bias[0], DN_HEADS),
        "g_dn_out": g_dn_out[0][None, :],
        "w_up_pool": w_up_pool[0].astype(BF16),
        "w_up_dn": w_up_dn[0].astype(BF16),
        "w_out": w_out[0].astype(BF16),
        "g_ffn": g_ffn[0][None, :],
        "w_peer_q": w_peer_q[0].reshape(D_MODEL, PEER_HEADS * PEER_DQ).astype(BF16),
        "peer_keys": peer_sub_keys[0].astype(BF16),
        "peer_table": _pack_rows(peer_u[0], peer_v[0]),
        "g_final": g_final[None, :],
    }
    zeros = lambda *shape: jnp.zeros(shape, F32)
    yp, pool_p, conv_p, dn_p = _group(
        x_prompt, zeros(bp, POOL_STATE, D_POOL), zeros(bp, CONV_W - 1, CONV_CH), zeros(bp, DN_HEADS, DN_DK, DN_DV),
        0, params)
    ys, pool_s, conv_s, dn_s = _group(
        x_sample, cache_pool[0], state_dn_conv[0], state_dn[0].astype(F32),
        PAST_LEN, params)
    return (yp, ys, pool_p[None], conv_p[None], dn_p[None].astype(state_dn.dtype),
            pool_s[None], conv_s[None], dn_s[None].astype(state_dn.dtype))
```
